```python
import jax, jax.numpy as jnp
from jax import lax
import numpy as np

D_MODEL = 1024
BATCH = 8
SEQ = 2048
DEPTH = 4
DEC_BATCH = 128
DEC_SEQ = 1
PAST_LEN = 16384
PAGE_SIZE = 128

N_META = 16
N_EVEN = (DEPTH + 1) // 2
N_ODD = DEPTH // 2
H_A = 4
DK_A = 128
DV_A = 256
RET_CHUNK = 128
ROPE_BASE = 10000.0
W_B = 1024
NB_B = 8
BS_B = W_B // NB_B
CONV_B = 4
LRU_C = 8.0
HS_C = 64
H_C = D_MODEL // HS_C
LORA_W = 64
LORA_A = 64
LORA_V = 32
LORA_G = 128
GN_EPS_C = 64e-5
D_FF = 2816
CONV_F = 3
EPS = 1e-6
IN_WIDTH = 2 * H_A * DK_A + 2 * H_A * DV_A + 2 * W_B
MIX_WIDTH = H_A * DV_A + W_B

kernel_name = 'hybrid_retention_rglru_rwkv7_convffn_step'


def rms_norm(x, g):
    xf = x.astype(jnp.float32)
    y = xf * lax.rsqrt(jnp.mean(xf * xf, axis=-1, keepdims=True) + EPS)
    return (y * g.astype(jnp.float32)).astype(x.dtype)


def causal_dwconv(x, buf, w, b):
    K = w.shape[0]
    T = x.shape[1]
    xc = jnp.concatenate([buf.astype(x.dtype), x], axis=1)
    y = b + sum(w[j] * xc[:, j:j + T] for j in range(K))
    return y, xc[:, T:]


def rotary(x, pos):
    half = x.shape[-1] // 2
    inv = ROPE_BASE ** (-jnp.linspace(0.0, 1.0, half, dtype=jnp.float32))
    ang = pos.astype(jnp.float32)[:, None] * inv[None, :]
    cos = jnp.cos(ang)[None, :, None, :]
    sin = jnp.sin(ang)[None, :, None, :]
    x1, x2 = x[..., :half], x[..., half:]
    return jnp.concatenate([x1 * cos - x2 * sin, x1 * sin + x2 * cos], axis=-1)


def retention_chunk(S, qkv, log_g):
    q, k, v = qkv
    C = q.shape[1]
    idx = jnp.arange(C, dtype=jnp.float32)
    diff = idx[:, None] - idx[None, :]
    mask = jnp.where(diff >= 0, jnp.exp(log_g[:, None, None] * jnp.maximum(diff, 0.0)[None]), 0.0)
    scores = jnp.einsum('bchd,bshd->bhcs', q, k) * mask[None]
    o = jnp.einsum('bhcs,bshe->bche', scores, v)
    dec_in = jnp.exp((idx[:, None] + 1.0) * log_g[None, :])
    o = o + jnp.einsum('bchd,bhde->bche', q, S) * dec_in[None, :, :, None]
    dec_k = jnp.exp((C - 1.0 - idx)[:, None] * log_g[None, :])
    S_new = jnp.exp(C * log_g)[None, :, None, None] * S + jnp.einsum('bshd,bshe->bhde', k * dec_k[None, :, :, None], v)
    return S_new, o


def retention_seq(q, k, v, S0, log_g, lead):
    outs = []
    S = S0
    if lead > 0:
        S, o = retention_chunk(S, (q[:, :lead], k[:, :lead], v[:, :lead]), log_g)
        outs.append(o)
    q, k, v = q[:, lead:], k[:, lead:], v[:, lead:]
    B, T = q.shape[0], q.shape[1]
    C = RET_CHUNK if T % RET_CHUNK == 0 else T
    N = T // C
    split = lambda t: jnp.moveaxis(t.reshape((B, N, C) + t.shape[2:]), 1, 0)
    S, o = lax.scan(lambda s, c: retention_chunk(s, c, log_g), S, (split(q), split(k), split(v)))
    outs.append(jnp.moveaxis(o, 0, 1).reshape(B, T, H_A, DV_A))
    return jnp.concatenate(outs, axis=1), S


def linear_combine(left, right):
    a1, b1 = left
    a2, b2 = right
    return a1 * a2, a2 * b1 + b2


def retention_lru_mixer(h, pos, lead, s_ret, s_lru, s_conv, p):
    B, T, _ = h.shape
    f32 = jnp.float32
    qk, vg = H_A * DK_A, H_A * DV_A
    z = h @ p['w_in']
    q, k, v, g_a, x_b, g_b = jnp.split(z, [qk, 2 * qk, 2 * qk + vg, 2 * qk + 2 * vg, 2 * qk + 2 * vg + W_B], axis=-1)
    log_g = jnp.log1p(-jnp.exp2(-5.0 - jnp.arange(H_A, dtype=f32)))
    q = rotary(q.reshape(B, T, H_A, DK_A).astype(f32), pos)
    k = rotary(k.reshape(B, T, H_A, DK_A).astype(f32), pos) * (DK_A ** -0.5)
    v = v.reshape(B, T, H_A, DV_A).astype(f32)
    o, s_ret_new = retention_seq(q, k, v, s_ret.astype(f32), log_g, lead)
    mu = jnp.mean(o, axis=-1, keepdims=True)
    var = jnp.mean(jnp.square(o - mu), axis=-1, keepdims=True)
    o = ((o - mu) * lax.rsqrt(var + EPS)).reshape(B, T, vg) * p['ret_gn'].astype(f32)
    y_a = jax.nn.silu(g_a.astype(f32)) * o
    xc, s_conv_new = causal_dwconv(x_b, s_conv, p['conv_w'], p['conv_b'])
    xg = xc.reshape(B, T, NB_B, BS_B)
    r = jax.nn.sigmoid((jnp.einsum('btgc,gcd->btgd', xg, p['wa']).reshape(B, T, W_B) + p['ba']).astype(f32))
    i = jax.nn.sigmoid((jnp.einsum('btgc,gcd->btgd', xg, p['wx']).reshape(B, T, W_B) + p['bx']).astype(f32))
    log_a = -LRU_C * r * jax.nn.softplus(-p['lam'].astype(f32))
    a = jnp.exp(log_a)
    u = jnp.sqrt(-jnp.expm1(2.0 * log_a)) * (i * xc.astype(f32))
    u = u.at[:, 0].add(a[:, 0] * s_lru.astype(f32))
    _, hs = lax.associative_scan(linear_combine, (a, u), axis=1)
    y_b = hs * jax.nn.gelu(g_b.astype(f32))
    out = jnp.concatenate([y_a, y_b], axis=-1).astype(h.dtype) @ p['w_out']
    return out, s_ret_new, hs[:, -1], s_conv_new


def rwkv7_step(S, inp):
    r_t, w_t, k_t, v_t, a_t, b_t = inp
    sa = jnp.einsum('bhij,bhj->bhi', S, a_t)
    S = S * w_t[:, :, None, :] + sa[..., None] * b_t[:, :, None, :] + v_t[..., None] * k_t[:, :, None, :]
    y = jnp.einsum('bhij,bhj->bhi', S, r_t)
    return S, y


def rwkv7_mixer(h, s0, shift0, v_first, p, vp):
    B, T, D = h.shape
    f32 = jnp.float32
    hprev = jnp.concatenate([shift0.astype(h.dtype)[:, None], h[:, :-1]], axis=1)
    xx = hprev - h
    xr, xw, xk, xv, xa, xg = (h + xx * p['mix'][n] for n in range(6))
    r = xr @ p['w_r']
    k = xk @ p['w_k']
    v = xv @ p['w_v']
    w_log = -jax.nn.softplus(-(p['w0'] + jnp.tanh(xw @ p['w1']) @ p['w2']).astype(f32)) - 0.5
    decay = jnp.exp(-jnp.exp(w_log))
    if vp is None:
        v_first = v
    else:
        v = v + (v_first - v) * jax.nn.sigmoid(vp['v0'] + (xv @ vp['v1']) @ vp['v2'])
    a = jax.nn.sigmoid(p['a0'] + (xa @ p['a1']) @ p['a2'])
    g = jax.nn.sigmoid(xg @ p['g1']) @ p['g2']
    heads = lambda t: t.reshape(B, T, H_C, HS_C).astype(f32)
    kk = heads(k * p['k_k'])
    kk = kk * lax.rsqrt(jnp.maximum(jnp.sum(kk * kk, axis=-1, keepdims=True), 1e-24))
    k = k * (1 + (a - 1) * p['k_a'])
    rh, kh, vh, ah = heads(r), heads(k), heads(v), heads(a)
    wh = decay.reshape(B, T, H_C, HS_C)
    tm = lambda t: jnp.moveaxis(t, 1, 0)
    s_new, ys = lax.scan(rwkv7_step, s0.astype(f32), (tm(rh), tm(wh), tm(kh), tm(vh), tm(-kk), tm(kk * ah)))
    ys = jnp.moveaxis(ys, 0, 1)
    mu = jnp.mean(ys, axis=-1, keepdims=True)
    var = jnp.mean(jnp.square(ys - mu), axis=-1, keepdims=True)
    o = ((ys - mu) * lax.rsqrt(var + GN_EPS_C)).reshape(B, T, D) * p['gn_g'].astype(f32) + p['gn_b'].astype(f32)
    bonus = jnp.sum(rh * kh * p['r_k'].astype(f32), axis=-1, keepdims=True) * vh
    o = (o + bonus.reshape(B, T, D)) * g.astype(f32)
    return o.astype(h.dtype) @ p['w_o'], s_new, h[:, -1], v_first


def conv_ffn(h, buf, p):
    u = h @ p['w_up']
    ug, uv = u[..., :D_FF], u[..., D_FF:]
    c, buf_new = causal_dwconv(ug, buf, p['conv_w'], p['conv_b'])
    return (jax.nn.gelu(c) * uv) @ p['w_down'], buf_new


def trunk(x, pos, lead, states, weights):
    st_ret, st_lru, st_lconv, st_rwkv, st_shift, st_ffn = states
    ev, od, vps, ff, norm_mix_g, norm_ffn_g, norm_final_g = weights
    n_ret, n_lru, n_lconv, n_rwkv, n_shift, n_ffn = [], [], [], [], [], []
    v_first = None
    for li in range(DEPTH):
        j = li // 2
        h = rms_norm(x, norm_mix_g[li])
        if li % 2 == 0:
            mix, s_ret, s_lru, s_lconv = retention_lru_mixer(h, pos, lead, st_ret[j], st_lru[j], st_lconv[j], ev[j])
            n_ret.append(s_ret)
            n_lru.append(s_lru)
            n_lconv.append(s_lconv)
        else:
            mix, s_rwkv, s_shift, v_first = rwkv7_mixer(h, st_rwkv[j], st_shift[j], v_first, od[j], vps[j])
            n_rwkv.append(s_rwkv)
            n_shift.append(s_shift)
        x = x + mix
        f, s_ffn = conv_ffn(rms_norm(x, norm_ffn_g[li]), st_ffn[li], ff[li])
        x = x + f
        n_ffn.append(s_ffn)
    stack = lambda lst, ref: jnp.stack(lst).astype(ref.dtype)
    return (rms_norm(x, norm_final_g), stack(n_ret, st_ret), stack(n_lru, st_lru), stack(n_lconv, st_lconv),
            stack(n_rwkv, st_rwkv), stack(n_shift, st_shift), stack(n_ffn, st_ffn))


def setup_inputs(seed: int = 0) -> dict:
    key = jax.random.key(seed)
    keys = iter(jax.random.split(key, 96))
    f32 = jnp.float32
    D = D_MODEL

    def nrm(shape, scale=1.0):
        return jax.random.normal(next(keys), shape, f32) * scale

    def unif(shape, lo, hi):
        return jax.random.uniform(next(keys), shape, f32, lo, hi)

    lam_a = unif((N_EVEN, W_B), 0.9, 0.999)
    return {
        'x_prompt': nrm((BATCH, SEQ, D)),
        'x_sample': nrm((DEC_BATCH, DEC_SEQ, D)),
        'state_ret': nrm((N_EVEN, DEC_BATCH, H_A, DK_A, DV_A), 0.1),
        'state_lru': nrm((N_EVEN, DEC_BATCH, W_B), 0.5),
        'state_lru_conv': nrm((N_EVEN, DEC_BATCH, CONV_B - 1, W_B)),
        'state_rwkv': nrm((N_ODD, DEC_BATCH, H_C, HS_C, HS_C), 0.1),
        'state_shift': nrm((N_ODD, DEC_BATCH, D)),
        'state_ffn_conv': nrm((DEPTH, DEC_BATCH, CONV_F - 1, D_FF)),
        'meta_tokens': nrm((N_META, D)),
        'norm_mix_g': 1.0 + nrm((DEPTH, D), 0.02),
        'norm_ffn_g': 1.0 + nrm((DEPTH, D), 0.02),
        'norm_final_g': 1.0 + nrm((D,), 0.02),
        'ev_w_in': nrm((N_EVEN, D, IN_WIDTH), D ** -0.5),
        'ev_ret_gn_g': 1.0 + nrm((N_EVEN, H_A * DV_A), 0.02),
        'ev_lru_conv_w': nrm((N_EVEN, CONV_B, W_B), CONV_B ** -0.5),
        'ev_lru_conv_b': nrm((N_EVEN, W_B), 0.01),
        'ev_lru_wa': nrm((N_EVEN, NB_B, BS_B, BS_B), BS_B ** -0.5),
        'ev_lru_ba': nrm((N_EVEN, W_B), 0.01),
        'ev_lru_wx': nrm((N_EVEN, NB_B, BS_B, BS_B), BS_B ** -0.5),
        'ev_lru_bx': nrm((N_EVEN, W_B), 0.01),
        'ev_lru_lambda': jnp.log(lam_a) - jnp.log1p(-lam_a),
        'ev_w_out': nrm((N_EVEN, MIX_WIDTH, D), MIX_WIDTH ** -0.5),
        'od_mix': unif((N_ODD, 6, D), 0.0, 1.0),
        'od_w_r': nrm((N_ODD, D, D), D ** -0.5),
        'od_w_k': nrm((N_ODD, D, D), D ** -0.5),
        'od_w_v': nrm((N_ODD, D, D), D ** -0.5),
        'od_w0': unif((N_ODD, D), -6.0, 1.0),
        'od_w1': nrm((N_ODD, D, LORA_W), D ** -0.5),
        'od_w2': nrm((N_ODD, LORA_W, D), 0.1 * LORA_W ** -0.5),
        'od_a0': nrm((N_ODD, D), 0.1),
        'od_a1': nrm((N_ODD, D, LORA_A), D ** -0.5),
        'od_a2': nrm((N_ODD, LORA_A, D), 0.1 * LORA_A ** -0.5),
        'od_v0': nrm((N_ODD - 1, D), 0.1),
        'od_v1': nrm((N_ODD - 1, D, LORA_V), D ** -0.5),
        'od_v2': nrm((N_ODD - 1, LORA_V, D), 0.1 * LORA_V ** -0.5),
        'od_g1': nrm((N_ODD, D, LORA_G), D ** -0.5),
        'od_g2': nrm((N_ODD, LORA_G, D), LORA_G ** -0.5),
        'od_k_k': 0.85 + nrm((N_ODD, D), 0.02),
        'od_k_a': 1.0 + nrm((N_ODD, D), 0.02),
        'od_r_k': nrm((N_ODD, H_C, HS_C), 0.1),
        'od_gn_g': 1.0 + nrm((N_ODD, D), 0.02),
        'od_gn_b': nrm((N_ODD, D), 0.01),
        'od_w_o': nrm((N_ODD, D, D), D ** -0.5),
        'ff_w_up': nrm((DEPTH, D, 2 * D_FF), D ** -0.5),
        'ff_conv_w': nrm((DEPTH, CONV_F, D_FF), CONV_F ** -0.5),
        'ff_conv_b': nrm((DEPTH, D_FF), 0.01),
        'ff_w_down': nrm((DEPTH, D_FF, D), D_FF ** -0.5),
    }


def reference(x_prompt, x_sample, state_ret, state_lru, state_lru_conv, state_rwkv, state_shift, state_ffn_conv,
              meta_tokens, norm_mix_g, norm_ffn_g, norm_final_g,
              ev_w_in, ev_ret_gn_g, ev_lru_conv_w, ev_lru_conv_b, ev_lru_wa, ev_lru_ba, ev_lru_wx, ev_lru_bx,
              ev_lru_lambda, ev_w_out,
              od_mix, od_w_r, od_w_k, od_w_v, od_w0, od_w1, od_w2, od_a0, od_a1, od_a2, od_v0, od_v1, od_v2,
              od_g1, od_g2, od_k_k, od_k_a, od_r_k, od_gn_g, od_gn_b, od_w_o,
              ff_w_up, ff_conv_w, ff_conv_b, ff_w_down):
    ev = [dict(w_in=ev_w_in[j], ret_gn=ev_ret_gn_g[j], conv_w=ev_lru_conv_w[j], conv_b=ev_lru_conv_b[j],
               wa=ev_lru_wa[j], ba=ev_lru_ba[j], wx=ev_lru_wx[j], bx=ev_lru_bx[j], lam=ev_lru_lambda[j],
               w_out=ev_w_out[j]) for j in range(N_EVEN)]
    od = [dict(mix=od_mix[j], w_r=od_w_r[j], w_k=od_w_k[j], w_v=od_w_v[j], w0=od_w0[j], w1=od_w1[j], w2=od_w2[j],
               a0=od_a0[j], a1=od_a1[j], a2=od_a2[j], g1=od_g1[j], g2=od_g2[j], k_k=od_k_k[j], k_a=od_k_a[j],
               r_k=od_r_k[j], gn_g=od_gn_g[j], gn_b=od_gn_b[j], w_o=od_w_o[j]) for j in range(N_ODD)]
    vps = [None] + [dict(v0=od_v0[j], v1=od_v1[j], v2=od_v2[j]) for j in range(N_ODD - 1)]
    ff = [dict(w_up=ff_w_up[l], conv_w=ff_conv_w[l], conv_b=ff_conv_b[l], w_down=ff_w_down[l]) for l in range(DEPTH)]
    weights = (ev, od, vps, ff, norm_mix_g, norm_ffn_g, norm_final_g)
    sample_states = (state_ret, state_lru, state_lru_conv, state_rwkv, state_shift, state_ffn_conv)

    bp = x_prompt.shape[0]
    meta = jnp.broadcast_to(meta_tokens.astype(x_prompt.dtype)[None], (bp, N_META, D_MODEL))
    xp = jnp.concatenate([meta, x_prompt], axis=1)
    pos_p = jnp.arange(xp.shape[1], dtype=jnp.int32)
    prompt_states = tuple(jnp.zeros((s.shape[0], bp) + s.shape[2:], x_prompt.dtype) for s in sample_states)
    yp, ret_p, lru_p, lru_conv_p, rwkv_p, shift_p, ffn_conv_p = trunk(xp, pos_p, N_META, prompt_states, weights)

    pos_s = PAST_LEN + jnp.arange(x_sample.shape[1], dtype=jnp.int32)
    ys, ret_s, lru_s, lru_conv_s, rwkv_s, shift_s, ffn_conv_s = trunk(x_sample, pos_s, 0, sample_states, weights)

    return (yp[:, N_META:], ys, ret_p, lru_p, lru_conv_p, rwkv_p, shift_p, ffn_conv_p,
            ret_s, lru_s, lru_conv_s, rwkv_s, shift_s, ffn_conv_s)
```

```python
import functools

import numpy as np
import jax
import jax.numpy as jnp
from jax import lax
from jax.experimental import pallas as pl
from jax.experimental.pallas import tpu as pltpu

F32 = jnp.float32
BF16 = jnp.bfloat16
HI = lax.Precision.HIGHEST

D_MODEL = 1024
N_META = 16
PAST_LEN = 16384
H_A, DK_A, DV_A = 4, 128, 256
W_B, NB_B, BS_B, CONV_B = 1024, 8, 128, 4
LRU_C = 8.0
HS_C = 64
N_PAIR = D_MODEL // (2 * HS_C)
LORA_PAD = 128
GN_EPS_C = 64e-5
D_FF, CONV_F = 2816, 3
EPS = 1e-6
ROPE_BASE = 10000.0
QK_W, VG_W = H_A * DK_A, H_A * DV_A

LANES = 128
TAIL = 8
VMEM_LIMIT = 52 * 1024 * 1024
PAD_T = 16
RWKV_SUB = 16


def _cparams(sem):
    return pltpu.CompilerParams(dimension_semantics=sem, vmem_limit_bytes=VMEM_LIMIT)


def _dot(a, b, prec=None):
    return jnp.dot(a, b, preferred_element_type=F32, precision=prec)


def _dot_nt(a, b, prec=None):
    return lax.dot_general(a, b, (((1,), (1,)), ((), ())), preferred_element_type=F32, precision=prec)


def _dot_tn(a, b, prec=None):
    return lax.dot_general(a, b, (((0,), (0,)), ((), ())), preferred_element_type=F32, precision=prec)


def _rms(x, g):
    return x * lax.rsqrt(jnp.mean(x * x, axis=-1, keepdims=True) + EPS) * g


def _log_sigmoid(x):
    return jnp.minimum(x, 0.0) - jnp.log(1.0 + jnp.exp(-jnp.abs(x)))


def _expm1(x):
    u = jnp.exp(x)
    near = jnp.where(u == 1.0, x, (u - 1.0) * x / jnp.log(jnp.where(u == 1.0, 2.0, u)))
    return jnp.where(jnp.abs(x) > 0.5, u - 1.0, near)


def _mm_body(*refs, n_in, norm, res):
    it = iter(refs)
    x_refs = [next(it) for _ in range(n_in)]
    g_ref = next(it) if norm else None
    w_refs = [next(it) for _ in range(n_in)]
    r_ref = next(it) if res else None
    o_ref = next(it)
    if norm:
        xn_ref = next(it)

        @pl.when(pl.program_id(1) == 0)
        def _():
            xn_ref[...] = _rms(x_refs[0][...], g_ref[...]).astype(BF16)

        acc = _dot(xn_ref[...], w_refs[0][...])
    else:
        acc = _dot(x_refs[0][...].astype(BF16), w_refs[0][...])
        for x_ref, w_ref in zip(x_refs[1:], w_refs[1:]):
            acc = acc + _dot(x_ref[...].astype(BF16), w_ref[...])
    if res:
        acc = acc + r_ref[...]
    o_ref[...] = acc.astype(o_ref.dtype)


def _mm(xs, ws, *, g=None, res=None, tm, tn, out_dtype=F32):
    R = xs[0].shape[0]
    N = ws[0].shape[1]
    norm = g is not None
    ins, specs = [], []
    for x in xs:
        ins.append(x)
        specs.append(pl.BlockSpec((tm, x.shape[1]), lambda i, j: (i, 0)))
    if norm:
        ins.append(g.reshape(1, -1))
        specs.append(pl.BlockSpec((1, g.shape[-1]), lambda i, j: (0, 0)))
    for w in ws:
        ins.append(w)
        specs.append(pl.BlockSpec((w.shape[0], tn), lambda i, j: (0, j)))
    if res is not None:
        ins.append(res)
        specs.append(pl.BlockSpec((tm, tn), lambda i, j: (i, j)))
    scratch = [pltpu.VMEM((tm, xs[0].shape[1]), BF16)] if norm else []
    return pl.pallas_call(
        functools.partial(_mm_body, n_in=len(xs), norm=norm, res=res is not None),
        grid=(R // tm, N // tn),
        in_specs=specs,
        out_specs=pl.BlockSpec((tm, tn), lambda i, j: (i, j)),
        out_shape=jax.ShapeDtypeStruct((R, N), out_dtype),
        scratch_shapes=scratch,
        compiler_params=_cparams(("arbitrary", "arbitrary")),
        name="mm",
    )(*ins)


def _final_norm_body(x_ref, g_ref, o_ref):
    o_ref[...] = _rms(x_ref[...], g_ref[...])


def _final_norm(x, g, tm):
    R = x.shape[0]
    return pl.pallas_call(
        _final_norm_body,
        grid=(R // tm,),
        in_specs=[pl.BlockSpec((tm, D_MODEL), lambda i: (i, 0)), pl.BlockSpec((1, D_MODEL), lambda i: (0, 0))],
        out_specs=pl.BlockSpec((tm, D_MODEL), lambda i: (i, 0)),
        out_shape=jax.ShapeDtypeStruct((R, D_MODEL), F32),
        compiler_params=_cparams(("arbitrary",)),
        name="final_norm",
    )(x, g.reshape(1, -1))


def _ret_body(q_ref, k_ref, v_ref, ga_ref, cos_ref, sin_ref, s0_ref, gn_ref, y_ref, s_ref, *, C, c_true):
    @pl.when(pl.program_id(1) == 0)
    def _():
        s_ref[...] = s0_ref[...]

    cosf, sinf = cos_ref[...], sin_ref[...]
    ti = lax.broadcasted_iota(jnp.int32, (C, C), 0)
    si = lax.broadcasted_iota(jnp.int32, (C, C), 1)
    dif = (ti - si).astype(F32)
    tcol = lax.broadcasted_iota(jnp.int32, (C, 1), 0).astype(F32)
    for h in range(H_A):
        lg = float(np.log1p(-(2.0 ** (-5.0 - h))))
        q = q_ref[:, h * DK_A:(h + 1) * DK_A]
        k = k_ref[:, h * DK_A:(h + 1) * DK_A]
        q = q * cosf + pltpu.roll(q, DK_A // 2, 1) * sinf
        k = (k * cosf + pltpu.roll(k, DK_A // 2, 1) * sinf) * (DK_A ** -0.5)
        vb = v_ref[:, h * DV_A:(h + 1) * DV_A].astype(BF16)
        qb = q.astype(BF16)
        mask = jnp.where(dif >= 0, jnp.exp(lg * jnp.maximum(dif, 0.0)), 0.0)
        sc = _dot_nt(qb, k.astype(BF16)) * mask
        S = s_ref[0, h]
        o = _dot(sc.astype(BF16), vb) + _dot(qb, S.astype(BF16)) * jnp.exp((tcol + 1.0) * lg)
        kd = (k * jnp.exp((c_true - 1.0 - tcol) * lg)).astype(BF16)
        s_ref[0, h] = float(np.exp(c_true * lg)) * S + _dot_tn(kd, vb)
        mu = jnp.mean(o, axis=-1, keepdims=True)
        d = o - mu
        var = jnp.mean(d * d, axis=-1, keepdims=True)
        on = d * lax.rsqrt(var + EPS) * gn_ref[:, h * DV_A:(h + 1) * DV_A]
        ga = ga_ref[:, h * DV_A:(h + 1) * DV_A]
        y_ref[:, h * DV_A:(h + 1) * DV_A] = (ga * jax.nn.sigmoid(ga) * on).astype(BF16)


def _retention(z, cosf, sinf, s0, gn, *, B, T, C, c_true, shared):
    NC = T // C
    smap = (lambda b, c: (0, 0, 0, 0)) if shared else (lambda b, c: (b, 0, 0, 0))
    row = lambda b, c: b * NC + c
    return pl.pallas_call(
        functools.partial(_ret_body, C=C, c_true=c_true),
        grid=(B, NC),
        in_specs=[
            pl.BlockSpec((C, QK_W), lambda b, c: (row(b, c), 0)),
            pl.BlockSpec((C, QK_W), lambda b, c: (row(b, c), 1)),
            pl.BlockSpec((C, VG_W), lambda b, c: (row(b, c), 1)),
            pl.BlockSpec((C, VG_W), lambda b, c: (row(b, c), 2)),
            pl.BlockSpec((C, DK_A), lambda b, c: (c, 0)),
            pl.BlockSpec((C, DK_A), lambda b, c: (c, 0)),
            pl.BlockSpec((1, H_A, DK_A, DV_A), smap),
            pl.BlockSpec((1, VG_W), lambda b, c: (0, 0)),
        ],
        out_specs=[
            pl.BlockSpec((C, VG_W), lambda b, c: (row(b, c), 0)),
            pl.BlockSpec((1, H_A, DK_A, DV_A), lambda b, c: (b, 0, 0, 0)),
        ],
        out_shape=[
            jax.ShapeDtypeStruct((B * T, VG_W), BF16),
            jax.ShapeDtypeStruct((B, H_A, DK_A, DV_A), F32),
        ],
        compiler_params=_cparams(("arbitrary", "arbitrary")),
        name="retention",
    )(z, z, z, z, cosf, sinf, s0, gn.reshape(1, -1))


def _rope_tables(pos):
    half = DK_A // 2
    inv = ROPE_BASE ** (-jnp.linspace(0.0, 1.0, half, dtype=F32))
    ang = pos.astype(F32)[:, None] * inv[None, :]
    cos, sin = jnp.cos(ang), jnp.sin(ang)
    return jnp.concatenate([cos, cos], axis=-1), jnp.concatenate([-sin, sin], axis=-1)


def _lru_gates(xc, wa_ref, ba, wx_ref, bx, lam):
    ra, ia = [], []
    for gi in range(NB_B):
        xg = xc[:, gi * BS_B:(gi + 1) * BS_B].astype(BF16)
        ra.append(_dot(xg, wa_ref[gi]))
        ia.append(_dot(xg, wx_ref[gi]))
    r = jax.nn.sigmoid(jnp.concatenate(ra, axis=1) + ba)
    i = jax.nn.sigmoid(jnp.concatenate(ia, axis=1) + bx)
    log_a = (-LRU_C) * r * (-_log_sigmoid(lam))
    a = jnp.exp(log_a)
    u = jnp.sqrt(-_expm1(2.0 * log_a)) * (i * xc)
    return a, u


def _lru_seq_body(xb_ref, gb_ref, cw_ref, cb_ref, wa_ref, ba_ref, wx_ref, bx_ref, lam_ref, h0_ref, tail0_ref,
                  y_ref, hout_ref, cout_ref, xe_ref, hc_ref, *, Tc):
    @pl.when(pl.program_id(1) == 0)
    def _():
        xe_ref[0:TAIL, :] = tail0_ref[0]
        hc_ref[...] = h0_ref[0]

    x = xb_ref[...]
    xe_ref[TAIL:TAIL + Tc, :] = x
    cw = cw_ref[...]
    xc = cb_ref[...]
    for j in range(CONV_B - 1):
        off = TAIL - (CONV_B - 1) + j
        xc = xc + cw[j:j + 1] * xe_ref[off:off + Tc, :]
    xc = xc + cw[CONV_B - 1:CONV_B] * x
    a, u = _lru_gates(xc, wa_ref, ba_ref[...], wx_ref, bx_ref[...], lam_ref[...])
    row = lax.broadcasted_iota(jnp.int32, (Tc, 1), 0)
    d = 1
    while d < Tc:
        a_sh = pltpu.roll(a, d, 0)
        u_sh = pltpu.roll(u, d, 0)
        valid = row >= d
        u = jnp.where(valid, a * u_sh + u, u)
        a = jnp.where(valid, a * a_sh, a)
        d *= 2
    hs = a * hc_ref[...] + u
    y_ref[...] = (hs * jax.nn.gelu(gb_ref[...])).astype(BF16)
    hc_ref[...] = hs[Tc - 1:Tc]
    hout_ref[0] = hs[Tc - 1:Tc]
    cout_ref[0] = xe_ref[Tc + TAIL - (CONV_B - 1):Tc + TAIL, :]
    xe_ref[0:TAIL, :] = xe_ref[Tc:Tc + TAIL, :]


def _tail_rows(state):
    return jnp.pad(state, ((0, 0), (TAIL - state.shape[1], 0), (0, 0)))


def _lru_seq(z, p, h0, conv0, *, B, T, Tc, shared):
    NC = T // Tc
    smap = (lambda b, c: (0, 0, 0)) if shared else (lambda b, c: (b, 0, 0))
    row = lambda b, c: b * NC + c
    vec = lambda a: pl.BlockSpec((1, W_B), lambda b, c: (0, 0))
    wspec = pl.BlockSpec((NB_B, BS_B, BS_B), lambda b, c: (0, 0, 0))
    return pl.pallas_call(
        functools.partial(_lru_seq_body, Tc=Tc),
        grid=(B, NC),
        in_specs=[
            pl.BlockSpec((Tc, W_B), lambda b, c: (row(b, c), 3)),
            pl.BlockSpec((Tc, W_B), lambda b, c: (row(b, c), 4)),
            pl.BlockSpec((CONV_B, W_B), lambda b, c: (0, 0)), vec(0),
            wspec, vec(0), wspec, vec(0), vec(0),
            pl.BlockSpec((1, 1, W_B), smap),
            pl.BlockSpec((1, TAIL, W_B), smap),
        ],
        out_specs=[
            pl.BlockSpec((Tc, W_B), lambda b, c: (row(b, c), 0)),
            pl.BlockSpec((1, 1, W_B), lambda b, c: (b, 0, 0)),
            pl.BlockSpec((1, CONV_B - 1, W_B), lambda b, c: (b, 0, 0)),
        ],
        out_shape=[
            jax.ShapeDtypeStruct((B * T, W_B), BF16),
            jax.ShapeDtypeStruct((B, 1, W_B), F32),
            jax.ShapeDtypeStruct((B, CONV_B - 1, W_B), F32),
        ],
        scratch_shapes=[pltpu.VMEM((Tc + TAIL, W_B), F32), pltpu.VMEM((1, W_B), F32)],
        compiler_params=_cparams(("arbitrary", "arbitrary")),
        name="lru_seq",
    )(z, z, p["conv_w"], p["conv_b"], p["wa"], p["ba"], p["wx"], p["bx"], p["lam"], h0, _tail_rows(conv0))


def _lru_dec_body(xb_ref, gb_ref, s0_ref, s1_ref, s2_ref, cw_ref, cb_ref, wa_ref, ba_ref, wx_ref, bx_ref, lam_ref,
                  h0_ref, y_ref, hout_ref):
    cw = cw_ref[...]
    x = xb_ref[...]
    xc = cb_ref[...] + cw[0:1] * s0_ref[...] + cw[1:2] * s1_ref[...] + cw[2:3] * s2_ref[...] + cw[3:4] * x
    a, u = _lru_gates(xc, wa_ref, ba_ref[...], wx_ref, bx_ref[...], lam_ref[...])
    hs = a * h0_ref[...] + u
    y_ref[...] = (hs * jax.nn.gelu(gb_ref[...])).astype(BF16)
    hout_ref[...] = hs


def _lru_dec(z, p, h0, conv0):
    B = z.shape[0]
    full = lambda shape: pl.BlockSpec(shape, lambda i: (0,) * len(shape))
    return pl.pallas_call(
        _lru_dec_body,
        grid=(1,),
        in_specs=[
            pl.BlockSpec((B, W_B), lambda i: (0, 3)), pl.BlockSpec((B, W_B), lambda i: (0, 4)),
            full((B, W_B)), full((B, W_B)), full((B, W_B)),
            full((CONV_B, W_B)), full((1, W_B)),
            full((NB_B, BS_B, BS_B)), full((1, W_B)), full((NB_B, BS_B, BS_B)), full((1, W_B)), full((1, W_B)),
            full((B, W_B)),
        ],
        out_specs=[full((B, W_B)), full((B, W_B))],
        out_shape=[jax.ShapeDtypeStruct((B, W_B), BF16), jax.ShapeDtypeStruct((B, W_B), F32)],
        compiler_params=_cparams(("arbitrary",)),
        name="lru_dec",
    )(z, z, conv0[:, 0], conv0[:, 1], conv0[:, 2], p["conv_w"], p["conv_b"], p["wa"], p["ba"], p["wx"], p["bx"],
      p["lam"], h0)


def _ffn_seq_body(ug_ref, uv_ref, cw_ref, cb_ref, tail0_ref, m_ref, cout_ref, xe_ref, *, Tc):
    @pl.when(pl.program_id(1) == 0)
    def _():
        xe_ref[0:TAIL, :] = tail0_ref[0]

    x = ug_ref[...]
    xe_ref[TAIL:TAIL + Tc, :] = x
    cw = cw_ref[...]
    c = cb_ref[...]
    for j in range(CONV_F - 1):
        off = TAIL - (CONV_F - 1) + j
        c = c + cw[j:j + 1] * xe_ref[off:off + Tc, :]
    c = c + cw[CONV_F - 1:CONV_F] * x
    m_ref[...] = (jax.nn.gelu(c) * uv_ref[...]).astype(BF16)
    cout_ref[0] = xe_ref[Tc + TAIL - (CONV_F - 1):Tc + TAIL, :]
    xe_ref[0:TAIL, :] = xe_ref[Tc:Tc + TAIL, :]


def _ffn_seq(u, cw, cb, conv0, *, B, T, Tc, shared):
    NC = T // Tc
    smap = (lambda b, c: (0, 0, 0)) if shared else (lambda b, c: (b, 0, 0))
    row = lambda b, c: b * NC + c
    return pl.pallas_call(
        functools.partial(_ffn_seq_body, Tc=Tc),
        grid=(B, NC),
        in_specs=[
            pl.BlockSpec((Tc, D_FF), lambda b, c: (row(b, c), 0)),
            pl.BlockSpec((Tc, D_FF), lambda b, c: (row(b, c), 1)),
            pl.BlockSpec((CONV_F, D_FF), lambda b, c: (0, 0)),
            pl.BlockSpec((1, D_FF), lambda b, c: (0, 0)),
            pl.BlockSpec((1, TAIL, D_FF), smap),
        ],
        out_specs=[
            pl.BlockSpec((Tc, D_FF), lambda b, c: (row(b, c), 0)),
            pl.BlockSpec((1, CONV_F - 1, D_FF), lambda b, c: (b, 0, 0)),
        ],
        out_shape=[
            jax.ShapeDtypeStruct((B * T, D_FF), BF16),
            jax.ShapeDtypeStruct((B, CONV_F - 1, D_FF), F32),
        ],
        scratch_shapes=[pltpu.VMEM((Tc + TAIL, D_FF), F32)],
        compiler_params=_cparams(("arbitrary", "arbitrary")),
        name="ffn_seq",
    )(u, u, cw, cb, _tail_rows(conv0))


def _ffn_dec_body(ug_ref, uv_ref, s0_ref, s1_ref, cw_ref, cb_ref, m_ref):
    cw = cw_ref[...]
    c = cb_ref[...] + cw[0:1] * s0_ref[...] + cw[1:2] * s1_ref[...] + cw[2:3] * ug_ref[...]
    m_ref[...] = (jax.nn.gelu(c) * uv_ref[...]).astype(BF16)


def _ffn_dec(u, cw, cb, conv0):
    B = u.shape[0]
    full = lambda shape: pl.BlockSpec(shape, lambda i: (0,) * len(shape))
    return pl.pallas_call(
        _ffn_dec_body,
        grid=(1,),
        in_specs=[pl.BlockSpec((B, D_FF), lambda i: (0, 0)), pl.BlockSpec((B, D_FF), lambda i: (0, 1)),
                  full((B, D_FF)), full((B, D_FF)), full((CONV_F, D_FF)), full((1, D_FF))],
        out_specs=full((B, D_FF)),
        out_shape=jax.ShapeDtypeStruct((B, D_FF), BF16),
        compiler_params=_cparams(("arbitrary",)),
        name="ffn_dec",
    )(u, u, conv0[:, 0], conv0[:, 1], cw, cb)


_PREP_W = ("mix", "w_r", "w_k", "w_v", "w0", "w1", "w2", "a0", "a1", "a2", "g1", "g2")
_PREP_VP = ("v0", "v1", "v2")


def _rwkv_prep_body(*refs, has_vp, carry):
    it = iter(refs)
    x_ref, gn_ref = next(it), next(it)
    p = {n: next(it) for n in _PREP_W}
    if has_vp:
        p.update({n: next(it) for n in _PREP_VP})
        vf_ref = next(it)
    prev_ref = next(it)
    r_ref, lw_ref, k_ref, v_ref, as_ref, g_ref, hn_ref = (next(it) for _ in range(7))
    h = _rms(x_ref[...], gn_ref[...])
    if carry:
        carry_ref = next(it)

        @pl.when(pl.program_id(1) == 0)
        def _():
            carry_ref[...] = prev_ref[0]

        row = lax.broadcasted_iota(jnp.int32, (h.shape[0], 1), 0)
        hprev = jnp.where(row == 0, carry_ref[...], pltpu.roll(h, 1, 0))
        carry_ref[...] = h[h.shape[0] - 1:]
        hn_ref[0] = h[h.shape[0] - 1:]
    else:
        hprev = prev_ref[...]
        hn_ref[...] = h
    xx = hprev - h
    mix = p["mix"][...]
    xs = [(h + xx * mix[n:n + 1]).astype(BF16) for n in range(6)]
    xr, xw, xk, xv, xa, xg = xs
    r_ref[...] = _dot(xr, p["w_r"][...])
    k_ref[...] = _dot(xk, p["w_k"][...])
    v = _dot(xv, p["w_v"][...])
    wl = p["w0"][...] + _dot(jnp.tanh(_dot(xw, p["w1"][...])).astype(BF16), p["w2"][...])
    lw_ref[...] = -jnp.exp(_log_sigmoid(wl) - 0.5)
    if has_vp:
        gate = jax.nn.sigmoid(p["v0"][...] + _dot(_dot(xv, p["v1"][...]).astype(BF16), p["v2"][...]))
        v = v + (vf_ref[...] - v) * gate
    v_ref[...] = v
    as_ref[...] = jax.nn.sigmoid(p["a0"][...] + _dot(_dot(xa, p["a1"][...]).astype(BF16), p["a2"][...]))
    g_ref[...] = _dot(jax.nn.sigmoid(_dot(xg, p["g1"][...])).astype(BF16), p["g2"][...])


def _rwkv_prep(x, gn, p, vp, v_first, prev, *, B, T, tm, carry, shared=False):
    R = x.shape[0]
    has_vp = vp is not None
    if carry:
        NT = T // tm
        grid = (B, NT)
        rowmap = lambda b, t: (b * NT + t, 0)
        cmap2 = lambda b, t: (0, 0)
        pmap = (lambda b, t: (0, 0, 0)) if shared else (lambda b, t: (b, 0, 0))
        prev_spec = pl.BlockSpec((1, 1, D_MODEL), pmap)
        hn_spec = pl.BlockSpec((1, 1, D_MODEL), lambda b, t: (b, 0, 0))
        hn_shape = jax.ShapeDtypeStruct((B, 1, D_MODEL), F32)
        sem = ("arbitrary", "arbitrary")
    else:
        grid = (R // tm,)
        rowmap = lambda i: (i, 0)
        cmap2 = lambda i: (0, 0)
        prev_spec = pl.BlockSpec((tm, D_MODEL), rowmap)
        hn_spec = pl.BlockSpec((tm, D_MODEL), rowmap)
        hn_shape = jax.ShapeDtypeStruct((R, D_MODEL), F32)
        sem = ("arbitrary",)
    tile = pl.BlockSpec((tm, D_MODEL), rowmap)
    ins = [x, gn.reshape(1, -1)]
    specs = [tile, pl.BlockSpec((1, D_MODEL), cmap2)]
    names = _PREP_W + (_PREP_VP if has_vp else ())
    src = dict(p)
    if has_vp:
        src.update(vp)
    for n in names:
        ins.append(src[n])
        specs.append(pl.BlockSpec(src[n].shape, cmap2))
    if has_vp:
        ins.append(v_first)
        specs.append(tile)
    ins.append(prev)
    specs.append(prev_spec)
    outs = pl.pallas_call(
        functools.partial(_rwkv_prep_body, has_vp=has_vp, carry=carry),
        grid=grid,
        in_specs=specs,
        out_specs=[tile] * 6 + [hn_spec],
        out_shape=[jax.ShapeDtypeStruct((R, D_MODEL), F32)] * 6 + [hn_shape],
        scratch_shapes=[pltpu.VMEM((1, D_MODEL), F32)] if carry else [],
        compiler_params=_cparams(sem),
        name="rwkv_prep",
    )(*ins)
    return outs


def _rwkv_chunk_body(r_ref, lw_ref, k_ref, v_ref, as_ref, g_ref, kk_ref, ka_ref, rk_ref, gg_ref, gb_ref, s0_ref,
                     o_ref, s_ref, *, Tc, C, c_true, n_pair):
    @pl.when(pl.program_id(2) == 0)
    def _():
        s_ref[...] = s0_ref[...]

    C2 = 2 * C
    lane = lax.broadcasted_iota(jnp.int32, (1, LANES), 1)
    m0 = (lane < HS_C).astype(F32)
    m1 = 1.0 - m0
    li = lax.broadcasted_iota(jnp.int32, (LANES, LANES), 0)
    lj = lax.broadcasted_iota(jnp.int32, (LANES, LANES), 1)
    same_head = (li >= HS_C) == (lj >= HS_C)
    bd_ones = same_head.astype(F32)
    bd_avg = bd_ones * (1.0 / HS_C)
    ti = lax.broadcasted_iota(jnp.int32, (C, C), 0)
    si = lax.broadcasted_iota(jnp.int32, (C, C), 1)
    tri = (ti >= si).astype(F32)
    r2 = lax.broadcasted_iota(jnp.int32, (C2, C2), 0)
    c2 = lax.broadcasted_iota(jnp.int32, (C2, C2), 1)
    same = (r2 >= C) == (c2 >= C)
    tt, ss = r2 & (C - 1), c2 & (C - 1)
    m_strict = same & (tt > ss)
    m_incl = same & (tt >= ss)
    same_sub = (tt & -RWKV_SUB) == (ss & -RWKV_SUB)
    m_sub = m_strict & same_sub
    m_off = m_strict & jnp.logical_not(same_sub)
    eye = (r2 == c2).astype(F32)
    nb = C // RWKV_SUB
    stack = lambda x: jnp.concatenate([x * m0, x * m1], axis=0)
    dup = lambda x: jnp.concatenate([x, x], axis=0)

    def sub_chunk(ci, carry):
        rows = pl.ds(pl.multiple_of(ci * C, C), C)
        for p in range(n_pair):
            ls = slice(p * LANES, (p + 1) * LANES)
            r, lw, k, v = r_ref[rows, ls], lw_ref[rows, ls], k_ref[rows, ls], v_ref[rows, ls]
            asig, g = as_ref[rows, ls], g_ref[rows, ls]
            kk = k * kk_ref[:, ls]
            kk = kk * lax.rsqrt(jnp.maximum(_dot(kk * kk, bd_ones, HI), 1e-24))
            k2 = k * (1.0 + (asig - 1.0) * ka_ref[:, ls])
            a, b = -kk, kk * asig
            cs = _dot(tri, lw, HI)
            cend = cs[C - 1:C]
            e_neg = jnp.exp(-cs)
            e_end = jnp.exp(cend - cs)
            As = stack(a * jnp.exp(cs - lw))
            Rs = stack(r * jnp.exp(cs))
            Vs = stack(v)
            Bd, Kd = dup(b * e_neg), dup(k2 * e_neg)
            G = _dot_nt(As, Bd, HI)
            Nd = jnp.where(m_sub, G, 0.0)
            Aak = jnp.where(m_strict, _dot_nt(As, Kd, HI), 0.0)
            Arb = jnp.where(m_incl, _dot_nt(Rs, Bd, HI), 0.0)
            Ark = jnp.where(m_incl, _dot_nt(Rs, Kd, HI), 0.0)
            P = eye + Nd
            Q = Nd
            for _ in range(3):
                Q = _dot(Q, Q, HI)
                P = P + _dot(P, Q, HI)
            if nb > 1:
                M = _dot(P, jnp.where(m_off, G, 0.0), HI)
                Tm = eye + M
                for _ in range(int(np.ceil(np.log2(nb))) - 1):
                    M = _dot(M, M, HI)
                    Tm = Tm + _dot(Tm, M, HI)
                Tinv = _dot(Tm, P, HI)
            else:
                Tinv = P
            S = s_ref[0, p]
            X = _dot_nt(As, S, HI) + _dot(Aak, Vs, HI)
            Us = _dot(Tinv, X, HI)
            Ys = _dot_nt(Rs, S, HI) + _dot(Arb, Us, HI) + _dot(Ark, Vs, HI)
            y = Ys[:C] + Ys[C:]
            s_ref[0, p] = (S * jnp.exp(cend) + _dot_tn(Us, stack(b * e_end), HI)
                           + _dot_tn(Vs, stack(k2 * e_end), HI))
            mu = _dot(y, bd_avg, HI)
            d = y - mu
            var = _dot(d * d, bd_avg, HI)
            on = d * lax.rsqrt(var + GN_EPS_C) * gg_ref[:, ls] + gb_ref[:, ls]
            bonus = _dot(r * k2 * rk_ref[:, ls], bd_ones, HI) * v
            o_ref[rows, ls] = ((on + bonus) * g).astype(BF16)
        return carry

    lax.fori_loop(0, Tc // C, sub_chunk, 0)


def _rwkv_chunk(arrs, p, s0, *, B, T, Tc, C, ppb, shared):
    NC = T // Tc
    NP = N_PAIR // ppb
    Wb = ppb * LANES
    tile = pl.BlockSpec((Tc, Wb), lambda b, q, c: (b * NC + c, q))
    vec = pl.BlockSpec((1, Wb), lambda b, q, c: (0, q))
    smap = (lambda b, q, c: (0, q, 0, 0)) if shared else (lambda b, q, c: (b, q, 0, 0))
    return pl.pallas_call(
        functools.partial(_rwkv_chunk_body, Tc=Tc, C=C, c_true=None, n_pair=ppb),
        grid=(B, NP, NC),
        in_specs=[tile] * 6 + [vec] * 5 + [pl.BlockSpec((1, ppb, LANES, LANES), smap)],
        out_specs=[tile, pl.BlockSpec((1, ppb, LANES, LANES), lambda b, q, c: (b, q, 0, 0))],
        out_shape=[jax.ShapeDtypeStruct((B * T, D_MODEL), BF16),
                   jax.ShapeDtypeStruct((B, N_PAIR, LANES, LANES), F32)],
        compiler_params=_cparams(("arbitrary", "arbitrary", "arbitrary")),
        name="rwkv_chunk",
    )(*arrs, p["k_k"], p["k_a"], p["r_k"], p["gn_g"], p["gn_b"], s0)


def _to_blockdiag(s):
    even = jnp.pad(s[:, 0::2], ((0, 0), (0, 0), (0, 0), (0, HS_C)))
    odd = jnp.pad(s[:, 1::2], ((0, 0), (0, 0), (0, 0), (HS_C, 0)))
    return jnp.concatenate([even, odd], axis=2)


def _from_blockdiag(s):
    B = s.shape[0]
    return jnp.stack([s[:, :, :HS_C, :HS_C], s[:, :, HS_C:, HS_C:]], axis=2).reshape(B, 2 * N_PAIR, HS_C, HS_C)


def _pad_rows(x):
    B, W = x.shape
    return jnp.pad(x[:, None, :], ((0, 0), (0, PAD_T - 1), (0, 0))).reshape(B * PAD_T, W)


def _first_rows(x):
    return x.reshape(-1, PAD_T, x.shape[-1])[:, 0]


MAIN_TM = 1024
MAIN_TN = 512
RET_C = 128
LRU_TC = 256
FFN_TC = 256
PREP_TM = 256
RWKV_TC = 256
RWKV_C = 64
RWKV_PPB = 4


def kernel(x_prompt, x_sample, state_ret, state_lru, state_lru_conv, state_rwkv, state_shift, state_ffn_conv,
           meta_tokens, norm_mix_g, norm_ffn_g, norm_final_g,
           ev_w_in, ev_ret_gn_g, ev_lru_conv_w, ev_lru_conv_b, ev_lru_wa, ev_lru_ba, ev_lru_wx, ev_lru_bx,
           ev_lru_lambda, ev_w_out,
           od_mix, od_w_r, od_w_k, od_w_v, od_w0, od_w1, od_w2, od_a0, od_a1, od_a2, od_v0, od_v1, od_v2,
           od_g1, od_g2, od_k_k, od_k_a, od_r_k, od_gn_g, od_gn_b, od_w_o,
           ff_w_up, ff_conv_w, ff_conv_b, ff_w_down):
    BP, TP, _ = x_prompt.shape
    BS = x_sample.shape[0]
    NS = BS + N_META
    bf = lambda a: a.astype(BF16)
    row = lambda a: a.reshape(1, -1)

    def lora_in(w):
        return bf(jnp.pad(w, ((0, 0), (0, LORA_PAD - w.shape[1]))))

    def lora_out(w):
        return bf(jnp.pad(w, ((0, LORA_PAD - w.shape[0]), (0, 0))))

    xm = x_prompt.reshape(BP * TP, D_MODEL)
    xs = jnp.concatenate([x_sample.reshape(BS, D_MODEL), meta_tokens.astype(x_prompt.dtype)], axis=0)

    cos_m, sin_m = _rope_tables(N_META + jnp.arange(TP, dtype=jnp.int32))
    cos_t, sin_t = _rope_tables(jnp.arange(N_META, dtype=jnp.int32))
    cos_s, sin_s = _rope_tables(PAST_LEN + jnp.arange(PAD_T, dtype=jnp.int32))

    out = {k: [] for k in ("ret_p", "lru_p", "lconv_p", "rwkv_p", "shift_p", "ffn_p",
                           "ret_s", "lru_s", "lconv_s", "rwkv_s", "shift_s", "ffn_s")}
    v_first_m = v_first_s = None
    for li in range(4):
        j = li // 2
        if li % 2 == 0:
            w_in = bf(ev_w_in[j])
            w_out_a, w_out_b = bf(ev_w_out[j][:VG_W]), bf(ev_w_out[j][VG_W:])
            gn = ev_ret_gn_g[j]
            lp = dict(conv_w=ev_lru_conv_w[j], conv_b=row(ev_lru_conv_b[j]), wa=bf(ev_lru_wa[j]),
                      ba=row(ev_lru_ba[j]), wx=bf(ev_lru_wx[j]), bx=row(ev_lru_bx[j]), lam=row(ev_lru_lambda[j]))
            zs = _mm([xs], [w_in], g=norm_mix_g[li], tm=NS, tn=MAIN_TN)
            z_smp, z_meta = zs[:BS], zs[BS:]
            ya_t, ret_t = _retention(z_meta, cos_t, sin_t, jnp.zeros((1, H_A, DK_A, DV_A), F32), gn,
                                     B=1, T=N_META, C=N_META, c_true=N_META, shared=False)
            yb_t, lru_t, lconv_t = _lru_seq(z_meta, lp, jnp.zeros((1, 1, W_B), F32),
                                            jnp.zeros((1, CONV_B - 1, W_B), F32), B=1, T=N_META, Tc=N_META,
                                            shared=False)
            ya_s, ret_s = _retention(_pad_rows(z_smp), cos_s, sin_s, state_ret[j], gn,
                                     B=BS, T=PAD_T, C=PAD_T, c_true=1, shared=False)
            yb_s, lru_s = _lru_dec(z_smp, lp, state_lru[j], state_lru_conv[j])
            lconv_s = jnp.concatenate([state_lru_conv[j][:, 1:], z_smp[:, None, 3 * W_B:4 * W_B]], axis=1)
            ya = jnp.concatenate([_first_rows(ya_s), ya_t], axis=0)
            yb = jnp.concatenate([yb_s, yb_t], axis=0)
            xs = _mm([ya, yb], [w_out_a, w_out_b], res=xs, tm=NS, tn=MAIN_TN)
            zm = _mm([xm], [w_in], g=norm_mix_g[li], tm=MAIN_TM, tn=MAIN_TN)
            ya_m, ret_m = _retention(zm, cos_m, sin_m, ret_t, gn, B=BP, T=TP, C=RET_C, c_true=RET_C, shared=True)
            yb_m, lru_m, lconv_m = _lru_seq(zm, lp, lru_t, lconv_t, B=BP, T=TP, Tc=LRU_TC, shared=True)
            xm = _mm([ya_m, yb_m], [w_out_a, w_out_b], res=xm, tm=MAIN_TM, tn=MAIN_TN)
            out["ret_p"].append(ret_m)
            out["lru_p"].append(lru_m[:, 0])
            out["lconv_p"].append(lconv_m)
            out["ret_s"].append(ret_s)
            out["lru_s"].append(lru_s)
            out["lconv_s"].append(lconv_s)
        else:
            p = dict(mix=od_mix[j], w_r=bf(od_w_r[j]), w_k=bf(od_w_k[j]), w_v=bf(od_w_v[j]), w0=row(od_w0[j]),
                     w1=lora_in(od_w1[j]), w2=lora_out(od_w2[j]), a0=row(od_a0[j]), a1=lora_in(od_a1[j]),
                     a2=lora_out(od_a2[j]), g1=lora_in(od_g1[j]), g2=lora_out(od_g2[j]),
                     k_k=row(od_k_k[j]), k_a=row(od_k_a[j]), r_k=row(od_r_k[j]), gn_g=row(od_gn_g[j]),
                     gn_b=row(od_gn_b[j]))
            vp = None
            if j > 0:
                vp = dict(v0=row(od_v0[j - 1]), v1=lora_in(od_v1[j - 1]), v2=lora_out(od_v2[j - 1]))
            w_o = bf(od_w_o[j])
            gmix = norm_mix_g[li]
            pre_t = _rwkv_prep(xs[BS:], gmix, p, vp, None if vp is None else v_first_s[BS:],
                               jnp.zeros((1, 1, D_MODEL), F32), B=1, T=N_META, tm=N_META, carry=True)
            o_t, rw_t = _rwkv_chunk(pre_t[:6], p, jnp.zeros((1, N_PAIR, LANES, LANES), F32),
                                    B=1, T=N_META, Tc=N_META, C=N_META, ppb=RWKV_PPB, shared=False)
            shift_t = pre_t[6]
            pre_s = _rwkv_prep(xs[:BS], gmix, p, vp, None if vp is None else v_first_s[:BS],
                               state_shift[j], B=BS, T=1, tm=BS, carry=False)
            o_s, rw_s = _rwkv_chunk([_pad_rows(a) for a in pre_s[:6]], p, _to_blockdiag(state_rwkv[j]),
                                    B=BS, T=PAD_T, Tc=PAD_T, C=PAD_T, ppb=RWKV_PPB, shared=False)
            if vp is None:
                v_first_s = jnp.concatenate([pre_s[3], pre_t[3]], axis=0)
            o_small = jnp.concatenate([_first_rows(o_s), o_t], axis=0)
            xs = _mm([o_small], [w_o], res=xs, tm=NS, tn=MAIN_TN)
            pre_m = _rwkv_prep(xm, gmix, p, vp, v_first_m, shift_t, B=BP, T=TP, tm=PREP_TM, carry=True, shared=True)
            if vp is None:
                v_first_m = pre_m[3]
            o_m, rw_m = _rwkv_chunk(pre_m[:6], p, rw_t, B=BP, T=TP, Tc=RWKV_TC, C=RWKV_C, ppb=RWKV_PPB, shared=True)
            xm = _mm([o_m], [w_o], res=xm, tm=MAIN_TM, tn=MAIN_TN)
            out["rwkv_p"].append(_from_blockdiag(rw_m))
            out["shift_p"].append(pre_m[6][:, 0])
            out["rwkv_s"].append(_from_blockdiag(rw_s))
            out["shift_s"].append(pre_s[6])
        w_up, w_down = bf(ff_w_up[li]), bf(ff_w_down[li])
        cw, cb = ff_conv_w[li], row(ff_conv_b[li])
        us = _mm([xs], [w_up], g=norm_ffn_g[li], tm=NS, tn=MAIN_TN)
        m_t, ffn_t = _ffn_seq(us[BS:], cw, cb, jnp.zeros((1, CONV_F - 1, D_FF), F32), B=1, T=N_META, Tc=N_META,
                              shared=False)
        m_s = _ffn_dec(us[:BS], cw, cb, state_ffn_conv[li])
        ffn_s = jnp.concatenate([state_ffn_conv[li][:, 1:], us[:BS, None, :D_FF]], axis=1)
        xs = _mm([jnp.concatenate([m_s, m_t], axis=0)], [w_down], res=xs, tm=NS, tn=MAIN_TN)
        um = _mm([xm], [w_up], g=norm_ffn_g[li], tm=MAIN_TM, tn=MAIN_TN)
        m_m, ffn_m = _ffn_seq(um, cw, cb, ffn_t, B=BP, T=TP, Tc=FFN_TC, shared=True)
        xm = _mm([m_m], [w_down], res=xm, tm=MAIN_TM, tn=MAIN_TN)
        out["ffn_p"].append(ffn_m)
        out["ffn_s"].append(ffn_s)

    y_prompt = _final_norm(xm, norm_final_g, MAIN_TM).reshape(BP, TP, D_MODEL)
    y_sample = _final_norm(xs, norm_final_g, NS)[:BS].reshape(BS, 1, D_MODEL)
    st = lambda k: jnp.stack(out[k])
    return (y_prompt, y_sample,
            st("ret_p"), st("lru_p"), st("lconv_p"), st("rwkv_p"), st("shift_p"), st("ffn_p"),
            st("ret_s"), st("lru_s"), st("lconv_s"), st("rwkv_s"), st("shift_s"), st("ffn_s"))
```

```python
import functools

import numpy as np
import jax
import jax.numpy as jnp
from jax import lax
from jax.experimental import pallas as pl
from jax.experimental.pallas import tpu as pltpu

F32 = jnp.float32
BF16 = jnp.bfloat16
HI = lax.Precision.HIGHEST

D_MODEL = 1024
N_META = 16
PAST_LEN = 16384
H_A, DK_A, DV_A = 4, 128, 256
W_B, NB_B, BS_B, CONV_B = 1024, 8, 128, 4
LRU_C = 8.0
HS_C = 64
N_PAIR = D_MODEL // (2 * HS_C)
LORA_PAD = 128
GN_EPS_C = 64e-5
D_FF, CONV_F = 2816, 3
EPS = 1e-6
ROPE_BASE = 10000.0
QK_W, VG_W = H_A * DK_A, H_A * DV_A

LANES = 128
TAIL = 8
VMEM_LIMIT = 52 * 1024 * 1024
PAD_T = 16
RWKV_SUB = 16


def _cparams(sem):
    return pltpu.CompilerParams(dimension_semantics=sem, vmem_limit_bytes=VMEM_LIMIT)


def _dot(a, b, prec=None):
    return jnp.dot(a, b, preferred_element_type=F32, precision=prec)


def _dot_nt(a, b, prec=None):
    return lax.dot_general(a, b, (((1,), (1,)), ((), ())), preferred_element_type=F32, precision=prec)


def _dot_tn(a, b, prec=None):
    return lax.dot_general(a, b, (((0,), (0,)), ((), ())), preferred_element_type=F32, precision=prec)


def _bf(x):
    return x.astype(BF16)


def _split(x):
    hi = x.astype(BF16)
    return hi, (x - hi.astype(F32)).astype(BF16)


def _dot3(a, b, dot):
    return dot(a[0], b[0]) + dot(a[0], b[1]) + dot(a[1], b[0])


def _rms(x, g):
    return x * lax.rsqrt(jnp.mean(x * x, axis=-1, keepdims=True) + EPS) * g


def _log_sigmoid(x):
    return jnp.minimum(x, 0.0) - jnp.log(1.0 + jnp.exp(-jnp.abs(x)))


def _expm1(x):
    u = jnp.exp(x)
    near = jnp.where(u == 1.0, x, (u - 1.0) * x / jnp.log(jnp.where(u == 1.0, 2.0, u)))
    return jnp.where(jnp.abs(x) > 0.5, u - 1.0, near)


def _mm_body(*refs, n_in, norm, res):
    it = iter(refs)
    x_refs = [next(it) for _ in range(n_in)]
    g_ref = next(it) if norm else None
    w_refs = [next(it) for _ in range(n_in)]
    r_ref = next(it) if res else None
    o_ref = next(it)
    if norm:
        xn_ref = next(it)

        @pl.when(pl.program_id(1) == 0)
        def _():
            xn_ref[...] = _rms(x_refs[0][...], g_ref[...]).astype(BF16)

        acc = _dot(xn_ref[...], w_refs[0][...])
    else:
        acc = _dot(x_refs[0][...].astype(BF16), w_refs[0][...])
        for x_ref, w_ref in zip(x_refs[1:], w_refs[1:]):
            acc = acc + _dot(x_ref[...].astype(BF16), w_ref[...])
    if res:
        acc = acc + r_ref[...]
    o_ref[...] = acc.astype(o_ref.dtype)


def _mm(xs, ws, *, g=None, res=None, tm, tn, out_dtype=F32):
    R = xs[0].shape[0]
    N = ws[0].shape[1]
    norm = g is not None
    ins, specs = [], []
    for x in xs:
        ins.append(x)
        specs.append(pl.BlockSpec((tm, x.shape[1]), lambda i, j: (i, 0)))
    if norm:
        ins.append(g.reshape(1, -1))
        specs.append(pl.BlockSpec((1, g.shape[-1]), lambda i, j: (0, 0)))
    for w in ws:
        ins.append(w)
        specs.append(pl.BlockSpec((w.shape[0], tn), lambda i, j: (0, j)))
    if res is not None:
        ins.append(res)
        specs.append(pl.BlockSpec((tm, tn), lambda i, j: (i, j)))
    scratch = [pltpu.VMEM((tm, xs[0].shape[1]), BF16)] if norm else []
    return pl.pallas_call(
        functools.partial(_mm_body, n_in=len(xs), norm=norm, res=res is not None),
        grid=(R // tm, N // tn),
        in_specs=specs,
        out_specs=pl.BlockSpec((tm, tn), lambda i, j: (i, j)),
        out_shape=jax.ShapeDtypeStruct((R, N), out_dtype),
        scratch_shapes=scratch,
        compiler_params=_cparams(("arbitrary", "arbitrary")),
        name="mm",
    )(*ins)


def _final_norm_body(x_ref, g_ref, o_ref):
    o_ref[...] = _rms(x_ref[...], g_ref[...])


def _final_norm(x, g, tm):
    R = x.shape[0]
    return pl.pallas_call(
        _final_norm_body,
        grid=(R // tm,),
        in_specs=[pl.BlockSpec((tm, D_MODEL), lambda i: (i, 0)), pl.BlockSpec((1, D_MODEL), lambda i: (0, 0))],
        out_specs=pl.BlockSpec((tm, D_MODEL), lambda i: (i, 0)),
        out_shape=jax.ShapeDtypeStruct((R, D_MODEL), F32),
        compiler_params=_cparams(("arbitrary",)),
        name="final_norm",
    )(x, g.reshape(1, -1))


def _ret_body(q_ref, k_ref, v_ref, ga_ref, cos_ref, sin_ref, s0_ref, gn_ref, y_ref, s_ref, *, C, c_true):
    @pl.when(pl.program_id(1) == 0)
    def _():
        s_ref[...] = s0_ref[...]

    cosf, sinf = cos_ref[...], sin_ref[...]
    ti = lax.broadcasted_iota(jnp.int32, (C, C), 0)
    si = lax.broadcasted_iota(jnp.int32, (C, C), 1)
    dif = (ti - si).astype(F32)
    tcol = lax.broadcasted_iota(jnp.int32, (C, 1), 0).astype(F32)
    for h in range(H_A):
        lg = float(np.log1p(-(2.0 ** (-5.0 - h))))
        q = q_ref[:, h * DK_A:(h + 1) * DK_A]
        k = k_ref[:, h * DK_A:(h + 1) * DK_A]
        q = q * cosf + pltpu.roll(q, DK_A // 2, 1) * sinf
        k = (k * cosf + pltpu.roll(k, DK_A // 2, 1) * sinf) * (DK_A ** -0.5)
        vb = v_ref[:, h * DV_A:(h + 1) * DV_A].astype(BF16)
        qb = q.astype(BF16)
        mask = jnp.where(dif >= 0, jnp.exp(lg * jnp.maximum(dif, 0.0)), 0.0)
        sc = _dot_nt(qb, k.astype(BF16)) * mask
        S = s_ref[0, h]
        o = _dot(sc.astype(BF16), vb) + _dot(qb, S.astype(BF16)) * jnp.exp((tcol + 1.0) * lg)
        kd = (k * jnp.exp((c_true - 1.0 - tcol) * lg)).astype(BF16)
        s_ref[0, h] = float(np.exp(c_true * lg)) * S + _dot_tn(kd, vb)
        mu = jnp.mean(o, axis=-1, keepdims=True)
        d = o - mu
        var = jnp.mean(d * d, axis=-1, keepdims=True)
        on = d * lax.rsqrt(var + EPS) * gn_ref[:, h * DV_A:(h + 1) * DV_A]
        ga = ga_ref[:, h * DV_A:(h + 1) * DV_A]
        y_ref[:, h * DV_A:(h + 1) * DV_A] = (ga * jax.nn.sigmoid(ga) * on).astype(BF16)


def _retention(z, cosf, sinf, s0, gn, *, B, T, C, c_true, shared):
    NC = T // C
    smap = (lambda b, c: (0, 0, 0, 0)) if shared else (lambda b, c: (b, 0, 0, 0))
    row = lambda b, c: b * NC + c
    return pl.pallas_call(
        functools.partial(_ret_body, C=C, c_true=c_true),
        grid=(B, NC),
        in_specs=[
            pl.BlockSpec((C, QK_W), lambda b, c: (row(b, c), 0)),
            pl.BlockSpec((C, QK_W), lambda b, c: (row(b, c), 1)),
            pl.BlockSpec((C, VG_W), lambda b, c: (row(b, c), 1)),
            pl.BlockSpec((C, VG_W), lambda b, c: (row(b, c), 2)),
            pl.BlockSpec((C, DK_A), lambda b, c: (c, 0)),
            pl.BlockSpec((C, DK_A), lambda b, c: (c, 0)),
            pl.BlockSpec((1, H_A, DK_A, DV_A), smap),
            pl.BlockSpec((1, VG_W), lambda b, c: (0, 0)),
        ],
        out_specs=[
            pl.BlockSpec((C, VG_W), lambda b, c: (row(b, c), 0)),
            pl.BlockSpec((1, H_A, DK_A, DV_A), lambda b, c: (b, 0, 0, 0)),
        ],
        out_shape=[
            jax.ShapeDtypeStruct((B * T, VG_W), BF16),
            jax.ShapeDtypeStruct((B, H_A, DK_A, DV_A), F32),
        ],
        compiler_params=_cparams(("arbitrary", "arbitrary")),
        name="retention",
    )(z, z, z, z, cosf, sinf, s0, gn.reshape(1, -1))


def _rope_tables(pos):
    half = DK_A // 2
    inv = ROPE_BASE ** (-jnp.linspace(0.0, 1.0, half, dtype=F32))
    ang = pos.astype(F32)[:, None] * inv[None, :]
    cos, sin = jnp.cos(ang), jnp.sin(ang)
    return jnp.concatenate([cos, cos], axis=-1), jnp.concatenate([-sin, sin], axis=-1)


def _lru_gates(xc, wa_ref, ba, wx_ref, bx, lam):
    ra, ia = [], []
    for gi in range(NB_B):
        xg = xc[:, gi * BS_B:(gi + 1) * BS_B].astype(BF16)
        ra.append(_dot(xg, wa_ref[gi]))
        ia.append(_dot(xg, wx_ref[gi]))
    r = jax.nn.sigmoid(jnp.concatenate(ra, axis=1) + ba)
    i = jax.nn.sigmoid(jnp.concatenate(ia, axis=1) + bx)
    log_a = (-LRU_C) * r * (-_log_sigmoid(lam))
    a = jnp.exp(log_a)
    u = jnp.sqrt(-_expm1(2.0 * log_a)) * (i * xc)
    return a, u


def _lru_seq_body(xb_ref, gb_ref, cw_ref, cb_ref, wa_ref, ba_ref, wx_ref, bx_ref, lam_ref, h0_ref, tail0_ref,
                  y_ref, hout_ref, cout_ref, xe_ref, hc_ref, *, Tc):
    @pl.when(pl.program_id(1) == 0)
    def _():
        xe_ref[0:TAIL, :] = tail0_ref[0]
        hc_ref[...] = h0_ref[0]

    x = xb_ref[...]
    xe_ref[TAIL:TAIL + Tc, :] = x
    cw = cw_ref[...]
    xc = cb_ref[...]
    for j in range(CONV_B - 1):
        off = TAIL - (CONV_B - 1) + j
        xc = xc + cw[j:j + 1] * xe_ref[off:off + Tc, :]
    xc = xc + cw[CONV_B - 1:CONV_B] * x
    a, u = _lru_gates(xc, wa_ref, ba_ref[...], wx_ref, bx_ref[...], lam_ref[...])
    row = lax.broadcasted_iota(jnp.int32, (Tc, 1), 0)
    d = 1
    while d < Tc:
        a_sh = pltpu.roll(a, d, 0)
        u_sh = pltpu.roll(u, d, 0)
        valid = row >= d
        u = jnp.where(valid, a * u_sh + u, u)
        a = jnp.where(valid, a * a_sh, a)
        d *= 2
    hs = a * hc_ref[...] + u
    y_ref[...] = (hs * jax.nn.gelu(gb_ref[...])).astype(BF16)
    hc_ref[...] = hs[Tc - 1:Tc]
    hout_ref[0] = hs[Tc - 1:Tc]
    cout_ref[0] = xe_ref[Tc + TAIL - (CONV_B - 1):Tc + TAIL, :]
    xe_ref[0:TAIL, :] = xe_ref[Tc:Tc + TAIL, :]


def _tail_rows(state):
    return jnp.pad(state, ((0, 0), (TAIL - state.shape[1], 0), (0, 0)))


def _lru_seq(z, p, h0, conv0, *, B, T, Tc, shared):
    NC = T // Tc
    smap = (lambda b, c: (0, 0, 0)) if shared else (lambda b, c: (b, 0, 0))
    row = lambda b, c: b * NC + c
    vec = lambda a: pl.BlockSpec((1, W_B), lambda b, c: (0, 0))
    wspec = pl.BlockSpec((NB_B, BS_B, BS_B), lambda b, c: (0, 0, 0))
    return pl.pallas_call(
        functools.partial(_lru_seq_body, Tc=Tc),
        grid=(B, NC),
        in_specs=[
            pl.BlockSpec((Tc, W_B), lambda b, c: (row(b, c), 3)),
            pl.BlockSpec((Tc, W_B), lambda b, c: (row(b, c), 4)),
            pl.BlockSpec((CONV_B, W_B), lambda b, c: (0, 0)), vec(0),
            wspec, vec(0), wspec, vec(0), vec(0),
            pl.BlockSpec((1, 1, W_B), smap),
            pl.BlockSpec((1, TAIL, W_B), smap),
        ],
        out_specs=[
            pl.BlockSpec((Tc, W_B), lambda b, c: (row(b, c), 0)),
            pl.BlockSpec((1, 1, W_B), lambda b, c: (b, 0, 0)),
            pl.BlockSpec((1, CONV_B - 1, W_B), lambda b, c: (b, 0, 0)),
        ],
        out_shape=[
            jax.ShapeDtypeStruct((B * T, W_B), BF16),
            jax.ShapeDtypeStruct((B, 1, W_B), F32),
            jax.ShapeDtypeStruct((B, CONV_B - 1, W_B), F32),
        ],
        scratch_shapes=[pltpu.VMEM((Tc + TAIL, W_B), F32), pltpu.VMEM((1, W_B), F32)],
        compiler_params=_cparams(("arbitrary", "arbitrary")),
        name="lru_seq",
    )(z, z, p["conv_w"], p["conv_b"], p["wa"], p["ba"], p["wx"], p["bx"], p["lam"], h0, _tail_rows(conv0))


def _lru_dec_body(xb_ref, gb_ref, s0_ref, s1_ref, s2_ref, cw_ref, cb_ref, wa_ref, ba_ref, wx_ref, bx_ref, lam_ref,
                  h0_ref, y_ref, hout_ref):
    cw = cw_ref[...]
    x = xb_ref[...]
    xc = cb_ref[...] + cw[0:1] * s0_ref[...] + cw[1:2] * s1_ref[...] + cw[2:3] * s2_ref[...] + cw[3:4] * x
    a, u = _lru_gates(xc, wa_ref, ba_ref[...], wx_ref, bx_ref[...], lam_ref[...])
    hs = a * h0_ref[...] + u
    y_ref[...] = (hs * jax.nn.gelu(gb_ref[...])).astype(BF16)
    hout_ref[...] = hs


def _lru_dec(z, p, h0, conv0):
    B = z.shape[0]
    full = lambda shape: pl.BlockSpec(shape, lambda i: (0,) * len(shape))
    return pl.pallas_call(
        _lru_dec_body,
        grid=(1,),
        in_specs=[
            pl.BlockSpec((B, W_B), lambda i: (0, 3)), pl.BlockSpec((B, W_B), lambda i: (0, 4)),
            full((B, W_B)), full((B, W_B)), full((B, W_B)),
            full((CONV_B, W_B)), full((1, W_B)),
            full((NB_B, BS_B, BS_B)), full((1, W_B)), full((NB_B, BS_B, BS_B)), full((1, W_B)), full((1, W_B)),
            full((B, W_B)),
        ],
        out_specs=[full((B, W_B)), full((B, W_B))],
        out_shape=[jax.ShapeDtypeStruct((B, W_B), BF16), jax.ShapeDtypeStruct((B, W_B), F32)],
        compiler_params=_cparams(("arbitrary",)),
        name="lru_dec",
    )(z, z, conv0[:, 0], conv0[:, 1], conv0[:, 2], p["conv_w"], p["conv_b"], p["wa"], p["ba"], p["wx"], p["bx"],
      p["lam"], h0)


def _ffn_seq_body(ug_ref, uv_ref, cw_ref, cb_ref, tail0_ref, m_ref, cout_ref, xe_ref, *, Tc):
    @pl.when(pl.program_id(1) == 0)
    def _():
        xe_ref[0:TAIL, :] = tail0_ref[0]

    x = ug_ref[...]
    xe_ref[TAIL:TAIL + Tc, :] = x
    cw = cw_ref[...]
    c = cb_ref[...]
    for j in range(CONV_F - 1):
        off = TAIL - (CONV_F - 1) + j
        c = c + cw[j:j + 1] * xe_ref[off:off + Tc, :]
    c = c + cw[CONV_F - 1:CONV_F] * x
    m_ref[...] = (jax.nn.gelu(c) * uv_ref[...]).astype(BF16)
    cout_ref[0] = xe_ref[Tc + TAIL - (CONV_F - 1):Tc + TAIL, :]
    xe_ref[0:TAIL, :] = xe_ref[Tc:Tc + TAIL, :]


def _ffn_seq(u, cw, cb, conv0, *, B, T, Tc, shared):
    NC = T // Tc
    smap = (lambda b, c: (0, 0, 0)) if shared else (lambda b, c: (b, 0, 0))
    row = lambda b, c: b * NC + c
    return pl.pallas_call(
        functools.partial(_ffn_seq_body, Tc=Tc),
        grid=(B, NC),
        in_specs=[
            pl.BlockSpec((Tc, D_FF), lambda b, c: (row(b, c), 0)),
            pl.BlockSpec((Tc, D_FF), lambda b, c: (row(b, c), 1)),
            pl.BlockSpec((CONV_F, D_FF), lambda b, c: (0, 0)),
            pl.BlockSpec((1, D_FF), lambda b, c: (0, 0)),
            pl.BlockSpec((1, TAIL, D_FF), smap),
        ],
        out_specs=[
            pl.BlockSpec((Tc, D_FF), lambda b, c: (row(b, c), 0)),
            pl.BlockSpec((1, CONV_F - 1, D_FF), lambda b, c: (b, 0, 0)),
        ],
        out_shape=[
            jax.ShapeDtypeStruct((B * T, D_FF), BF16),
            jax.ShapeDtypeStruct((B, CONV_F - 1, D_FF), F32),
        ],
        scratch_shapes=[pltpu.VMEM((Tc + TAIL, D_FF), F32)],
        compiler_params=_cparams(("arbitrary", "arbitrary")),
        name="ffn_seq",
    )(u, u, cw, cb, _tail_rows(conv0))


def _ffn_dec_body(ug_ref, uv_ref, s0_ref, s1_ref, cw_ref, cb_ref, m_ref):
    cw = cw_ref[...]
    c = cb_ref[...] + cw[0:1] * s0_ref[...] + cw[1:2] * s1_ref[...] + cw[2:3] * ug_ref[...]
    m_ref[...] = (jax.nn.gelu(c) * uv_ref[...]).astype(BF16)


def _ffn_dec(u, cw, cb, conv0):
    B = u.shape[0]
    full = lambda shape: pl.BlockSpec(shape, lambda i: (0,) * len(shape))
    return pl.pallas_call(
        _ffn_dec_body,
        grid=(1,),
        in_specs=[pl.BlockSpec((B, D_FF), lambda i: (0, 0)), pl.BlockSpec((B, D_FF), lambda i: (0, 1)),
                  full((B, D_FF)), full((B, D_FF)), full((CONV_F, D_FF)), full((1, D_FF))],
        out_specs=full((B, D_FF)),
        out_shape=jax.ShapeDtypeStruct((B, D_FF), BF16),
        compiler_params=_cparams(("arbitrary",)),
        name="ffn_dec",
    )(u, u, conv0[:, 0], conv0[:, 1], cw, cb)


_PREP_W = ("mix", "w_r", "w_k", "w_v", "w0", "w1", "w2", "a0", "a1", "a2", "g1", "g2")
_PREP_VP = ("v0", "v1", "v2")


def _rwkv_prep_body(*refs, has_vp, carry):
    it = iter(refs)
    x_ref, gn_ref = next(it), next(it)
    p = {n: next(it) for n in _PREP_W}
    if has_vp:
        p.update({n: next(it) for n in _PREP_VP})
        vf_ref = next(it)
    prev_ref = next(it)
    r_ref, lw_ref, k_ref, v_ref, as_ref, g_ref, hn_ref = (next(it) for _ in range(7))
    h = _rms(x_ref[...], gn_ref[...])
    if carry:
        carry_ref = next(it)

        @pl.when(pl.program_id(1) == 0)
        def _():
            carry_ref[...] = prev_ref[0]

        row = lax.broadcasted_iota(jnp.int32, (h.shape[0], 1), 0)
        hprev = jnp.where(row == 0, carry_ref[...], pltpu.roll(h, 1, 0))
        carry_ref[...] = h[h.shape[0] - 1:]
        hn_ref[0] = h[h.shape[0] - 1:]
    else:
        hprev = prev_ref[...]
        hn_ref[...] = h
    xx = hprev - h
    mix = p["mix"][...]
    xs = [(h + xx * mix[n:n + 1]).astype(BF16) for n in range(6)]
    xr, xw, xk, xv, xa, xg = xs
    r_ref[...] = _dot(xr, p["w_r"][...])
    k_ref[...] = _dot(xk, p["w_k"][...])
    v = _dot(xv, p["w_v"][...])
    wl = p["w0"][...] + _dot(jnp.tanh(_dot(xw, p["w1"][...])).astype(BF16), p["w2"][...])
    lw_ref[...] = -jnp.exp(_log_sigmoid(wl) - 0.5)
    if has_vp:
        gate = jax.nn.sigmoid(p["v0"][...] + _dot(_dot(xv, p["v1"][...]).astype(BF16), p["v2"][...]))
        v = v + (vf_ref[...] - v) * gate
    v_ref[...] = v
    as_ref[...] = jax.nn.sigmoid(p["a0"][...] + _dot(_dot(xa, p["a1"][...]).astype(BF16), p["a2"][...]))
    g_ref[...] = _dot(jax.nn.sigmoid(_dot(xg, p["g1"][...])).astype(BF16), p["g2"][...])


def _rwkv_prep(x, gn, p, vp, v_first, prev, *, B, T, tm, carry, shared=False):
    R = x.shape[0]
    has_vp = vp is not None
    if carry:
        NT = T // tm
        grid = (B, NT)
        rowmap = lambda b, t: (b * NT + t, 0)
        cmap2 = lambda b, t: (0, 0)
        pmap = (lambda b, t: (0, 0, 0)) if shared else (lambda b, t: (b, 0, 0))
        prev_spec = pl.BlockSpec((1, 1, D_MODEL), pmap)
        hn_spec = pl.BlockSpec((1, 1, D_MODEL), lambda b, t: (b, 0, 0))
        hn_shape = jax.ShapeDtypeStruct((B, 1, D_MODEL), F32)
        sem = ("arbitrary", "arbitrary")
    else:
        grid = (R // tm,)
        rowmap = lambda i: (i, 0)
        cmap2 = lambda i: (0, 0)
        prev_spec = pl.BlockSpec((tm, D_MODEL), rowmap)
        hn_spec = pl.BlockSpec((tm, D_MODEL), rowmap)
        hn_shape = jax.ShapeDtypeStruct((R, D_MODEL), F32)
        sem = ("arbitrary",)
    tile = pl.BlockSpec((tm, D_MODEL), rowmap)
    ins = [x, gn.reshape(1, -1)]
    specs = [tile, pl.BlockSpec((1, D_MODEL), cmap2)]
    names = _PREP_W + (_PREP_VP if has_vp else ())
    src = dict(p)
    if has_vp:
        src.update(vp)
    for n in names:
        ins.append(src[n])
        specs.append(pl.BlockSpec(src[n].shape, cmap2))
    if has_vp:
        ins.append(v_first)
        specs.append(tile)
    ins.append(prev)
    specs.append(prev_spec)
    outs = pl.pallas_call(
        functools.partial(_rwkv_prep_body, has_vp=has_vp, carry=carry),
        grid=grid,
        in_specs=specs,
        out_specs=[tile] * 6 + [hn_spec],
        out_shape=[jax.ShapeDtypeStruct((R, D_MODEL), F32)] * 6 + [hn_shape],
        scratch_shapes=[pltpu.VMEM((1, D_MODEL), F32)] if carry else [],
        compiler_params=_cparams(sem),
        name="rwkv_prep",
    )(*ins)
    return outs


def _rwkv_chunk_body(r_ref, lw_ref, k_ref, v_ref, as_ref, g_ref, kk_ref, ka_ref, rk_ref, gg_ref, gb_ref, s0_ref,
                     o_ref, s_ref, *, Tc, C, c_true, n_pair):
    @pl.when(pl.program_id(2) == 0)
    def _():
        s_ref[...] = s0_ref[...]

    C2 = 2 * C
    lane = lax.broadcasted_iota(jnp.int32, (1, LANES), 1)
    m0 = (lane < HS_C).astype(F32)
    m1 = 1.0 - m0
    li = lax.broadcasted_iota(jnp.int32, (LANES, LANES), 0)
    lj = lax.broadcasted_iota(jnp.int32, (LANES, LANES), 1)
    same_head = (li >= HS_C) == (lj >= HS_C)
    bd_ones = same_head.astype(F32).astype(BF16)
    bd_avg = (same_head.astype(F32) * (1.0 / HS_C)).astype(BF16)
    ti = lax.broadcasted_iota(jnp.int32, (C, C), 0)
    si = lax.broadcasted_iota(jnp.int32, (C, C), 1)
    tri = (ti >= si).astype(F32).astype(BF16)
    r2 = lax.broadcasted_iota(jnp.int32, (C2, C2), 0)
    c2 = lax.broadcasted_iota(jnp.int32, (C2, C2), 1)
    same = (r2 >= C) == (c2 >= C)
    tt, ss = r2 & (C - 1), c2 & (C - 1)
    m_strict = same & (tt > ss)
    m_incl = same & (tt >= ss)
    same_sub = (tt & -RWKV_SUB) == (ss & -RWKV_SUB)
    m_sub = m_strict & same_sub
    m_off = m_strict & jnp.logical_not(same_sub)
    eye = (r2 == c2).astype(F32)
    nb = C // RWKV_SUB
    stack = lambda x: jnp.concatenate([x * m0, x * m1], axis=0)
    dup = lambda x: jnp.concatenate([x, x], axis=0)

    def sub_chunk(ci, carry):
        rows = pl.ds(pl.multiple_of(ci * C, C), C)
        lanes = [slice(p * LANES, (p + 1) * LANES) for p in range(n_pair)]
        each = lambda f, *cols: [f(*xs) for xs in zip(*cols)]

        def load(ls):
            r, lw, k, v = r_ref[rows, ls], lw_ref[rows, ls], k_ref[rows, ls], v_ref[rows, ls]
            asig = as_ref[rows, ls]
            kk = k * kk_ref[:, ls]
            k2 = k * (1.0 + (asig - 1.0) * ka_ref[:, ls])
            return r, lw, k2, v, asig, kk

        r, lw, k2, v, asig, kk = zip(*each(load, lanes))
        ssq = each(lambda x: _dot(_bf(x * x), bd_ones), kk)
        cs = each(lambda x: sum(_dot(tri, part) for part in _split(x)), lw)
        kk = each(lambda x, q: x * lax.rsqrt(jnp.maximum(q, 1e-24)), kk, ssq)
        b = each(lambda x, q: x * q, kk, asig)
        cend = [x[C - 1:C] for x in cs]
        e_neg = each(lambda x: jnp.exp(-x), cs)
        e_end = each(lambda x, y: jnp.exp(y - x), cs, cend)
        As = each(lambda x, c, l: _split(stack(-x * jnp.exp(c - l))), kk, cs, lw)
        Bd = each(lambda x, e: _split(dup(x * e)), b, e_neg)
        Rs = each(lambda x, c: _bf(stack(x * jnp.exp(c))), r, cs)
        Vs = each(lambda x: _bf(stack(x)), v)
        Kd = each(lambda x, e: _bf(dup(x * e)), k2, e_neg)
        G = each(lambda x, y: _dot3(x, y, _dot_nt), As, Bd)
        Aak = each(lambda x, y: _bf(jnp.where(m_strict, _dot_nt(x[0], y), 0.0)), As, Kd)
        Arb = each(lambda x, y: _bf(jnp.where(m_incl, _dot_nt(x, y[0]), 0.0)), Rs, Bd)
        Ark = each(lambda x, y: _bf(jnp.where(m_incl, _dot_nt(x, y), 0.0)), Rs, Kd)
        Nd = each(lambda x: jnp.where(m_sub, x, 0.0), G)
        P = each(lambda x: eye + x, Nd)
        Qs = each(_split, Nd)
        for _ in range(3):
            Qs = each(lambda q: _split(_dot3(q, q, _dot)), Qs)
            P = each(lambda x, q: x + _dot3(_split(x), q, _dot), P, Qs)
        if nb > 1:
            Pb = each(_bf, P)
            M = each(lambda x, y: _dot(x, _bf(jnp.where(m_off, y, 0.0))), Pb, G)
            Tm = each(lambda x: eye + x, M)
            for _ in range(int(np.ceil(np.log2(nb))) - 1):
                M = each(lambda x: _dot(_bf(x), _bf(x)), M)
                Tm = each(lambda x, y: x + _dot(_bf(x), _bf(y)), Tm, M)
            Tinv = each(lambda x, y: _bf(_dot(_bf(x), y)), Tm, Pb)
        else:
            Tinv = each(_bf, P)
        S = [s_ref[0, p] for p in range(n_pair)]
        Sb = each(_bf, S)
        X = each(lambda a_, s_, k_, v_: _bf(_dot_nt(a_[0], s_) + _dot(k_, v_)), As, Sb, Aak, Vs)
        Us = each(lambda t_, x_: _bf(_dot(t_, x_)), Tinv, X)
        Ys = each(lambda r_, s_, b_, u_, k_, v_: _dot_nt(r_, s_) + _dot(b_, u_) + _dot(k_, v_),
                  Rs, Sb, Arb, Us, Ark, Vs)
        for p in range(n_pair):
            s_ref[0, p] = (S[p] * jnp.exp(cend[p]) + _dot_tn(Us[p], _bf(stack(b[p] * e_end[p])))
                           + _dot_tn(Vs[p], _bf(stack(k2[p] * e_end[p]))))
        y = [x[:C] + x[C:] for x in Ys]
        mu = each(lambda x: _dot(_bf(x), bd_avg), y)
        d = each(lambda x, m: x - m, y, mu)
        var = each(lambda x: _dot(_bf(x * x), bd_avg), d)
        bonus = each(lambda r_, k_, ls: _dot(_bf(r_ * k_ * rk_ref[:, ls]), bd_ones), r, k2, lanes)
        for p, ls in enumerate(lanes):
            on = d[p] * lax.rsqrt(var[p] + GN_EPS_C) * gg_ref[:, ls] + gb_ref[:, ls]
            o_ref[rows, ls] = ((on + bonus[p] * v[p]) * g_ref[rows, ls]).astype(BF16)
        return carry

    lax.fori_loop(0, Tc // C, sub_chunk, 0)


def _rwkv_chunk(arrs, p, s0, *, B, T, Tc, C, ppb, shared):
    NC = T // Tc
    NP = N_PAIR // ppb
    Wb = ppb * LANES
    tile = pl.BlockSpec((Tc, Wb), lambda b, q, c: (b * NC + c, q))
    vec = pl.BlockSpec((1, Wb), lambda b, q, c: (0, q))
    smap = (lambda b, q, c: (0, q, 0, 0)) if shared else (lambda b, q, c: (b, q, 0, 0))
    return pl.pallas_call(
        functools.partial(_rwkv_chunk_body, Tc=Tc, C=C, c_true=None, n_pair=ppb),
        grid=(B, NP, NC),
        in_specs=[tile] * 6 + [vec] * 5 + [pl.BlockSpec((1, ppb, LANES, LANES), smap)],
        out_specs=[tile, pl.BlockSpec((1, ppb, LANES, LANES), lambda b, q, c: (b, q, 0, 0))],
        out_shape=[jax.ShapeDtypeStruct((B * T, D_MODEL), BF16),
                   jax.ShapeDtypeStruct((B, N_PAIR, LANES, LANES), F32)],
        compiler_params=_cparams(("arbitrary", "arbitrary", "arbitrary")),
        name="rwkv_chunk",
    )(*arrs, p["k_k"], p["k_a"], p["r_k"], p["gn_g"], p["gn_b"], s0)


def _to_blockdiag(s):
    even = jnp.pad(s[:, 0::2], ((0, 0), (0, 0), (0, 0), (0, HS_C)))
    odd = jnp.pad(s[:, 1::2], ((0, 0), (0, 0), (0, 0), (HS_C, 0)))
    return jnp.concatenate([even, odd], axis=2)


def _from_blockdiag(s):
    B = s.shape[0]
    return jnp.stack([s[:, :, :HS_C, :HS_C], s[:, :, HS_C:, HS_C:]], axis=2).reshape(B, 2 * N_PAIR, HS_C, HS_C)


def _pad_rows(x):
    B, W = x.shape
    return jnp.pad(x[:, None, :], ((0, 0), (0, PAD_T - 1), (0, 0))).reshape(B * PAD_T, W)


def _first_rows(x):
    return x.reshape(-1, PAD_T, x.shape[-1])[:, 0]


MAIN_TM = 1024
MAIN_TN = 512
RET_C = 128
LRU_TC = 256
FFN_TC = 256
PREP_TM = 256
RWKV_TC = 256
RWKV_C = 64
RWKV_PPB = 8


def kernel(x_prompt, x_sample, state_ret, state_lru, state_lru_conv, state_rwkv, state_shift, state_ffn_conv,
           meta_tokens, norm_mix_g, norm_ffn_g, norm_final_g,
           ev_w_in, ev_ret_gn_g, ev_lru_conv_w, ev_lru_conv_b, ev_lru_wa, ev_lru_ba, ev_lru_wx, ev_lru_bx,
           ev_lru_lambda, ev_w_out,
           od_mix, od_w_r, od_w_k, od_w_v, od_w0, od_w1, od_w2, od_a0, od_a1, od_a2, od_v0, od_v1, od_v2,
           od_g1, od_g2, od_k_k, od_k_a, od_r_k, od_gn_g, od_gn_b, od_w_o,
           ff_w_up, ff_conv_w, ff_conv_b, ff_w_down):
    BP, TP, _ = x_prompt.shape
    BS = x_sample.shape[0]
    NS = BS + N_META
    bf = lambda a: a.astype(BF16)
    row = lambda a: a.reshape(1, -1)

    def lora_in(w):
        return bf(jnp.pad(w, ((0, 0), (0, LORA_PAD - w.shape[1]))))

    def lora_out(w):
        return bf(jnp.pad(w, ((0, LORA_PAD - w.shape[0]), (0, 0))))

    xm = x_prompt.reshape(BP * TP, D_MODEL)
    xs = jnp.concatenate([x_sample.reshape(BS, D_MODEL), meta_tokens.astype(x_prompt.dtype)], axis=0)

    cos_m, sin_m = _rope_tables(N_META + jnp.arange(TP, dtype=jnp.int32))
    cos_t, sin_t = _rope_tables(jnp.arange(N_META, dtype=jnp.int32))
    cos_s, sin_s = _rope_tables(PAST_LEN + jnp.arange(PAD_T, dtype=jnp.int32))

    out = {k: [] for k in ("ret_p", "lru_p", "lconv_p", "rwkv_p", "shift_p", "ffn_p",
                           "ret_s", "lru_s", "lconv_s", "rwkv_s", "shift_s", "ffn_s")}
    v_first_m = v_first_s = None
    for li in range(4):
        j = li // 2
        if li % 2 == 0:
            w_in = bf(ev_w_in[j])
            w_out_a, w_out_b = bf(ev_w_out[j][:VG_W]), bf(ev_w_out[j][VG_W:])
            gn = ev_ret_gn_g[j]
            lp = dict(conv_w=ev_lru_conv_w[j], conv_b=row(ev_lru_conv_b[j]), wa=bf(ev_lru_wa[j]),
                      ba=row(ev_lru_ba[j]), wx=bf(ev_lru_wx[j]), bx=row(ev_lru_bx[j]), lam=row(ev_lru_lambda[j]))
            zs = _mm([xs], [w_in], g=norm_mix_g[li], tm=NS, tn=MAIN_TN)
            z_smp, z_meta = zs[:BS], zs[BS:]
            ya_t, ret_t = _retention(z_meta, cos_t, sin_t, jnp.zeros((1, H_A, DK_A, DV_A), F32), gn,
                                     B=1, T=N_META, C=N_META, c_true=N_META, shared=False)
            yb_t, lru_t, lconv_t = _lru_seq(z_meta, lp, jnp.zeros((1, 1, W_B), F32),
                                            jnp.zeros((1, CONV_B - 1, W_B), F32), B=1, T=N_META, Tc=N_META,
                                            shared=False)
            ya_s, ret_s = _retention(_pad_rows(z_smp), cos_s, sin_s, state_ret[j], gn,
                                     B=BS, T=PAD_T, C=PAD_T, c_true=1, shared=False)
            yb_s, lru_s = _lru_dec(z_smp, lp, state_lru[j], state_lru_conv[j])
            lconv_s = jnp.concatenate([state_lru_conv[j][:, 1:], z_smp[:, None, 3 * W_B:4 * W_B]], axis=1)
            ya = jnp.concatenate([_first_rows(ya_s), ya_t], axis=0)
            yb = jnp.concatenate([yb_s, yb_t], axis=0)
            xs = _mm([ya, yb], [w_out_a, w_out_b], res=xs, tm=NS, tn=MAIN_TN)
            zm = _mm([xm], [w_in], g=norm_mix_g[li], tm=MAIN_TM, tn=MAIN_TN)
            ya_m, ret_m = _retention(zm, cos_m, sin_m, ret_t, gn, B=BP, T=TP, C=RET_C, c_true=RET_C, shared=True)
            yb_m, lru_m, lconv_m = _lru_seq(zm, lp, lru_t, lconv_t, B=BP, T=TP, Tc=LRU_TC, shared=True)
            xm = _mm([ya_m, yb_m], [w_out_a, w_out_b], res=xm, tm=MAIN_TM, tn=MAIN_TN)
            out["ret_p"].append(ret_m)
            out["lru_p"].append(lru_m[:, 0])
            out["lconv_p"].append(lconv_m)
            out["ret_s"].append(ret_s)
            out["lru_s"].append(lru_s)
            out["lconv_s"].append(lconv_s)
        else:
            p = dict(mix=od_mix[j], w_r=bf(od_w_r[j]), w_k=bf(od_w_k[j]), w_v=bf(od_w_v[j]), w0=row(od_w0[j]),
                     w1=lora_in(od_w1[j]), w2=lora_out(od_w2[j]), a0=row(od_a0[j]), a1=lora_in(od_a1[j]),
                     a2=lora_out(od_a2[j]), g1=lora_in(od_g1[j]), g2=lora_out(od_g2[j]),
                     k_k=row(od_k_k[j]), k_a=row(od_k_a[j]), r_k=row(od_r_k[j]), gn_g=row(od_gn_g[j]),
                     gn_b=row(od_gn_b[j]))
            vp = None
            if j > 0:
                vp = dict(v0=row(od_v0[j - 1]), v1=lora_in(od_v1[j - 1]), v2=lora_out(od_v2[j - 1]))
            w_o = bf(od_w_o[j])
            gmix = norm_mix_g[li]
            pre_t = _rwkv_prep(xs[BS:], gmix, p, vp, None if vp is None else v_first_s[BS:],
                               jnp.zeros((1, 1, D_MODEL), F32), B=1, T=N_META, tm=N_META, carry=True)
            o_t, rw_t = _rwkv_chunk(pre_t[:6], p, jnp.zeros((1, N_PAIR, LANES, LANES), F32),
                                    B=1, T=N_META, Tc=N_META, C=N_META, ppb=RWKV_PPB, shared=False)
            shift_t = pre_t[6]
            pre_s = _rwkv_prep(xs[:BS], gmix, p, vp, None if vp is None else v_first_s[:BS],
                               state_shift[j], B=BS, T=1, tm=BS, carry=False)
            o_s, rw_s = _rwkv_chunk([_pad_rows(a) for a in pre_s[:6]], p, _to_blockdiag(state_rwkv[j]),
                                    B=BS, T=PAD_T, Tc=PAD_T, C=PAD_T, ppb=RWKV_PPB, shared=False)
            if vp is None:
                v_first_s = jnp.concatenate([pre_s[3], pre_t[3]], axis=0)
            o_small = jnp.concatenate([_first_rows(o_s), o_t], axis=0)
            xs = _mm([o_small], [w_o], res=xs, tm=NS, tn=MAIN_TN)
            pre_m = _rwkv_prep(xm, gmix, p, vp, v_first_m, shift_t, B=BP, T=TP, tm=PREP_TM, carry=True, shared=True)
            if vp is None:
                v_first_m = pre_m[3]
            o_m, rw_m = _rwkv_chunk(pre_m[:6], p, rw_t, B=BP, T=TP, Tc=RWKV_TC, C=RWKV_C, ppb=RWKV_PPB, shared=True)
            xm = _mm([o_m], [w_o], res=xm, tm=MAIN_TM, tn=MAIN_TN)
            out["rwkv_p"].append(_from_blockdiag(rw_m))
            out["shift_p"].append(pre_m[6][:, 0])
            out["rwkv_s"].append(_from_blockdiag(rw_s))
            out["shift_s"].append(pre_s[6])
        w_up, w_down = bf(ff_w_up[li]), bf(ff_w_down[li])
        cw, cb = ff_conv_w[li], row(ff_conv_b[li])
        us = _mm([xs], [w_up], g=norm_ffn_g[li], tm=NS, tn=MAIN_TN)
        m_t, ffn_t = _ffn_seq(us[BS:], cw, cb, jnp.zeros((1, CONV_F - 1, D_FF), F32), B=1, T=N_META, Tc=N_META,
                              shared=False)
        m_s = _ffn_dec(us[:BS], cw, cb, state_ffn_conv[li])
        ffn_s = jnp.concatenate([state_ffn_conv[li][:, 1:], us[:BS, None, :D_FF]], axis=1)
        xs = _mm([jnp.concatenate([m_s, m_t], axis=0)], [w_down], res=xs, tm=NS, tn=MAIN_TN)
        um = _mm([xm], [w_up], g=norm_ffn_g[li], tm=MAIN_TM, tn=MAIN_TN)
        m_m, ffn_m = _ffn_seq(um, cw, cb, ffn_t, B=BP, T=TP, Tc=FFN_TC, shared=True)
        xm = _mm([m_m], [w_down], res=xm, tm=MAIN_TM, tn=MAIN_TN)
        out["ffn_p"].append(ffn_m)
        out["ffn_s"].append(ffn_s)

    y_prompt = _final_norm(xm, norm_final_g, MAIN_TM).reshape(BP, TP, D_MODEL)
    y_sample = _final_norm(xs, norm_final_g, NS)[:BS].reshape(BS, 1, D_MODEL)
    st = lambda k: jnp.stack(out[k])
    return (y_prompt, y_sample,
            st("ret_p"), st("lru_p"), st("lconv_p"), st("rwkv_p"), st("shift_p"), st("ffn_p"),
            st("ret_s"), st("lru_s"), st("lconv_s"), st("rwkv_s"), st("shift_s"), st("ffn_s"))
```

```python
import functools

import numpy as np
import jax
import jax.numpy as jnp
from jax import lax
from jax.experimental import pallas as pl
from jax.experimental.pallas import tpu as pltpu

F32 = jnp.float32
BF16 = jnp.bfloat16
HI = lax.Precision.HIGHEST

D_MODEL = 1024
N_META = 16
PAST_LEN = 16384
H_A, DK_A, DV_A = 4, 128, 256
W_B, NB_B, BS_B, CONV_B = 1024, 8, 128, 4
LRU_C = 8.0
HS_C = 64
N_PAIR = D_MODEL // (2 * HS_C)
LORA_PAD = 128
GN_EPS_C = 64e-5
D_FF, CONV_F = 2816, 3
EPS = 1e-6
ROPE_BASE = 10000.0
QK_W, VG_W = H_A * DK_A, H_A * DV_A

LANES = 128
TAIL = 8
VMEM_LIMIT = 52 * 1024 * 1024
PAD_T = 16
RWKV_SUB = 16


def _cparams(sem):
    return pltpu.CompilerParams(dimension_semantics=sem, vmem_limit_bytes=VMEM_LIMIT)


def _dot(a, b, prec=None):
    return jnp.dot(a, b, preferred_element_type=F32, precision=prec)


def _dot_nt(a, b, prec=None):
    return lax.dot_general(a, b, (((1,), (1,)), ((), ())), preferred_element_type=F32, precision=prec)


def _dot_tn(a, b, prec=None):
    return lax.dot_general(a, b, (((0,), (0,)), ((), ())), preferred_element_type=F32, precision=prec)


def _bf(x):
    return x.astype(BF16)


def _split(x):
    hi = x.astype(BF16)
    return hi, (x - hi.astype(F32)).astype(BF16)


def _dot3(a, b, dot):
    return dot(a[0], b[0]) + dot(a[0], b[1]) + dot(a[1], b[0])


def _rms(x, g):
    return x * lax.rsqrt(jnp.mean(x * x, axis=-1, keepdims=True) + EPS) * g


def _log_sigmoid(x):
    return jnp.minimum(x, 0.0) - jnp.log(1.0 + jnp.exp(-jnp.abs(x)))


def _expm1(x):
    u = jnp.exp(x)
    near = jnp.where(u == 1.0, x, (u - 1.0) * x / jnp.log(jnp.where(u == 1.0, 2.0, u)))
    return jnp.where(jnp.abs(x) > 0.5, u - 1.0, near)


def _mm_body(*refs, n_in, norm, res):
    it = iter(refs)
    x_refs = [next(it) for _ in range(n_in)]
    g_ref = next(it) if norm else None
    w_refs = [next(it) for _ in range(n_in)]
    r_ref = next(it) if res else None
    o_ref = next(it)
    if norm:
        xn_ref = next(it)

        @pl.when(pl.program_id(1) == 0)
        def _():
            xn_ref[...] = _rms(x_refs[0][...], g_ref[...]).astype(BF16)

        acc = _dot(xn_ref[...], w_refs[0][...])
    else:
        acc = _dot(x_refs[0][...].astype(BF16), w_refs[0][...])
        for x_ref, w_ref in zip(x_refs[1:], w_refs[1:]):
            acc = acc + _dot(x_ref[...].astype(BF16), w_ref[...])
    if res:
        acc = acc + r_ref[...]
    o_ref[...] = acc.astype(o_ref.dtype)


def _mm(xs, ws, *, g=None, res=None, tm, tn, out_dtype=F32):
    R = xs[0].shape[0]
    N = ws[0].shape[1]
    norm = g is not None
    ins, specs = [], []
    for x in xs:
        ins.append(x)
        specs.append(pl.BlockSpec((tm, x.shape[1]), lambda i, j: (i, 0)))
    if norm:
        ins.append(g.reshape(1, -1))
        specs.append(pl.BlockSpec((1, g.shape[-1]), lambda i, j: (0, 0)))
    for w in ws:
        ins.append(w)
        specs.append(pl.BlockSpec((w.shape[0], tn), lambda i, j: (0, j)))
    if res is not None:
        ins.append(res)
        specs.append(pl.BlockSpec((tm, tn), lambda i, j: (i, j)))
    scratch = [pltpu.VMEM((tm, xs[0].shape[1]), BF16)] if norm else []
    return pl.pallas_call(
        functools.partial(_mm_body, n_in=len(xs), norm=norm, res=res is not None),
        grid=(R // tm, N // tn),
        in_specs=specs,
        out_specs=pl.BlockSpec((tm, tn), lambda i, j: (i, j)),
        out_shape=jax.ShapeDtypeStruct((R, N), out_dtype),
        scratch_shapes=scratch,
        compiler_params=_cparams(("arbitrary", "arbitrary")),
        name="mm",
    )(*ins)


def _final_norm_body(x_ref, g_ref, o_ref):
    o_ref[...] = _rms(x_ref[...], g_ref[...])


def _final_norm(x, g, tm):
    R = x.shape[0]
    return pl.pallas_call(
        _final_norm_body,
        grid=(R // tm,),
        in_specs=[pl.BlockSpec((tm, D_MODEL), lambda i: (i, 0)), pl.BlockSpec((1, D_MODEL), lambda i: (0, 0))],
        out_specs=pl.BlockSpec((tm, D_MODEL), lambda i: (i, 0)),
        out_shape=jax.ShapeDtypeStruct((R, D_MODEL), F32),
        compiler_params=_cparams(("arbitrary",)),
        name="final_norm",
    )(x, g.reshape(1, -1))


def _ret_body(q_ref, k_ref, v_ref, ga_ref, cos_ref, sin_ref, s0_ref, gn_ref, y_ref, s_ref, *, C, c_true, bb):
    @pl.when(pl.program_id(1) == 0)
    def _():
        s_ref[...] = s0_ref[...]

    cosf, sinf = cos_ref[...], sin_ref[...]
    ti = lax.broadcasted_iota(jnp.int32, (C, C), 0)
    si = lax.broadcasted_iota(jnp.int32, (C, C), 1)
    dif = (ti - si).astype(F32)
    tcol = lax.broadcasted_iota(jnp.int32, (C, 1), 0).astype(F32)
    for h in range(H_A):
        lg = float(np.log1p(-(2.0 ** (-5.0 - h))))
        mask = jnp.where(dif >= 0, jnp.exp(lg * jnp.maximum(dif, 0.0)), 0.0)
        dec_in = jnp.exp((tcol + 1.0) * lg)
        dec_k = jnp.exp((c_true - 1.0 - tcol) * lg)
        for s in range(bb):
            rs = slice(s * C, (s + 1) * C)
            q = q_ref[rs, h * DK_A:(h + 1) * DK_A]
            k = k_ref[rs, h * DK_A:(h + 1) * DK_A]
            q = q * cosf + pltpu.roll(q, DK_A // 2, 1) * sinf
            k = (k * cosf + pltpu.roll(k, DK_A // 2, 1) * sinf) * (DK_A ** -0.5)
            vb = v_ref[rs, h * DV_A:(h + 1) * DV_A].astype(BF16)
            qb = q.astype(BF16)
            sc = _dot_nt(qb, k.astype(BF16)) * mask
            S = s_ref[s, h]
            o = _dot(sc.astype(BF16), vb) + _dot(qb, S.astype(BF16)) * dec_in
            s_ref[s, h] = float(np.exp(c_true * lg)) * S + _dot_tn((k * dec_k).astype(BF16), vb)
            mu = jnp.mean(o, axis=-1, keepdims=True)
            d = o - mu
            var = jnp.mean(d * d, axis=-1, keepdims=True)
            on = d * lax.rsqrt(var + EPS) * gn_ref[:, h * DV_A:(h + 1) * DV_A]
            ga = ga_ref[rs, h * DV_A:(h + 1) * DV_A]
            y_ref[rs, h * DV_A:(h + 1) * DV_A] = (ga * jax.nn.sigmoid(ga) * on).astype(BF16)


def _retention(z, cosf, sinf, s0, gn, *, B, T, C, c_true, shared, bb=1):
    NC = T // C
    assert bb == 1 or (NC == 1 and not shared)
    smap = (lambda b, c: (0, 0, 0, 0)) if shared else (lambda b, c: (b, 0, 0, 0))
    row = lambda b, c: b * NC + c
    R = bb * C
    return pl.pallas_call(
        functools.partial(_ret_body, C=C, c_true=c_true, bb=bb),
        grid=(B // bb, NC),
        in_specs=[
            pl.BlockSpec((R, QK_W), lambda b, c: (row(b, c), 0)),
            pl.BlockSpec((R, QK_W), lambda b, c: (row(b, c), 1)),
            pl.BlockSpec((R, VG_W), lambda b, c: (row(b, c), 1)),
            pl.BlockSpec((R, VG_W), lambda b, c: (row(b, c), 2)),
            pl.BlockSpec((C, DK_A), lambda b, c: (c, 0)),
            pl.BlockSpec((C, DK_A), lambda b, c: (c, 0)),
            pl.BlockSpec((bb, H_A, DK_A, DV_A), smap),
            pl.BlockSpec((1, VG_W), lambda b, c: (0, 0)),
        ],
        out_specs=[
            pl.BlockSpec((R, VG_W), lambda b, c: (row(b, c), 0)),
            pl.BlockSpec((bb, H_A, DK_A, DV_A), lambda b, c: (b, 0, 0, 0)),
        ],
        out_shape=[
            jax.ShapeDtypeStruct((B * T, VG_W), BF16),
            jax.ShapeDtypeStruct((B, H_A, DK_A, DV_A), F32),
        ],
        compiler_params=_cparams(("arbitrary", "arbitrary")),
        name="retention",
    )(z, z, z, z, cosf, sinf, s0, gn.reshape(1, -1))


def _rope_tables(pos):
    half = DK_A // 2
    inv = ROPE_BASE ** (-jnp.linspace(0.0, 1.0, half, dtype=F32))
    ang = pos.astype(F32)[:, None] * inv[None, :]
    cos, sin = jnp.cos(ang), jnp.sin(ang)
    return jnp.concatenate([cos, cos], axis=-1), jnp.concatenate([-sin, sin], axis=-1)


def _lru_gates(xc, wa_ref, ba, wx_ref, bx, lam):
    ra, ia = [], []
    for gi in range(NB_B):
        xg = xc[:, gi * BS_B:(gi + 1) * BS_B].astype(BF16)
        ra.append(_dot(xg, wa_ref[gi]))
        ia.append(_dot(xg, wx_ref[gi]))
    r = jax.nn.sigmoid(jnp.concatenate(ra, axis=1) + ba)
    i = jax.nn.sigmoid(jnp.concatenate(ia, axis=1) + bx)
    log_a = (-LRU_C) * r * (-_log_sigmoid(lam))
    a = jnp.exp(log_a)
    u = jnp.sqrt(-_expm1(2.0 * log_a)) * (i * xc)
    return a, u


def _lru_seq_body(xb_ref, gb_ref, cw_ref, cb_ref, wa_ref, ba_ref, wx_ref, bx_ref, lam_ref, h0_ref, tail0_ref,
                  y_ref, hout_ref, cout_ref, xe_ref, hc_ref, *, Tc):
    @pl.when(pl.program_id(1) == 0)
    def _():
        xe_ref[0:TAIL, :] = tail0_ref[0]
        hc_ref[...] = h0_ref[0]

    x = xb_ref[...]
    xe_ref[TAIL:TAIL + Tc, :] = x
    cw = cw_ref[...]
    xc = cb_ref[...]
    for j in range(CONV_B - 1):
        off = TAIL - (CONV_B - 1) + j
        xc = xc + cw[j:j + 1] * xe_ref[off:off + Tc, :]
    xc = xc + cw[CONV_B - 1:CONV_B] * x
    a, u = _lru_gates(xc, wa_ref, ba_ref[...], wx_ref, bx_ref[...], lam_ref[...])
    row = lax.broadcasted_iota(jnp.int32, (Tc, 1), 0)
    d = 1
    while d < Tc:
        a_sh = pltpu.roll(a, d, 0)
        u_sh = pltpu.roll(u, d, 0)
        valid = row >= d
        u = jnp.where(valid, a * u_sh + u, u)
        a = jnp.where(valid, a * a_sh, a)
        d *= 2
    hs = a * hc_ref[...] + u
    y_ref[...] = (hs * jax.nn.gelu(gb_ref[...])).astype(BF16)
    hc_ref[...] = hs[Tc - 1:Tc]
    hout_ref[0] = hs[Tc - 1:Tc]
    cout_ref[0] = xe_ref[Tc + TAIL - (CONV_B - 1):Tc + TAIL, :]
    xe_ref[0:TAIL, :] = xe_ref[Tc:Tc + TAIL, :]


def _tail_rows(state):
    return jnp.pad(state, ((0, 0), (TAIL - state.shape[1], 0), (0, 0)))


def _lru_seq(z, p, h0, conv0, *, B, T, Tc, shared):
    NC = T // Tc
    smap = (lambda b, c: (0, 0, 0)) if shared else (lambda b, c: (b, 0, 0))
    row = lambda b, c: b * NC + c
    vec = lambda a: pl.BlockSpec((1, W_B), lambda b, c: (0, 0))
    wspec = pl.BlockSpec((NB_B, BS_B, BS_B), lambda b, c: (0, 0, 0))
    return pl.pallas_call(
        functools.partial(_lru_seq_body, Tc=Tc),
        grid=(B, NC),
        in_specs=[
            pl.BlockSpec((Tc, W_B), lambda b, c: (row(b, c), 3)),
            pl.BlockSpec((Tc, W_B), lambda b, c: (row(b, c), 4)),
            pl.BlockSpec((CONV_B, W_B), lambda b, c: (0, 0)), vec(0),
            wspec, vec(0), wspec, vec(0), vec(0),
            pl.BlockSpec((1, 1, W_B), smap),
            pl.BlockSpec((1, TAIL, W_B), smap),
        ],
        out_specs=[
            pl.BlockSpec((Tc, W_B), lambda b, c: (row(b, c), 0)),
            pl.BlockSpec((1, 1, W_B), lambda b, c: (b, 0, 0)),
            pl.BlockSpec((1, CONV_B - 1, W_B), lambda b, c: (b, 0, 0)),
        ],
        out_shape=[
            jax.ShapeDtypeStruct((B * T, W_B), BF16),
            jax.ShapeDtypeStruct((B, 1, W_B), F32),
            jax.ShapeDtypeStruct((B, CONV_B - 1, W_B), F32),
        ],
        scratch_shapes=[pltpu.VMEM((Tc + TAIL, W_B), F32), pltpu.VMEM((1, W_B), F32)],
        compiler_params=_cparams(("arbitrary", "arbitrary")),
        name="lru_seq",
    )(z, z, p["conv_w"], p["conv_b"], p["wa"], p["ba"], p["wx"], p["bx"], p["lam"], h0, _tail_rows(conv0))


def _lru_dec_body(xb_ref, gb_ref, s0_ref, s1_ref, s2_ref, cw_ref, cb_ref, wa_ref, ba_ref, wx_ref, bx_ref, lam_ref,
                  h0_ref, y_ref, hout_ref):
    cw = cw_ref[...]
    x = xb_ref[...]
    xc = cb_ref[...] + cw[0:1] * s0_ref[...] + cw[1:2] * s1_ref[...] + cw[2:3] * s2_ref[...] + cw[3:4] * x
    a, u = _lru_gates(xc, wa_ref, ba_ref[...], wx_ref, bx_ref[...], lam_ref[...])
    hs = a * h0_ref[...] + u
    y_ref[...] = (hs * jax.nn.gelu(gb_ref[...])).astype(BF16)
    hout_ref[...] = hs


def _lru_dec(z, p, h0, conv0):
    B = z.shape[0]
    full = lambda shape: pl.BlockSpec(shape, lambda i: (0,) * len(shape))
    return pl.pallas_call(
        _lru_dec_body,
        grid=(1,),
        in_specs=[
            pl.BlockSpec((B, W_B), lambda i: (0, 3)), pl.BlockSpec((B, W_B), lambda i: (0, 4)),
            full((B, W_B)), full((B, W_B)), full((B, W_B)),
            full((CONV_B, W_B)), full((1, W_B)),
            full((NB_B, BS_B, BS_B)), full((1, W_B)), full((NB_B, BS_B, BS_B)), full((1, W_B)), full((1, W_B)),
            full((B, W_B)),
        ],
        out_specs=[full((B, W_B)), full((B, W_B))],
        out_shape=[jax.ShapeDtypeStruct((B, W_B), BF16), jax.ShapeDtypeStruct((B, W_B), F32)],
        compiler_params=_cparams(("arbitrary",)),
        name="lru_dec",
    )(z, z, conv0[:, 0], conv0[:, 1], conv0[:, 2], p["conv_w"], p["conv_b"], p["wa"], p["ba"], p["wx"], p["bx"],
      p["lam"], h0)


def _ffn_seq_body(ug_ref, uv_ref, cw_ref, cb_ref, tail0_ref, m_ref, cout_ref, xe_ref, *, Tc):
    @pl.when(pl.program_id(1) == 0)
    def _():
        xe_ref[0:TAIL, :] = tail0_ref[0]

    x = ug_ref[...]
    xe_ref[TAIL:TAIL + Tc, :] = x
    cw = cw_ref[...]
    c = cb_ref[...]
    for j in range(CONV_F - 1):
        off = TAIL - (CONV_F - 1) + j
        c = c + cw[j:j + 1] * xe_ref[off:off + Tc, :]
    c = c + cw[CONV_F - 1:CONV_F] * x
    m_ref[...] = (jax.nn.gelu(c) * uv_ref[...]).astype(BF16)
    cout_ref[0] = xe_ref[Tc + TAIL - (CONV_F - 1):Tc + TAIL, :]
    xe_ref[0:TAIL, :] = xe_ref[Tc:Tc + TAIL, :]


def _ffn_seq(u, cw, cb, conv0, *, B, T, Tc, shared):
    NC = T // Tc
    smap = (lambda b, c: (0, 0, 0)) if shared else (lambda b, c: (b, 0, 0))
    row = lambda b, c: b * NC + c
    return pl.pallas_call(
        functools.partial(_ffn_seq_body, Tc=Tc),
        grid=(B, NC),
        in_specs=[
            pl.BlockSpec((Tc, D_FF), lambda b, c: (row(b, c), 0)),
            pl.BlockSpec((Tc, D_FF), lambda b, c: (row(b, c), 1)),
            pl.BlockSpec((CONV_F, D_FF), lambda b, c: (0, 0)),
            pl.BlockSpec((1, D_FF), lambda b, c: (0, 0)),
            pl.BlockSpec((1, TAIL, D_FF), smap),
        ],
        out_specs=[
            pl.BlockSpec((Tc, D_FF), lambda b, c: (row(b, c), 0)),
            pl.BlockSpec((1, CONV_F - 1, D_FF), lambda b, c: (b, 0, 0)),
        ],
        out_shape=[
            jax.ShapeDtypeStruct((B * T, D_FF), BF16),
            jax.ShapeDtypeStruct((B, CONV_F - 1, D_FF), F32),
        ],
        scratch_shapes=[pltpu.VMEM((Tc + TAIL, D_FF), F32)],
        compiler_params=_cparams(("arbitrary", "arbitrary")),
        name="ffn_seq",
    )(u, u, cw, cb, _tail_rows(conv0))


def _ffn_dec_body(ug_ref, uv_ref, s0_ref, s1_ref, cw_ref, cb_ref, m_ref):
    cw = cw_ref[...]
    c = cb_ref[...] + cw[0:1] * s0_ref[...] + cw[1:2] * s1_ref[...] + cw[2:3] * ug_ref[...]
    m_ref[...] = (jax.nn.gelu(c) * uv_ref[...]).astype(BF16)


def _ffn_dec(u, cw, cb, conv0):
    B = u.shape[0]
    full = lambda shape: pl.BlockSpec(shape, lambda i: (0,) * len(shape))
    return pl.pallas_call(
        _ffn_dec_body,
        grid=(1,),
        in_specs=[pl.BlockSpec((B, D_FF), lambda i: (0, 0)), pl.BlockSpec((B, D_FF), lambda i: (0, 1)),
                  full((B, D_FF)), full((B, D_FF)), full((CONV_F, D_FF)), full((1, D_FF))],
        out_specs=full((B, D_FF)),
        out_shape=jax.ShapeDtypeStruct((B, D_FF), BF16),
        compiler_params=_cparams(("arbitrary",)),
        name="ffn_dec",
    )(u, u, conv0[:, 0], conv0[:, 1], cw, cb)


def _ffn_fused_body(x_ref, g_ref, wup_ref, cw_ref, cb_ref, wdn_ref, tail0_ref, o_ref, cout_ref,
                    xn_ref, m_ref, tail_ref, *, tm, tc):
    @pl.when(pl.program_id(1) == 0)
    def _():
        tail_ref[...] = tail0_ref[0]

    x = x_ref[...]
    xn_ref[...] = _rms(x, g_ref[...]).astype(BF16)
    row = lax.broadcasted_iota(jnp.int32, (tm, 1), 0)
    for j in range(D_FF // tc):
        cols = slice(j * tc, (j + 1) * tc)
        ug = _dot(xn_ref[...], wup_ref[:, cols])
        uv = _dot(xn_ref[...], wup_ref[:, D_FF + j * tc:D_FF + (j + 1) * tc])
        t1 = tail_ref[TAIL - 1:TAIL, cols]
        t2 = tail_ref[TAIL - 2:TAIL - 1, cols]
        s1 = jnp.where(row == 0, t1, pltpu.roll(ug, 1, 0))
        s2 = jnp.where(row == 0, t2, jnp.where(row == 1, t1, pltpu.roll(ug, 2, 0)))
        c = cb_ref[:, cols] + cw_ref[0:1, cols] * s2 + cw_ref[1:2, cols] * s1 + cw_ref[2:3, cols] * ug
        m_ref[:, cols] = (jax.nn.gelu(c) * uv).astype(BF16)
        tail_ref[:, cols] = ug[tm - TAIL:tm]
        cout_ref[0, :, cols] = ug[tm - (CONV_F - 1):tm]
    o_ref[...] = x + _dot(m_ref[...], wdn_ref[...])


def _ffn_fused(x, g, w_up, cw, cb, w_down, conv0, *, B, T, tm, tc, shared):
    NT = T // tm
    smap = (lambda b, t: (0, 0, 0)) if shared else (lambda b, t: (b, 0, 0))
    tile = pl.BlockSpec((tm, D_MODEL), lambda b, t: (b * NT + t, 0))
    const = lambda a: pl.BlockSpec(a.shape, lambda b, t: (0, 0), pipeline_mode=pl.Buffered(1))
    return pl.pallas_call(
        functools.partial(_ffn_fused_body, tm=tm, tc=tc),
        grid=(B, NT),
        in_specs=[tile, pl.BlockSpec((1, D_MODEL), lambda b, t: (0, 0)), const(w_up),
                  pl.BlockSpec((CONV_F, D_FF), lambda b, t: (0, 0)), pl.BlockSpec((1, D_FF), lambda b, t: (0, 0)),
                  const(w_down), pl.BlockSpec((1, TAIL, D_FF), smap)],
        out_specs=[tile, pl.BlockSpec((1, CONV_F - 1, D_FF), lambda b, t: (b, 0, 0))],
        out_shape=[jax.ShapeDtypeStruct((B * T, D_MODEL), F32),
                   jax.ShapeDtypeStruct((B, CONV_F - 1, D_FF), F32)],
        scratch_shapes=[pltpu.VMEM((tm, D_MODEL), BF16), pltpu.VMEM((tm, D_FF), BF16),
                        pltpu.VMEM((TAIL, D_FF), F32)],
        compiler_params=_cparams(("arbitrary", "arbitrary")),
        name="ffn_fused",
    )(x, g.reshape(1, -1), w_up, cw, cb, w_down, _tail_rows(conv0))


_PREP_W = ("mix", "w_r", "w_k", "w_v", "w0", "w1", "w2", "a0", "a1", "a2", "g1", "g2")
_PREP_VP = ("v0", "v1", "v2")


def _rwkv_prep_body(*refs, has_vp, carry):
    it = iter(refs)
    x_ref, gn_ref = next(it), next(it)
    p = {n: next(it) for n in _PREP_W}
    if has_vp:
        p.update({n: next(it) for n in _PREP_VP})
        vf_ref = next(it)
    prev_ref = next(it)
    r_ref, lw_ref, k_ref, v_ref, as_ref, g_ref, hn_ref = (next(it) for _ in range(7))
    h = _rms(x_ref[...], gn_ref[...])
    if carry:
        carry_ref = next(it)

        @pl.when(pl.program_id(1) == 0)
        def _():
            carry_ref[...] = prev_ref[0]

        row = lax.broadcasted_iota(jnp.int32, (h.shape[0], 1), 0)
        hprev = jnp.where(row == 0, carry_ref[...], pltpu.roll(h, 1, 0))
        carry_ref[...] = h[h.shape[0] - 1:]
        hn_ref[0] = h[h.shape[0] - 1:]
    else:
        hprev = prev_ref[...]
        hn_ref[...] = h
    xx = hprev - h
    mix = p["mix"][...]
    xs = [(h + xx * mix[n:n + 1]).astype(BF16) for n in range(6)]
    xr, xw, xk, xv, xa, xg = xs
    r_ref[...] = _dot(xr, p["w_r"][...])
    k_ref[...] = _dot(xk, p["w_k"][...])
    v = _dot(xv, p["w_v"][...])
    wl = p["w0"][...] + _dot(jnp.tanh(_dot(xw, p["w1"][...])).astype(BF16), p["w2"][...])
    lw_ref[...] = -jnp.exp(_log_sigmoid(wl) - 0.5)
    if has_vp:
        gate = jax.nn.sigmoid(p["v0"][...] + _dot(_dot(xv, p["v1"][...]).astype(BF16), p["v2"][...]))
        v = v + (vf_ref[...] - v) * gate
    v_ref[...] = v
    as_ref[...] = jax.nn.sigmoid(p["a0"][...] + _dot(_dot(xa, p["a1"][...]).astype(BF16), p["a2"][...]))
    g_ref[...] = _dot(jax.nn.sigmoid(_dot(xg, p["g1"][...])).astype(BF16), p["g2"][...])


def _rwkv_prep(x, gn, p, vp, v_first, prev, *, B, T, tm, carry, shared=False):
    R = x.shape[0]
    has_vp = vp is not None
    if carry:
        NT = T // tm
        grid = (B, NT)
        rowmap = lambda b, t: (b * NT + t, 0)
        cmap2 = lambda b, t: (0, 0)
        pmap = (lambda b, t: (0, 0, 0)) if shared else (lambda b, t: (b, 0, 0))
        prev_spec = pl.BlockSpec((1, 1, D_MODEL), pmap)
        hn_spec = pl.BlockSpec((1, 1, D_MODEL), lambda b, t: (b, 0, 0))
        hn_shape = jax.ShapeDtypeStruct((B, 1, D_MODEL), F32)
        sem = ("arbitrary", "arbitrary")
    else:
        grid = (R // tm,)
        rowmap = lambda i: (i, 0)
        cmap2 = lambda i: (0, 0)
        prev_spec = pl.BlockSpec((tm, D_MODEL), rowmap)
        hn_spec = pl.BlockSpec((tm, D_MODEL), rowmap)
        hn_shape = jax.ShapeDtypeStruct((R, D_MODEL), F32)
        sem = ("arbitrary",)
    tile = pl.BlockSpec((tm, D_MODEL), rowmap)
    ins = [x, gn.reshape(1, -1)]
    specs = [tile, pl.BlockSpec((1, D_MODEL), cmap2)]
    names = _PREP_W + (_PREP_VP if has_vp else ())
    src = dict(p)
    if has_vp:
        src.update(vp)
    for n in names:
        ins.append(src[n])
        specs.append(pl.BlockSpec(src[n].shape, cmap2))
    if has_vp:
        ins.append(v_first)
        specs.append(tile)
    ins.append(prev)
    specs.append(prev_spec)
    outs = pl.pallas_call(
        functools.partial(_rwkv_prep_body, has_vp=has_vp, carry=carry),
        grid=grid,
        in_specs=specs,
        out_specs=[tile] * 6 + [hn_spec],
        out_shape=[jax.ShapeDtypeStruct((R, D_MODEL), F32)] * 6 + [hn_shape],
        scratch_shapes=[pltpu.VMEM((1, D_MODEL), F32)] if carry else [],
        compiler_params=_cparams(sem),
        name="rwkv_prep",
    )(*ins)
    return outs


def _rwkv_chunk_body(r_ref, lw_ref, k_ref, v_ref, as_ref, g_ref, kk_ref, ka_ref, rk_ref, gg_ref, gb_ref, s0_ref,
                     o_ref, s_ref, *, Tc, C, n_pair, unroll):
    @pl.when(pl.program_id(2) == 0)
    def _():
        s_ref[...] = s0_ref[...]

    C2 = 2 * C
    lane = lax.broadcasted_iota(jnp.int32, (1, LANES), 1)
    m0 = (lane < HS_C).astype(F32)
    m1 = 1.0 - m0
    li = lax.broadcasted_iota(jnp.int32, (LANES, LANES), 0)
    lj = lax.broadcasted_iota(jnp.int32, (LANES, LANES), 1)
    same_head = (li >= HS_C) == (lj >= HS_C)
    bd_ones = same_head.astype(F32).astype(BF16)
    bd_avg = (same_head.astype(F32) * (1.0 / HS_C)).astype(BF16)
    ti = lax.broadcasted_iota(jnp.int32, (C, C), 0)
    si = lax.broadcasted_iota(jnp.int32, (C, C), 1)
    tri = (ti >= si).astype(F32).astype(BF16)
    r2 = lax.broadcasted_iota(jnp.int32, (C2, C2), 0)
    c2 = lax.broadcasted_iota(jnp.int32, (C2, C2), 1)
    same = (r2 >= C) == (c2 >= C)
    tt, ss = r2 & (C - 1), c2 & (C - 1)
    m_strict = same & (tt > ss)
    m_incl = same & (tt >= ss)
    same_sub = (tt & -RWKV_SUB) == (ss & -RWKV_SUB)
    m_sub = m_strict & same_sub
    m_off = m_strict & jnp.logical_not(same_sub)
    eye = (r2 == c2).astype(F32)
    nb = C // RWKV_SUB
    stack = lambda x: jnp.concatenate([x * m0, x * m1], axis=0)
    dup = lambda x: jnp.concatenate([x, x], axis=0)

    def sub_chunk(ci, carry):
        row_l = [pl.ds(pl.multiple_of((ci * unroll + u) * C, C), C) for u in range(unroll)]
        rows = [rw for rw in row_l for _ in range(n_pair)]
        lanes = [slice(p * LANES, (p + 1) * LANES) for p in range(n_pair)] * unroll
        each = lambda f, *cols: [f(*xs) for xs in zip(*cols)]

        def load(rw, ls):
            r, lw, k, v = r_ref[rw, ls], lw_ref[rw, ls], k_ref[rw, ls], v_ref[rw, ls]
            asig = as_ref[rw, ls]
            kk = k * kk_ref[:, ls]
            k2 = k * (1.0 + (asig - 1.0) * ka_ref[:, ls])
            return r, lw, k2, v, asig, kk

        r, lw, k2, v, asig, kk = zip(*each(load, rows, lanes))
        ssq = each(lambda x: _dot(_bf(x * x), bd_ones), kk)
        cs = each(lambda x: sum(_dot(tri, part) for part in _split(x)), lw)
        kk = each(lambda x, q: x * lax.rsqrt(jnp.maximum(q, 1e-24)), kk, ssq)
        b = each(lambda x, q: x * q, kk, asig)
        cend = [x[C - 1:C] for x in cs]
        e_neg = each(lambda x: jnp.exp(-x), cs)
        e_end = each(lambda x, y: jnp.exp(y - x), cs, cend)
        As = each(lambda x, c, l: _split(stack(-x * jnp.exp(c - l))), kk, cs, lw)
        Bd = each(lambda x, e: _split(dup(x * e)), b, e_neg)
        Rs = each(lambda x, c: _bf(stack(x * jnp.exp(c))), r, cs)
        Vs = each(lambda x: _bf(stack(x)), v)
        Kd = each(lambda x, e: _bf(dup(x * e)), k2, e_neg)
        G = each(lambda x, y: _dot3(x, y, _dot_nt), As, Bd)
        Aak = each(lambda x, y: _bf(jnp.where(m_strict, _dot_nt(x[0], y), 0.0)), As, Kd)
        Arb = each(lambda x, y: _bf(jnp.where(m_incl, _dot_nt(x, y[0]), 0.0)), Rs, Bd)
        Ark = each(lambda x, y: _bf(jnp.where(m_incl, _dot_nt(x, y), 0.0)), Rs, Kd)
        Nd = each(lambda x: jnp.where(m_sub, x, 0.0), G)
        P = each(lambda x: eye + x, Nd)
        Qs = each(_split, Nd)
        for _ in range(3):
            Qs = each(lambda q: _split(_dot3(q, q, _dot)), Qs)
            P = each(lambda x, q: x + _dot3(_split(x), q, _dot), P, Qs)
        if nb > 1:
            Pb = each(_bf, P)
            M = each(lambda x, y: _dot(x, _bf(jnp.where(m_off, y, 0.0))), Pb, G)
            Tm = each(lambda x: eye + x, M)
            for _ in range(int(np.ceil(np.log2(nb))) - 1):
                M = each(lambda x: _dot(_bf(x), _bf(x)), M)
                Tm = each(lambda x, y: x + _dot(_bf(x), _bf(y)), Tm, M)
            Tinv = each(lambda x, y: _bf(_dot(_bf(x), y)), Tm, Pb)
        else:
            Tinv = each(_bf, P)
        bE = each(lambda x, e: _bf(stack(x * e)), b, e_end)
        kE = each(lambda x, e: _bf(stack(x * e)), k2, e_end)
        gC = each(jnp.exp, cend)
        Ys = []
        for u in range(unroll):
            sl = slice(u * n_pair, (u + 1) * n_pair)
            S = [s_ref[0, p] for p in range(n_pair)]
            Sb = each(_bf, S)
            X = each(lambda a_, s_, k_, v_: _bf(_dot_nt(a_[0], s_) + _dot(k_, v_)), As[sl], Sb, Aak[sl], Vs[sl])
            Us = each(lambda t_, x_: _bf(_dot(t_, x_)), Tinv[sl], X)
            Ys += each(lambda r_, s_, b_, u_, k_, v_: _dot_nt(r_, s_) + _dot(b_, u_) + _dot(k_, v_),
                       Rs[sl], Sb, Arb[sl], Us, Ark[sl], Vs[sl])
            Sn = each(lambda s_, g_, u_, b_, v_, k_: s_ * g_ + _dot_tn(u_, b_) + _dot_tn(v_, k_),
                      S, gC[sl], Us, bE[sl], Vs[sl], kE[sl])
            for p in range(n_pair):
                s_ref[0, p] = Sn[p]
        y = [x[:C] + x[C:] for x in Ys]
        mu = each(lambda x: _dot(_bf(x), bd_avg), y)
        d = each(lambda x, m: x - m, y, mu)
        var = each(lambda x: _dot(_bf(x * x), bd_avg), d)
        bonus = each(lambda r_, k_, ls: _dot(_bf(r_ * k_ * rk_ref[:, ls]), bd_ones), r, k2, lanes)
        for i, (rw, ls) in enumerate(zip(rows, lanes)):
            on = d[i] * lax.rsqrt(var[i] + GN_EPS_C) * gg_ref[:, ls] + gb_ref[:, ls]
            o_ref[rw, ls] = ((on + bonus[i] * v[i]) * g_ref[rw, ls]).astype(BF16)
        return carry

    lax.fori_loop(0, Tc // (C * unroll), sub_chunk, 0)


def _rwkv_chunk(arrs, p, s0, *, B, T, Tc, C, ppb, shared, unroll=1):
    NC = T // Tc
    NP = N_PAIR // ppb
    Wb = ppb * LANES
    tile = pl.BlockSpec((Tc, Wb), lambda b, q, c: (b * NC + c, q))
    vec = pl.BlockSpec((1, Wb), lambda b, q, c: (0, q))
    smap = (lambda b, q, c: (0, q, 0, 0)) if shared else (lambda b, q, c: (b, q, 0, 0))
    return pl.pallas_call(
        functools.partial(_rwkv_chunk_body, Tc=Tc, C=C, n_pair=ppb, unroll=unroll),
        grid=(B, NP, NC),
        in_specs=[tile] * 6 + [vec] * 5 + [pl.BlockSpec((1, ppb, LANES, LANES), smap)],
        out_specs=[tile, pl.BlockSpec((1, ppb, LANES, LANES), lambda b, q, c: (b, q, 0, 0))],
        out_shape=[jax.ShapeDtypeStruct((B * T, D_MODEL), BF16),
                   jax.ShapeDtypeStruct((B, N_PAIR, LANES, LANES), F32)],
        compiler_params=_cparams(("arbitrary", "arbitrary", "arbitrary")),
        name="rwkv_chunk",
    )(*arrs, p["k_k"], p["k_a"], p["r_k"], p["gn_g"], p["gn_b"], s0)


def _rwkv_dec_body(r_ref, lw_ref, k_ref, v_ref, as_ref, g_ref, kk_ref, ka_ref, rk_ref, gg_ref, gb_ref, s0_ref,
                   o_ref, s_ref, r_scr, w_scr, k2_scr, v_scr, a_scr, b_scr, y_scr, *, nb):
    n_head = 2 * N_PAIR
    ri = lax.broadcasted_iota(jnp.int32, (HS_C, HS_C), 0)
    ci = lax.broadcasted_iota(jnp.int32, (HS_C, HS_C), 1)
    eye = (ri == ci).astype(F32)
    for h in range(n_head):
        ls = slice(h * HS_C, (h + 1) * HS_C)
        k, asig = k_ref[:, ls], as_ref[:, ls]
        kk = k * kk_ref[:, ls]
        kk = kk * lax.rsqrt(jnp.maximum(jnp.sum(kk * kk, axis=1, keepdims=True), 1e-24))
        a_scr[h] = -kk
        b_scr[h] = kk * asig
        k2_scr[h] = k * (1.0 + (asig - 1.0) * ka_ref[:, ls])
        r_scr[h] = r_ref[:, ls]
        v_scr[h] = v_ref[:, ls]
        w_scr[h] = jnp.exp(lw_ref[:, ls])

    def per_sample(s, carry):
        row = pl.ds(s, 1)
        for h in range(n_head):
            S = s0_ref[s, h]
            sa = jnp.sum(S * a_scr[h, row, :], axis=1, keepdims=True)
            v_col = jnp.sum(eye * v_scr[h, row, :], axis=1, keepdims=True)
            Sn = S * w_scr[h, row, :] + sa * b_scr[h, row, :] + v_col * k2_scr[h, row, :]
            s_ref[s, h] = Sn
            y_col = jnp.sum(Sn * r_scr[h, row, :], axis=1, keepdims=True)
            y_scr[h, row, :] = jnp.sum(eye * y_col, axis=0, keepdims=True)
        return carry

    lax.fori_loop(0, nb, per_sample, 0)
    for h in range(n_head):
        ls = slice(h * HS_C, (h + 1) * HS_C)
        y = y_scr[h]
        d = y - jnp.mean(y, axis=1, keepdims=True)
        var = jnp.mean(d * d, axis=1, keepdims=True)
        on = d * lax.rsqrt(var + GN_EPS_C) * gg_ref[:, ls] + gb_ref[:, ls]
        bonus = jnp.sum(r_scr[h] * k2_scr[h] * rk_ref[:, ls], axis=1, keepdims=True) * v_scr[h]
        o_ref[:, ls] = (on + bonus) * g_ref[:, ls]


def _rwkv_dec(arrs, p, s0, *, nb):
    B = s0.shape[0]
    n_head = 2 * N_PAIR
    tile = pl.BlockSpec((nb, D_MODEL), lambda i: (i, 0))
    vec = pl.BlockSpec((1, D_MODEL), lambda i: (0, 0))
    sspec = pl.BlockSpec((nb, n_head, HS_C, HS_C), lambda i: (i, 0, 0, 0))
    return pl.pallas_call(
        functools.partial(_rwkv_dec_body, nb=nb),
        grid=(B // nb,),
        in_specs=[tile] * 6 + [vec] * 5 + [sspec],
        out_specs=[tile, sspec],
        out_shape=[jax.ShapeDtypeStruct((B, D_MODEL), F32), jax.ShapeDtypeStruct(s0.shape, F32)],
        scratch_shapes=[pltpu.VMEM((n_head, nb, HS_C), F32)] * 7,
        compiler_params=_cparams(("arbitrary",)),
        name="rwkv_dec",
    )(*arrs, p["k_k"], p["k_a"], p["r_k"], p["gn_g"], p["gn_b"], s0)


def _to_blockdiag(s):
    even = jnp.pad(s[:, 0::2], ((0, 0), (0, 0), (0, 0), (0, HS_C)))
    odd = jnp.pad(s[:, 1::2], ((0, 0), (0, 0), (0, 0), (HS_C, 0)))
    return jnp.concatenate([even, odd], axis=2)


def _from_blockdiag(s):
    B = s.shape[0]
    return jnp.stack([s[:, :, :HS_C, :HS_C], s[:, :, HS_C:, HS_C:]], axis=2).reshape(B, 2 * N_PAIR, HS_C, HS_C)


def _pad_rows(x):
    B, W = x.shape
    return jnp.pad(x[:, None, :], ((0, 0), (0, PAD_T - 1), (0, 0))).reshape(B * PAD_T, W)


def _first_rows(x):
    return x.reshape(-1, PAD_T, x.shape[-1])[:, 0]


MAIN_TM = 2048
MAIN_TN = 512
RET_C = 128
LRU_TC = 256
FFN_TC = 256
FFN_TM = 512
FFN_TCOL = 256
PREP_TM = 256
RWKV_TC = 256
RWKV_C = 64
RWKV_PPB = 8
RWKV_UNROLL = 2
RWKV_DEC_NB = 8
RET_DEC_NB = 8


def kernel(x_prompt, x_sample, state_ret, state_lru, state_lru_conv, state_rwkv, state_shift, state_ffn_conv,
           meta_tokens, norm_mix_g, norm_ffn_g, norm_final_g,
           ev_w_in, ev_ret_gn_g, ev_lru_conv_w, ev_lru_conv_b, ev_lru_wa, ev_lru_ba, ev_lru_wx, ev_lru_bx,
           ev_lru_lambda, ev_w_out,
           od_mix, od_w_r, od_w_k, od_w_v, od_w0, od_w1, od_w2, od_a0, od_a1, od_a2, od_v0, od_v1, od_v2,
           od_g1, od_g2, od_k_k, od_k_a, od_r_k, od_gn_g, od_gn_b, od_w_o,
           ff_w_up, ff_conv_w, ff_conv_b, ff_w_down):
    BP, TP, _ = x_prompt.shape
    BS = x_sample.shape[0]
    NS = BS + N_META
    bf = lambda a: a.astype(BF16)
    row = lambda a: a.reshape(1, -1)

    def lora_in(w):
        return bf(jnp.pad(w, ((0, 0), (0, LORA_PAD - w.shape[1]))))

    def lora_out(w):
        return bf(jnp.pad(w, ((0, LORA_PAD - w.shape[0]), (0, 0))))

    xm = x_prompt.reshape(BP * TP, D_MODEL)
    xs = jnp.concatenate([x_sample.reshape(BS, D_MODEL), meta_tokens.astype(x_prompt.dtype)], axis=0)

    cos_m, sin_m = _rope_tables(N_META + jnp.arange(TP, dtype=jnp.int32))
    cos_t, sin_t = _rope_tables(jnp.arange(N_META, dtype=jnp.int32))
    cos_s, sin_s = _rope_tables(PAST_LEN + jnp.arange(PAD_T, dtype=jnp.int32))

    out = {k: [] for k in ("ret_p", "lru_p", "lconv_p", "rwkv_p", "shift_p", "ffn_p",
                           "ret_s", "lru_s", "lconv_s", "rwkv_s", "shift_s", "ffn_s")}
    v_first_m = v_first_s = None
    for li in range(4):
        j = li // 2
        if li % 2 == 0:
            w_in = bf(ev_w_in[j])
            w_out_a, w_out_b = bf(ev_w_out[j][:VG_W]), bf(ev_w_out[j][VG_W:])
            gn = ev_ret_gn_g[j]
            lp = dict(conv_w=ev_lru_conv_w[j], conv_b=row(ev_lru_conv_b[j]), wa=bf(ev_lru_wa[j]),
                      ba=row(ev_lru_ba[j]), wx=bf(ev_lru_wx[j]), bx=row(ev_lru_bx[j]), lam=row(ev_lru_lambda[j]))
            zs = _mm([xs], [w_in], g=norm_mix_g[li], tm=NS, tn=MAIN_TN)
            z_smp, z_meta = zs[:BS], zs[BS:]
            ya_t, ret_t = _retention(z_meta, cos_t, sin_t, jnp.zeros((1, H_A, DK_A, DV_A), F32), gn,
                                     B=1, T=N_META, C=N_META, c_true=N_META, shared=False)
            yb_t, lru_t, lconv_t = _lru_seq(z_meta, lp, jnp.zeros((1, 1, W_B), F32),
                                            jnp.zeros((1, CONV_B - 1, W_B), F32), B=1, T=N_META, Tc=N_META,
                                            shared=False)
            ya_s, ret_s = _retention(_pad_rows(z_smp), cos_s, sin_s, state_ret[j], gn,
                                     B=BS, T=PAD_T, C=PAD_T, c_true=1, shared=False, bb=RET_DEC_NB)
            yb_s, lru_s = _lru_dec(z_smp, lp, state_lru[j], state_lru_conv[j])
            lconv_s = jnp.concatenate([state_lru_conv[j][:, 1:], z_smp[:, None, 3 * W_B:4 * W_B]], axis=1)
            ya = jnp.concatenate([_first_rows(ya_s), ya_t], axis=0)
            yb = jnp.concatenate([yb_s, yb_t], axis=0)
            xs = _mm([ya, yb], [w_out_a, w_out_b], res=xs, tm=NS, tn=MAIN_TN)
            zm = _mm([xm], [w_in], g=norm_mix_g[li], tm=MAIN_TM, tn=MAIN_TN)
            ya_m, ret_m = _retention(zm, cos_m, sin_m, ret_t, gn, B=BP, T=TP, C=RET_C, c_true=RET_C, shared=True)
            yb_m, lru_m, lconv_m = _lru_seq(zm, lp, lru_t, lconv_t, B=BP, T=TP, Tc=LRU_TC, shared=True)
            xm = _mm([ya_m, yb_m], [w_out_a, w_out_b], res=xm, tm=MAIN_TM, tn=MAIN_TN)
            out["ret_p"].append(ret_m)
            out["lru_p"].append(lru_m[:, 0])
            out["lconv_p"].append(lconv_m)
            out["ret_s"].append(ret_s)
            out["lru_s"].append(lru_s)
            out["lconv_s"].append(lconv_s)
        else:
            p = dict(mix=od_mix[j], w_r=bf(od_w_r[j]), w_k=bf(od_w_k[j]), w_v=bf(od_w_v[j]), w0=row(od_w0[j]),
                     w1=lora_in(od_w1[j]), w2=lora_out(od_w2[j]), a0=row(od_a0[j]), a1=lora_in(od_a1[j]),
                     a2=lora_out(od_a2[j]), g1=lora_in(od_g1[j]), g2=lora_out(od_g2[j]),
                     k_k=row(od_k_k[j]), k_a=row(od_k_a[j]), r_k=row(od_r_k[j]), gn_g=row(od_gn_g[j]),
                     gn_b=row(od_gn_b[j]))
            vp = None
            if j > 0:
                vp = dict(v0=row(od_v0[j - 1]), v1=lora_in(od_v1[j - 1]), v2=lora_out(od_v2[j - 1]))
            w_o = bf(od_w_o[j])
            gmix = norm_mix_g[li]
            pre_t = _rwkv_prep(xs[BS:], gmix, p, vp, None if vp is None else v_first_s[BS:],
                               jnp.zeros((1, 1, D_MODEL), F32), B=1, T=N_META, tm=N_META, carry=True)
            o_t, rw_t = _rwkv_chunk(pre_t[:6], p, jnp.zeros((1, N_PAIR, LANES, LANES), F32),
                                    B=1, T=N_META, Tc=N_META, C=N_META, ppb=RWKV_PPB, shared=False)
            shift_t = pre_t[6]
            pre_s = _rwkv_prep(xs[:BS], gmix, p, vp, None if vp is None else v_first_s[:BS],
                               state_shift[j], B=BS, T=1, tm=BS, carry=False)
            o_s, rw_s = _rwkv_dec(pre_s[:6], p, state_rwkv[j], nb=RWKV_DEC_NB)
            if vp is None:
                v_first_s = jnp.concatenate([pre_s[3], pre_t[3]], axis=0)
            o_small = jnp.concatenate([o_s, o_t.astype(F32)], axis=0)
            xs = _mm([o_small], [w_o], res=xs, tm=NS, tn=MAIN_TN)
            pre_m = _rwkv_prep(xm, gmix, p, vp, v_first_m, shift_t, B=BP, T=TP, tm=PREP_TM, carry=True, shared=True)
            if vp is None:
                v_first_m = pre_m[3]
            o_m, rw_m = _rwkv_chunk(pre_m[:6], p, rw_t, B=BP, T=TP, Tc=RWKV_TC, C=RWKV_C, ppb=RWKV_PPB, shared=True,
                                    unroll=RWKV_UNROLL)
            xm = _mm([o_m], [w_o], res=xm, tm=MAIN_TM, tn=MAIN_TN)
            out["rwkv_p"].append(_from_blockdiag(rw_m))
            out["shift_p"].append(pre_m[6][:, 0])
            out["rwkv_s"].append(rw_s)
            out["shift_s"].append(pre_s[6])
        w_up, w_down = bf(ff_w_up[li]), bf(ff_w_down[li])
        cw, cb = ff_conv_w[li], row(ff_conv_b[li])
        us = _mm([xs], [w_up], g=norm_ffn_g[li], tm=NS, tn=MAIN_TN)
        m_t, ffn_t = _ffn_seq(us[BS:], cw, cb, jnp.zeros((1, CONV_F - 1, D_FF), F32), B=1, T=N_META, Tc=N_META,
                              shared=False)
        m_s = _ffn_dec(us[:BS], cw, cb, state_ffn_conv[li])
        ffn_s = jnp.concatenate([state_ffn_conv[li][:, 1:], us[:BS, None, :D_FF]], axis=1)
        xs = _mm([jnp.concatenate([m_s, m_t], axis=0)], [w_down], res=xs, tm=NS, tn=MAIN_TN)
        xm, ffn_m = _ffn_fused(xm, norm_ffn_g[li], w_up, cw, cb, w_down, ffn_t, B=BP, T=TP, tm=FFN_TM, tc=FFN_TCOL,
                               shared=True)
        out["ffn_p"].append(ffn_m)
        out["ffn_s"].append(ffn_s)

    y_prompt = _final_norm(xm, norm_final_g, MAIN_TM).reshape(BP, TP, D_MODEL)
    y_sample = _final_norm(xs, norm_final_g, NS)[:BS].reshape(BS, 1, D_MODEL)
    st = lambda k: jnp.stack(out[k])
    return (y_prompt, y_sample,
            st("ret_p"), st("lru_p"), st("lconv_p"), st("rwkv_p"), st("shift_p"), st("ffn_p"),
            st("ret_s"), st("lru_s"), st("lconv_s"), st("rwkv_s"), st("shift_s"), st("ffn_s"))
```

```python
import functools

import numpy as np
import jax
import jax.numpy as jnp
from jax import lax
from jax.experimental import pallas as pl
from jax.experimental.pallas import tpu as pltpu

F32 = jnp.float32
BF16 = jnp.bfloat16
HI = lax.Precision.HIGHEST

D_MODEL = 1024
N_META = 16
PAST_LEN = 16384
H_A, DK_A, DV_A = 4, 128, 256
W_B, NB_B, BS_B, CONV_B = 1024, 8, 128, 4
LRU_C = 8.0
HS_C = 64
N_PAIR = D_MODEL // (2 * HS_C)
LORA_PAD = 128
GN_EPS_C = 64e-5
D_FF, CONV_F = 2816, 3
EPS = 1e-6
ROPE_BASE = 10000.0
QK_W, VG_W = H_A * DK_A, H_A * DV_A

LANES = 128
TAIL = 8
VMEM_LIMIT = 52 * 1024 * 1024
PAD_T = 16
RWKV_SUB = 16


def _cparams(sem):
    return pltpu.CompilerParams(dimension_semantics=sem, vmem_limit_bytes=VMEM_LIMIT)


def _dot(a, b, prec=None):
    return jnp.dot(a, b, preferred_element_type=F32, precision=prec)


def _dot_nt(a, b, prec=None):
    return lax.dot_general(a, b, (((1,), (1,)), ((), ())), preferred_element_type=F32, precision=prec)


def _dot_tn(a, b, prec=None):
    return lax.dot_general(a, b, (((0,), (0,)), ((), ())), preferred_element_type=F32, precision=prec)


def _bf(x):
    return x.astype(BF16)


def _split(x):
    hi = x.astype(BF16)
    return hi, (x - hi.astype(F32)).astype(BF16)


def _dot3(a, b, dot):
    return dot(a[0], b[0]) + dot(a[0], b[1]) + dot(a[1], b[0])


def _dot3_wide(a, b):
    n = a[0].shape[1]
    lhs = jnp.concatenate([a[0], a[1]], axis=1)
    rhs = jnp.concatenate([jnp.concatenate([b[0], b[1]], axis=1),
                           jnp.concatenate([b[0], jnp.zeros_like(b[0])], axis=1)], axis=0)
    out = _dot(lhs, rhs)
    return out[:, :n] + out[:, n:]


def _dot3_wide_nt(a, b):
    n = b[0].shape[0]
    lhs = jnp.concatenate([a[0], a[1]], axis=1)
    rhs = jnp.concatenate([jnp.concatenate([b[0], b[0]], axis=1),
                           jnp.concatenate([b[1], jnp.zeros_like(b[1])], axis=1)], axis=0)
    out = _dot_nt(lhs, rhs)
    return out[:, :n] + out[:, n:]


def _rms(x, g):
    return x * lax.rsqrt(jnp.mean(x * x, axis=-1, keepdims=True) + EPS) * g


def _log_sigmoid(x):
    return jnp.minimum(x, 0.0) - jnp.log(1.0 + jnp.exp(-jnp.abs(x)))


def _mm_body(*refs, n_in, norm, res):
    it = iter(refs)
    x_refs = [next(it) for _ in range(n_in)]
    g_ref = next(it) if norm else None
    w_refs = [next(it) for _ in range(n_in)]
    r_ref = next(it) if res else None
    o_ref = next(it)
    if norm:
        xn_ref = next(it)

        @pl.when(pl.program_id(1) == 0)
        def _():
            xn_ref[...] = _rms(x_refs[0][...], g_ref[...]).astype(BF16)

        acc = _dot(xn_ref[...], w_refs[0][...])
    else:
        acc = _dot(x_refs[0][...].astype(BF16), w_refs[0][...])
        for x_ref, w_ref in zip(x_refs[1:], w_refs[1:]):
            acc = acc + _dot(x_ref[...].astype(BF16), w_ref[...])
    if res:
        acc = acc + r_ref[...]
    o_ref[...] = acc.astype(o_ref.dtype)


def _mm(xs, ws, *, g=None, res=None, tm, tn, out_dtype=F32):
    R = xs[0].shape[0]
    N = ws[0].shape[1]
    norm = g is not None
    ins, specs = [], []
    for x in xs:
        ins.append(x)
        specs.append(pl.BlockSpec((tm, x.shape[1]), lambda i, j: (i, 0)))
    if norm:
        ins.append(g.reshape(1, -1))
        specs.append(pl.BlockSpec((1, g.shape[-1]), lambda i, j: (0, 0)))
    for w in ws:
        ins.append(w)
        specs.append(pl.BlockSpec((w.shape[0], tn), lambda i, j: (0, j)))
    if res is not None:
        ins.append(res)
        specs.append(pl.BlockSpec((tm, tn), lambda i, j: (i, j)))
    scratch = [pltpu.VMEM((tm, xs[0].shape[1]), BF16)] if norm else []
    return pl.pallas_call(
        functools.partial(_mm_body, n_in=len(xs), norm=norm, res=res is not None),
        grid=(R // tm, N // tn),
        in_specs=specs,
        out_specs=pl.BlockSpec((tm, tn), lambda i, j: (i, j)),
        out_shape=jax.ShapeDtypeStruct((R, N), out_dtype),
        scratch_shapes=scratch,
        compiler_params=_cparams(("arbitrary", "arbitrary")),
        name="mm",
    )(*ins)


def _final_norm_body(x_ref, g_ref, o_ref):
    o_ref[...] = _rms(x_ref[...], g_ref[...])


def _final_norm(x, g, tm):
    R = x.shape[0]
    return pl.pallas_call(
        _final_norm_body,
        grid=(R // tm,),
        in_specs=[pl.BlockSpec((tm, D_MODEL), lambda i: (i, 0)), pl.BlockSpec((1, D_MODEL), lambda i: (0, 0))],
        out_specs=pl.BlockSpec((tm, D_MODEL), lambda i: (i, 0)),
        out_shape=jax.ShapeDtypeStruct((R, D_MODEL), F32),
        compiler_params=_cparams(("arbitrary",)),
        name="final_norm",
    )(x, g.reshape(1, -1))


def _ret_body(q_ref, k_ref, v_ref, ga_ref, cos_ref, sin_ref, s0_ref, gn_ref, y_ref, s_ref, *, C, c_true, bb, single):
    @pl.when(pl.program_id(1) == 0)
    def _():
        s_ref[...] = s0_ref[...]

    cosf, sinf = cos_ref[...], sin_ref[...]
    ti = lax.broadcasted_iota(jnp.int32, (C, C), 0)
    si = lax.broadcasted_iota(jnp.int32, (C, C), 1)
    dif = (ti - si).astype(F32)
    trow = lax.broadcasted_iota(jnp.int32, (C, 1), 0)
    tcol = trow.astype(F32)
    for h in range(H_A):
        lg = float(np.log1p(-(2.0 ** (-5.0 - h))))
        mask = jnp.where(dif >= 0, jnp.exp(lg * jnp.maximum(dif, 0.0)), 0.0)
        dec_in = jnp.exp((tcol + 1.0) * lg)
        dec_k = jnp.exp((c_true - 1.0 - tcol) * lg)
        for s in range(bb):
            rs = slice(s, s + 1) if single else slice(s * C, (s + 1) * C)
            chunk = (lambda x: jnp.where(trow == 0, x, 0.0)) if single else (lambda x: x)
            q = chunk(q_ref[rs, h * DK_A:(h + 1) * DK_A])
            k = chunk(k_ref[rs, h * DK_A:(h + 1) * DK_A])
            q = q * cosf + pltpu.roll(q, DK_A // 2, 1) * sinf
            k = (k * cosf + pltpu.roll(k, DK_A // 2, 1) * sinf) * (DK_A ** -0.5)
            vb = chunk(v_ref[rs, h * DV_A:(h + 1) * DV_A]).astype(BF16)
            qb = q.astype(BF16)
            sc = _dot_nt(qb, k.astype(BF16)) * mask
            S = s_ref[s, h]
            o = _dot(sc.astype(BF16), vb) + _dot(qb, S.astype(BF16)) * dec_in
            s_ref[s, h] = float(np.exp(c_true * lg)) * S + _dot_tn((k * dec_k).astype(BF16), vb)
            mu = jnp.mean(o, axis=-1, keepdims=True)
            d = o - mu
            var = jnp.mean(d * d, axis=-1, keepdims=True)
            on = d * lax.rsqrt(var + EPS) * gn_ref[:, h * DV_A:(h + 1) * DV_A]
            ga = ga_ref[rs, h * DV_A:(h + 1) * DV_A]
            if single:
                on = on[0:1]
            y_ref[rs, h * DV_A:(h + 1) * DV_A] = (ga * jax.nn.sigmoid(ga) * on).astype(y_ref.dtype)


def _retention(z, cosf, sinf, s0, gn, *, B, T, C, c_true, shared, bb=1, single=False):
    NC = 1 if single else T // C
    assert bb == 1 or (NC == 1 and not shared)
    assert not single or (T == 1 and c_true == 1)
    smap = (lambda b, c: (0, 0, 0, 0)) if shared else (lambda b, c: (b, 0, 0, 0))
    row = lambda b, c: b * NC + c
    R = bb if single else bb * C
    return pl.pallas_call(
        functools.partial(_ret_body, C=C, c_true=c_true, bb=bb, single=single),
        grid=(B // bb, NC),
        in_specs=[
            pl.BlockSpec((R, QK_W), lambda b, c: (row(b, c), 0)),
            pl.BlockSpec((R, QK_W), lambda b, c: (row(b, c), 1)),
            pl.BlockSpec((R, VG_W), lambda b, c: (row(b, c), 1)),
            pl.BlockSpec((R, VG_W), lambda b, c: (row(b, c), 2)),
            pl.BlockSpec((C, DK_A), lambda b, c: (c, 0)),
            pl.BlockSpec((C, DK_A), lambda b, c: (c, 0)),
            pl.BlockSpec((bb, H_A, DK_A, DV_A), smap),
            pl.BlockSpec((1, VG_W), lambda b, c: (0, 0)),
        ],
        out_specs=[
            pl.BlockSpec((R, VG_W), lambda b, c: (row(b, c), 0)),
            pl.BlockSpec((bb, H_A, DK_A, DV_A), lambda b, c: (b, 0, 0, 0)),
        ],
        out_shape=[
            jax.ShapeDtypeStruct((B * T, VG_W), F32 if single else BF16),
            jax.ShapeDtypeStruct((B, H_A, DK_A, DV_A), F32),
        ],
        compiler_params=_cparams(("arbitrary", "arbitrary")),
        name="retention",
    )(z, z, z, z, cosf, sinf, s0, gn.reshape(1, -1))


def _rope_tables(pos):
    half = DK_A // 2
    inv = ROPE_BASE ** (-jnp.linspace(0.0, 1.0, half, dtype=F32))
    ang = pos.astype(F32)[:, None] * inv[None, :]
    cos, sin = jnp.cos(ang), jnp.sin(ang)
    return jnp.concatenate([cos, cos], axis=-1), jnp.concatenate([-sin, sin], axis=-1)


def _lru_gates(xc, wa_ref, ba, wx_ref, bx, lam):
    ra, ia = [], []
    for gi in range(NB_B):
        xg = xc[:, gi * BS_B:(gi + 1) * BS_B].astype(BF16)
        ra.append(_dot(xg, wa_ref[gi]))
        ia.append(_dot(xg, wx_ref[gi]))
    r = jax.nn.sigmoid(jnp.concatenate(ra, axis=1) + ba)
    i = jax.nn.sigmoid(jnp.concatenate(ia, axis=1) + bx)
    log_a = (-LRU_C) * r * (-_log_sigmoid(lam))
    a = jnp.exp(log_a)
    u = jnp.sqrt(1.0 - a * a) * (i * xc)
    return a, u


def _lru_seq_body(xb_ref, gb_ref, cw_ref, cb_ref, wa_ref, ba_ref, wx_ref, bx_ref, lam_ref, h0_ref, tail0_ref,
                  y_ref, hout_ref, cout_ref, xe_ref, hc_ref, *, Tc):
    @pl.when(pl.program_id(1) == 0)
    def _():
        xe_ref[0:TAIL, :] = tail0_ref[0]
        hc_ref[...] = h0_ref[0]

    x = xb_ref[...]
    xe_ref[TAIL:TAIL + Tc, :] = x
    cw = cw_ref[...]
    xc = cb_ref[...]
    for j in range(CONV_B - 1):
        off = TAIL - (CONV_B - 1) + j
        xc = xc + cw[j:j + 1] * xe_ref[off:off + Tc, :]
    xc = xc + cw[CONV_B - 1:CONV_B] * x
    a, u = _lru_gates(xc, wa_ref, ba_ref[...], wx_ref, bx_ref[...], lam_ref[...])
    row = lax.broadcasted_iota(jnp.int32, (Tc, 1), 0)
    d = 1
    while d < Tc:
        a_sh = pltpu.roll(a, d, 0)
        u_sh = pltpu.roll(u, d, 0)
        valid = row >= d
        u = jnp.where(valid, a * u_sh + u, u)
        a = jnp.where(valid, a * a_sh, a)
        d *= 2
    hs = a * hc_ref[...] + u
    y_ref[...] = (hs * jax.nn.gelu(gb_ref[...])).astype(BF16)
    hc_ref[...] = hs[Tc - 1:Tc]
    hout_ref[0] = hs[Tc - 1:Tc]
    cout_ref[0] = xe_ref[Tc + TAIL - (CONV_B - 1):Tc + TAIL, :]
    xe_ref[0:TAIL, :] = xe_ref[Tc:Tc + TAIL, :]


def _tail_rows(state):
    return jnp.pad(state, ((0, 0), (TAIL - state.shape[1], 0), (0, 0)))


def _lru_seq(z, p, h0, conv0, *, B, T, Tc, shared):
    NC = T // Tc
    smap = (lambda b, c: (0, 0, 0)) if shared else (lambda b, c: (b, 0, 0))
    row = lambda b, c: b * NC + c
    vec = lambda a: pl.BlockSpec((1, W_B), lambda b, c: (0, 0))
    wspec = pl.BlockSpec((NB_B, BS_B, BS_B), lambda b, c: (0, 0, 0))
    return pl.pallas_call(
        functools.partial(_lru_seq_body, Tc=Tc),
        grid=(B, NC),
        in_specs=[
            pl.BlockSpec((Tc, W_B), lambda b, c: (row(b, c), 3)),
            pl.BlockSpec((Tc, W_B), lambda b, c: (row(b, c), 4)),
            pl.BlockSpec((CONV_B, W_B), lambda b, c: (0, 0)), vec(0),
            wspec, vec(0), wspec, vec(0), vec(0),
            pl.BlockSpec((1, 1, W_B), smap),
            pl.BlockSpec((1, TAIL, W_B), smap),
        ],
        out_specs=[
            pl.BlockSpec((Tc, W_B), lambda b, c: (row(b, c), 0)),
            pl.BlockSpec((1, 1, W_B), lambda b, c: (b, 0, 0)),
            pl.BlockSpec((1, CONV_B - 1, W_B), lambda b, c: (b, 0, 0)),
        ],
        out_shape=[
            jax.ShapeDtypeStruct((B * T, W_B), BF16),
            jax.ShapeDtypeStruct((B, 1, W_B), F32),
            jax.ShapeDtypeStruct((B, CONV_B - 1, W_B), F32),
        ],
        scratch_shapes=[pltpu.VMEM((Tc + TAIL, W_B), F32), pltpu.VMEM((1, W_B), F32)],
        compiler_params=_cparams(("arbitrary", "arbitrary")),
        name="lru_seq",
    )(z, z, p["conv_w"], p["conv_b"], p["wa"], p["ba"], p["wx"], p["bx"], p["lam"], h0, _tail_rows(conv0))


def _lru_dec_body(xb_ref, gb_ref, s0_ref, s1_ref, s2_ref, cw_ref, cb_ref, wa_ref, ba_ref, wx_ref, bx_ref, lam_ref,
                  h0_ref, y_ref, hout_ref):
    cw = cw_ref[...]
    x = xb_ref[...]
    xc = cb_ref[...] + cw[0:1] * s0_ref[...] + cw[1:2] * s1_ref[...] + cw[2:3] * s2_ref[...] + cw[3:4] * x
    a, u = _lru_gates(xc, wa_ref, ba_ref[...], wx_ref, bx_ref[...], lam_ref[...])
    hs = a * h0_ref[...] + u
    y_ref[...] = (hs * jax.nn.gelu(gb_ref[...])).astype(BF16)
    hout_ref[...] = hs


def _lru_dec(z, p, h0, conv0):
    B = z.shape[0]
    full = lambda shape: pl.BlockSpec(shape, lambda i: (0,) * len(shape))
    return pl.pallas_call(
        _lru_dec_body,
        grid=(1,),
        in_specs=[
            pl.BlockSpec((B, W_B), lambda i: (0, 3)), pl.BlockSpec((B, W_B), lambda i: (0, 4)),
            full((B, W_B)), full((B, W_B)), full((B, W_B)),
            full((CONV_B, W_B)), full((1, W_B)),
            full((NB_B, BS_B, BS_B)), full((1, W_B)), full((NB_B, BS_B, BS_B)), full((1, W_B)), full((1, W_B)),
            full((B, W_B)),
        ],
        out_specs=[full((B, W_B)), full((B, W_B))],
        out_shape=[jax.ShapeDtypeStruct((B, W_B), BF16), jax.ShapeDtypeStruct((B, W_B), F32)],
        compiler_params=_cparams(("arbitrary",)),
        name="lru_dec",
    )(z, z, conv0[:, 0], conv0[:, 1], conv0[:, 2], p["conv_w"], p["conv_b"], p["wa"], p["ba"], p["wx"], p["bx"],
      p["lam"], h0)


def _ffn_seq_body(ug_ref, uv_ref, cw_ref, cb_ref, tail0_ref, m_ref, cout_ref, xe_ref, *, Tc):
    @pl.when(pl.program_id(1) == 0)
    def _():
        xe_ref[0:TAIL, :] = tail0_ref[0]

    x = ug_ref[...]
    xe_ref[TAIL:TAIL + Tc, :] = x
    cw = cw_ref[...]
    c = cb_ref[...]
    for j in range(CONV_F - 1):
        off = TAIL - (CONV_F - 1) + j
        c = c + cw[j:j + 1] * xe_ref[off:off + Tc, :]
    c = c + cw[CONV_F - 1:CONV_F] * x
    m_ref[...] = (jax.nn.gelu(c) * uv_ref[...]).astype(BF16)
    cout_ref[0] = xe_ref[Tc + TAIL - (CONV_F - 1):Tc + TAIL, :]
    xe_ref[0:TAIL, :] = xe_ref[Tc:Tc + TAIL, :]


def _ffn_seq(u, cw, cb, conv0, *, B, T, Tc, shared):
    NC = T // Tc
    smap = (lambda b, c: (0, 0, 0)) if shared else (lambda b, c: (b, 0, 0))
    row = lambda b, c: b * NC + c
    return pl.pallas_call(
        functools.partial(_ffn_seq_body, Tc=Tc),
        grid=(B, NC),
        in_specs=[
            pl.BlockSpec((Tc, D_FF), lambda b, c: (row(b, c), 0)),
            pl.BlockSpec((Tc, D_FF), lambda b, c: (row(b, c), 1)),
            pl.BlockSpec((CONV_F, D_FF), lambda b, c: (0, 0)),
            pl.BlockSpec((1, D_FF), lambda b, c: (0, 0)),
            pl.BlockSpec((1, TAIL, D_FF), smap),
        ],
        out_specs=[
            pl.BlockSpec((Tc, D_FF), lambda b, c: (row(b, c), 0)),
            pl.BlockSpec((1, CONV_F - 1, D_FF), lambda b, c: (b, 0, 0)),
        ],
        out_shape=[
            jax.ShapeDtypeStruct((B * T, D_FF), BF16),
            jax.ShapeDtypeStruct((B, CONV_F - 1, D_FF), F32),
        ],
        scratch_shapes=[pltpu.VMEM((Tc + TAIL, D_FF), F32)],
        compiler_params=_cparams(("arbitrary", "arbitrary")),
        name="ffn_seq",
    )(u, u, cw, cb, _tail_rows(conv0))


def _ffn_dec_body(ug_ref, uv_ref, s0_ref, s1_ref, cw_ref, cb_ref, m_ref):
    cw = cw_ref[...]
    c = cb_ref[...] + cw[0:1] * s0_ref[...] + cw[1:2] * s1_ref[...] + cw[2:3] * ug_ref[...]
    m_ref[...] = (jax.nn.gelu(c) * uv_ref[...]).astype(BF16)


def _ffn_dec(u, cw, cb, conv0):
    B = u.shape[0]
    full = lambda shape: pl.BlockSpec(shape, lambda i: (0,) * len(shape))
    return pl.pallas_call(
        _ffn_dec_body,
        grid=(1,),
        in_specs=[pl.BlockSpec((B, D_FF), lambda i: (0, 0)), pl.BlockSpec((B, D_FF), lambda i: (0, 1)),
                  full((B, D_FF)), full((B, D_FF)), full((CONV_F, D_FF)), full((1, D_FF))],
        out_specs=full((B, D_FF)),
        out_shape=jax.ShapeDtypeStruct((B, D_FF), BF16),
        compiler_params=_cparams(("arbitrary",)),
        name="ffn_dec",
    )(u, u, conv0[:, 0], conv0[:, 1], cw, cb)


def _ffn_fused_body(x_ref, g_ref, wup_ref, cw_ref, cb_ref, wdn_ref, tail0_ref, o_ref, cout_ref,
                    xn_ref, m_ref, tail_ref, *, tm, tc):
    @pl.when(pl.program_id(1) == 0)
    def _():
        tail_ref[...] = tail0_ref[0]

    x = x_ref[...]
    xn_ref[...] = _rms(x, g_ref[...]).astype(BF16)
    row = lax.broadcasted_iota(jnp.int32, (tm, 1), 0)
    for j in range(D_FF // tc):
        cols = slice(j * tc, (j + 1) * tc)
        ug = _dot(xn_ref[...], wup_ref[:, cols])
        uv = _dot(xn_ref[...], wup_ref[:, D_FF + j * tc:D_FF + (j + 1) * tc])
        t1 = tail_ref[TAIL - 1:TAIL, cols]
        t2 = tail_ref[TAIL - 2:TAIL - 1, cols]
        s1 = jnp.where(row == 0, t1, pltpu.roll(ug, 1, 0))
        s2 = jnp.where(row == 0, t2, jnp.where(row == 1, t1, pltpu.roll(ug, 2, 0)))
        c = cb_ref[:, cols] + cw_ref[0:1, cols] * s2 + cw_ref[1:2, cols] * s1 + cw_ref[2:3, cols] * ug
        m_ref[:, cols] = (jax.nn.gelu(c) * uv).astype(BF16)
        tail_ref[:, cols] = ug[tm - TAIL:tm]
        cout_ref[0, :, cols] = ug[tm - (CONV_F - 1):tm]
    o_ref[...] = x + _dot(m_ref[...], wdn_ref[...])


def _ffn_fused(x, g, w_up, cw, cb, w_down, conv0, *, B, T, tm, tc, shared):
    NT = T // tm
    smap = (lambda b, t: (0, 0, 0)) if shared else (lambda b, t: (b, 0, 0))
    tile = pl.BlockSpec((tm, D_MODEL), lambda b, t: (b * NT + t, 0))
    const = lambda a: pl.BlockSpec(a.shape, lambda b, t: (0, 0), pipeline_mode=pl.Buffered(1))
    return pl.pallas_call(
        functools.partial(_ffn_fused_body, tm=tm, tc=tc),
        grid=(B, NT),
        in_specs=[tile, pl.BlockSpec((1, D_MODEL), lambda b, t: (0, 0)), const(w_up),
                  pl.BlockSpec((CONV_F, D_FF), lambda b, t: (0, 0)), pl.BlockSpec((1, D_FF), lambda b, t: (0, 0)),
                  const(w_down), pl.BlockSpec((1, TAIL, D_FF), smap)],
        out_specs=[tile, pl.BlockSpec((1, CONV_F - 1, D_FF), lambda b, t: (b, 0, 0))],
        out_shape=[jax.ShapeDtypeStruct((B * T, D_MODEL), F32),
                   jax.ShapeDtypeStruct((B, CONV_F - 1, D_FF), F32)],
        scratch_shapes=[pltpu.VMEM((tm, D_MODEL), BF16), pltpu.VMEM((tm, D_FF), BF16),
                        pltpu.VMEM((TAIL, D_FF), F32)],
        compiler_params=_cparams(("arbitrary", "arbitrary")),
        name="ffn_fused",
    )(x, g.reshape(1, -1), w_up, cw, cb, w_down, _tail_rows(conv0))


_PREP_W = ("mix", "w_r", "w_k", "w_v", "w0", "w1", "w2", "a0", "a1", "a2", "g1", "g2")
_PREP_VP = ("v0", "v1", "v2")


def _rwkv_prep_body(*refs, has_vp, carry):
    it = iter(refs)
    x_ref, gn_ref = next(it), next(it)
    p = {n: next(it) for n in _PREP_W}
    if has_vp:
        p.update({n: next(it) for n in _PREP_VP})
        vf_ref = next(it)
    prev_ref = next(it)
    r_ref, lw_ref, k_ref, v_ref, as_ref, g_ref, hn_ref = (next(it) for _ in range(7))
    h = _rms(x_ref[...], gn_ref[...])
    if carry:
        carry_ref = next(it)

        @pl.when(pl.program_id(1) == 0)
        def _():
            carry_ref[...] = prev_ref[0]

        row = lax.broadcasted_iota(jnp.int32, (h.shape[0], 1), 0)
        hprev = jnp.where(row == 0, carry_ref[...], pltpu.roll(h, 1, 0))
        carry_ref[...] = h[h.shape[0] - 1:]
        hn_ref[0] = h[h.shape[0] - 1:]
    else:
        hprev = prev_ref[...]
        hn_ref[...] = h
    xx = hprev - h
    mix = p["mix"][...]
    xs = [(h + xx * mix[n:n + 1]).astype(BF16) for n in range(6)]
    xr, xw, xk, xv, xa, xg = xs
    r_ref[...] = _dot(xr, p["w_r"][...])
    k_ref[...] = _dot(xk, p["w_k"][...])
    v = _dot(xv, p["w_v"][...])
    wl = p["w0"][...] + _dot(jnp.tanh(_dot(xw, p["w1"][...])).astype(BF16), p["w2"][...])
    lw_ref[...] = -jnp.exp(_log_sigmoid(wl) - 0.5)
    if has_vp:
        gate = jax.nn.sigmoid(p["v0"][...] + _dot(_dot(xv, p["v1"][...]).astype(BF16), p["v2"][...]))
        v = v + (vf_ref[...] - v) * gate
    v_ref[...] = v
    as_ref[...] = jax.nn.sigmoid(p["a0"][...] + _dot(_dot(xa, p["a1"][...]).astype(BF16), p["a2"][...]))
    g_ref[...] = _dot(jax.nn.sigmoid(_dot(xg, p["g1"][...])).astype(BF16), p["g2"][...])


def _rwkv_prep(x, gn, p, vp, v_first, prev, *, B, T, tm, carry, shared=False):
    R = x.shape[0]
    has_vp = vp is not None
    if carry:
        NT = T // tm
        grid = (B, NT)
        rowmap = lambda b, t: (b * NT + t, 0)
        cmap2 = lambda b, t: (0, 0)
        pmap = (lambda b, t: (0, 0, 0)) if shared else (lambda b, t: (b, 0, 0))
        prev_spec = pl.BlockSpec((1, 1, D_MODEL), pmap)
        hn_spec = pl.BlockSpec((1, 1, D_MODEL), lambda b, t: (b, 0, 0))
        hn_shape = jax.ShapeDtypeStruct((B, 1, D_MODEL), F32)
        sem = ("arbitrary", "arbitrary")
    else:
        grid = (R // tm,)
        rowmap = lambda i: (i, 0)
        cmap2 = lambda i: (0, 0)
        prev_spec = pl.BlockSpec((tm, D_MODEL), rowmap)
        hn_spec = pl.BlockSpec((tm, D_MODEL), rowmap)
        hn_shape = jax.ShapeDtypeStruct((R, D_MODEL), F32)
        sem = ("arbitrary",)
    tile = pl.BlockSpec((tm, D_MODEL), rowmap)
    ins = [x, gn.reshape(1, -1)]
    specs = [tile, pl.BlockSpec((1, D_MODEL), cmap2)]
    names = _PREP_W + (_PREP_VP if has_vp else ())
    src = dict(p)
    if has_vp:
        src.update(vp)
    for n in names:
        ins.append(src[n])
        specs.append(pl.BlockSpec(src[n].shape, cmap2))
    if has_vp:
        ins.append(v_first)
        specs.append(tile)
    ins.append(prev)
    specs.append(prev_spec)
    outs = pl.pallas_call(
        functools.partial(_rwkv_prep_body, has_vp=has_vp, carry=carry),
        grid=grid,
        in_specs=specs,
        out_specs=[tile] * 6 + [hn_spec],
        out_shape=[jax.ShapeDtypeStruct((R, D_MODEL), F32)] * 6 + [hn_shape],
        scratch_shapes=[pltpu.VMEM((1, D_MODEL), F32)] if carry else [],
        compiler_params=_cparams(sem),
        name="rwkv_prep",
    )(*ins)
    return outs


def _rwkv_chunk_body(r_ref, lw_ref, k_ref, v_ref, as_ref, g_ref, kk_ref, ka_ref, rk_ref, gg_ref, gb_ref, s0_ref,
                     o_ref, s_ref, *, Tc, C, n_pair, unroll):
    @pl.when(pl.program_id(2) == 0)
    def _():
        s_ref[...] = s0_ref[...]

    C2 = 2 * C
    lane = lax.broadcasted_iota(jnp.int32, (1, LANES), 1)
    m0 = (lane < HS_C).astype(F32)
    m1 = 1.0 - m0
    li = lax.broadcasted_iota(jnp.int32, (LANES, LANES), 0)
    lj = lax.broadcasted_iota(jnp.int32, (LANES, LANES), 1)
    same_head = (li >= HS_C) == (lj >= HS_C)
    bd_ones = same_head.astype(F32).astype(BF16)
    bd_avg = (same_head.astype(F32) * (1.0 / HS_C)).astype(BF16)
    ti = lax.broadcasted_iota(jnp.int32, (C, C), 0)
    si = lax.broadcasted_iota(jnp.int32, (C, C), 1)
    tri = (ti >= si).astype(F32).astype(BF16)
    tri2 = jnp.concatenate([tri, tri], axis=1)
    r2 = lax.broadcasted_iota(jnp.int32, (C2, C2), 0)
    c2 = lax.broadcasted_iota(jnp.int32, (C2, C2), 1)
    same = (r2 >= C) == (c2 >= C)
    tt, ss = r2 & (C - 1), c2 & (C - 1)
    m_strict = same & (tt > ss)
    m_incl = same & (tt >= ss)
    same_sub = (tt & -RWKV_SUB) == (ss & -RWKV_SUB)
    m_sub = m_strict & same_sub
    m_off = m_strict & jnp.logical_not(same_sub)
    eye = (r2 == c2).astype(F32)
    nb = C // RWKV_SUB
    stack = lambda x: jnp.concatenate([x * m0, x * m1], axis=0)
    dup = lambda x: jnp.concatenate([x, x], axis=0)

    def sub_chunk(ci, carry):
        row_l = [pl.ds(pl.multiple_of((ci * unroll + u) * C, C), C) for u in range(unroll)]
        rows = [rw for rw in row_l for _ in range(n_pair)]
        lanes = [slice(p * LANES, (p + 1) * LANES) for p in range(n_pair)] * unroll
        each = lambda f, *cols: [f(*xs) for xs in zip(*cols)]

        def load(rw, ls):
            r, lw, k, v = r_ref[rw, ls], lw_ref[rw, ls], k_ref[rw, ls], v_ref[rw, ls]
            asig = as_ref[rw, ls]
            kk = k * kk_ref[:, ls]
            k2 = k * (1.0 + (asig - 1.0) * ka_ref[:, ls])
            return r, lw, k2, v, asig, kk

        r, lw, k2, v, asig, kk = zip(*each(load, rows, lanes))
        ssq = each(lambda x: _dot(_bf(x * x), bd_ones), kk)
        cs = each(lambda x: _dot(tri2, jnp.concatenate(_split(x), axis=0)), lw)
        kk = each(lambda x, q: x * lax.rsqrt(jnp.maximum(q, 1e-24)), kk, ssq)
        b = each(lambda x, q: x * q, kk, asig)
        cend = [x[C - 1:C] for x in cs]
        e_neg = each(lambda x: jnp.exp(-x), cs)
        e_end = each(lambda x, y: jnp.exp(y - x), cs, cend)
        As = each(lambda x, c, l: _split(stack(-x * jnp.exp(c - l))), kk, cs, lw)
        Bd = each(lambda x, e: _split(dup(x * e)), b, e_neg)
        Rs = each(lambda x, c: _bf(stack(x * jnp.exp(c))), r, cs)
        Vs = each(lambda x: _bf(stack(x)), v)
        Kd = each(lambda x, e: _bf(dup(x * e)), k2, e_neg)
        G = each(_dot3_wide_nt, As, Bd)
        Aak = each(lambda x, y: _bf(jnp.where(m_strict, _dot_nt(x[0], y), 0.0)), As, Kd)
        Arb = each(lambda x, y: _bf(jnp.where(m_incl, _dot_nt(x, y[0]), 0.0)), Rs, Bd)
        Ark = each(lambda x, y: _bf(jnp.where(m_incl, _dot_nt(x, y), 0.0)), Rs, Kd)
        Nd = each(lambda x: jnp.where(m_sub, x, 0.0), G)
        P = each(lambda x: eye + x, Nd)
        Qs = each(_split, Nd)
        for _ in range(3):
            Qs = each(lambda q: _split(_dot3_wide(q, q)), Qs)
            P = each(lambda x, q: x + _dot3_wide(_split(x), q), P, Qs)
        if nb > 1:
            Pb = each(_bf, P)
            M = each(lambda x, y: _dot(x, _bf(jnp.where(m_off, y, 0.0))), Pb, G)
            Tm = each(lambda x: eye + x, M)
            for _ in range(int(np.ceil(np.log2(nb))) - 1):
                M = each(lambda x: _dot(_bf(x), _bf(x)), M)
                Tm = each(lambda x, y: x + _dot(_bf(x), _bf(y)), Tm, M)
            Tinv = each(lambda x, y: _bf(_dot(_bf(x), y)), Tm, Pb)
        else:
            Tinv = each(_bf, P)
        bE = each(lambda x, e: _bf(stack(x * e)), b, e_end)
        kE = each(lambda x, e: _bf(stack(x * e)), k2, e_end)
        gC = each(jnp.exp, cend)
        Ys = []
        for u in range(unroll):
            sl = slice(u * n_pair, (u + 1) * n_pair)
            S = [s_ref[0, p] for p in range(n_pair)]
            Sb = each(_bf, S)
            X = each(lambda a_, s_, k_, v_: _bf(_dot_nt(a_[0], s_) + _dot(k_, v_)), As[sl], Sb, Aak[sl], Vs[sl])
            Us = each(lambda t_, x_: _bf(_dot(t_, x_)), Tinv[sl], X)
            Ys += each(lambda r_, s_, b_, u_, k_, v_: _dot_nt(r_, s_) + _dot(b_, u_) + _dot(k_, v_),
                       Rs[sl], Sb, Arb[sl], Us, Ark[sl], Vs[sl])
            Sn = each(lambda s_, g_, u_, b_, v_, k_: s_ * g_ + _dot_tn(u_, b_) + _dot_tn(v_, k_),
                      S, gC[sl], Us, bE[sl], Vs[sl], kE[sl])
            for p in range(n_pair):
                s_ref[0, p] = Sn[p]
        y = [x[:C] + x[C:] for x in Ys]
        mu = each(lambda x: _dot(_bf(x), bd_avg), y)
        d = each(lambda x, m: x - m, y, mu)
        var = each(lambda x: _dot(_bf(x * x), bd_avg), d)
        bonus = each(lambda r_, k_, ls: _dot(_bf(r_ * k_ * rk_ref[:, ls]), bd_ones), r, k2, lanes)
        for i, (rw, ls) in enumerate(zip(rows, lanes)):
            on = d[i] * lax.rsqrt(var[i] + GN_EPS_C) * gg_ref[:, ls] + gb_ref[:, ls]
            o_ref[rw, ls] = ((on + bonus[i] * v[i]) * g_ref[rw, ls]).astype(BF16)
        return carry

    lax.fori_loop(0, Tc // (C * unroll), sub_chunk, 0)


def _rwkv_chunk(arrs, p, s0, *, B, T, Tc, C, ppb, shared, unroll=1):
    NC = T // Tc
    NP = N_PAIR // ppb
    Wb = ppb * LANES
    tile = pl.BlockSpec((Tc, Wb), lambda b, q, c: (b * NC + c, q))
    vec = pl.BlockSpec((1, Wb), lambda b, q, c: (0, q))
    smap = (lambda b, q, c: (0, q, 0, 0)) if shared else (lambda b, q, c: (b, q, 0, 0))
    return pl.pallas_call(
        functools.partial(_rwkv_chunk_body, Tc=Tc, C=C, n_pair=ppb, unroll=unroll),
        grid=(B, NP, NC),
        in_specs=[tile] * 6 + [vec] * 5 + [pl.BlockSpec((1, ppb, LANES, LANES), smap)],
        out_specs=[tile, pl.BlockSpec((1, ppb, LANES, LANES), lambda b, q, c: (b, q, 0, 0))],
        out_shape=[jax.ShapeDtypeStruct((B * T, D_MODEL), BF16),
                   jax.ShapeDtypeStruct((B, N_PAIR, LANES, LANES), F32)],
        compiler_params=_cparams(("arbitrary", "arbitrary", "arbitrary")),
        name="rwkv_chunk",
    )(*arrs, p["k_k"], p["k_a"], p["r_k"], p["gn_g"], p["gn_b"], s0)


def _rwkv_dec_body(r_ref, lw_ref, k_ref, v_ref, as_ref, g_ref, kk_ref, ka_ref, rk_ref, gg_ref, gb_ref, s0_ref,
                   o_ref, s_ref, r_scr, w_scr, k2_scr, v_scr, a_scr, b_scr, y_scr, *, nb):
    n_head = 2 * N_PAIR
    ri = lax.broadcasted_iota(jnp.int32, (HS_C, HS_C), 0)
    ci = lax.broadcasted_iota(jnp.int32, (HS_C, HS_C), 1)
    eye = (ri == ci).astype(F32)
    for h in range(n_head):
        ls = slice(h * HS_C, (h + 1) * HS_C)
        k, asig = k_ref[:, ls], as_ref[:, ls]
        kk = k * kk_ref[:, ls]
        kk = kk * lax.rsqrt(jnp.maximum(jnp.sum(kk * kk, axis=1, keepdims=True), 1e-24))
        a_scr[h] = -kk
        b_scr[h] = kk * asig
        k2_scr[h] = k * (1.0 + (asig - 1.0) * ka_ref[:, ls])
        r_scr[h] = r_ref[:, ls]
        v_scr[h] = v_ref[:, ls]
        w_scr[h] = jnp.exp(lw_ref[:, ls])

    def per_sample(s, carry):
        row = pl.ds(s, 1)
        heads = range(n_head)
        S = [s0_ref[s, h] for h in heads]
        sa = [jnp.sum(S[h] * a_scr[h, row, :], axis=1, keepdims=True) for h in heads]
        v_col = [jnp.sum(eye * v_scr[h, row, :], axis=1, keepdims=True) for h in heads]
        Sn = [S[h] * w_scr[h, row, :] + sa[h] * b_scr[h, row, :] + v_col[h] * k2_scr[h, row, :] for h in heads]
        for h in heads:
            s_ref[s, h] = Sn[h]
        y_col = [jnp.sum(Sn[h] * r_scr[h, row, :], axis=1, keepdims=True) for h in heads]
        for h in heads:
            y_scr[h, row, :] = jnp.sum(eye * y_col[h], axis=0, keepdims=True)
        return carry

    lax.fori_loop(0, nb, per_sample, 0)
    for h in range(n_head):
        ls = slice(h * HS_C, (h + 1) * HS_C)
        y = y_scr[h]
        d = y - jnp.mean(y, axis=1, keepdims=True)
        var = jnp.mean(d * d, axis=1, keepdims=True)
        on = d * lax.rsqrt(var + GN_EPS_C) * gg_ref[:, ls] + gb_ref[:, ls]
        bonus = jnp.sum(r_scr[h] * k2_scr[h] * rk_ref[:, ls], axis=1, keepdims=True) * v_scr[h]
        o_ref[:, ls] = (on + bonus) * g_ref[:, ls]


def _rwkv_dec(arrs, p, s0, *, nb):
    B = s0.shape[0]
    n_head = 2 * N_PAIR
    tile = pl.BlockSpec((nb, D_MODEL), lambda i: (i, 0))
    vec = pl.BlockSpec((1, D_MODEL), lambda i: (0, 0))
    sspec = pl.BlockSpec((nb, n_head, HS_C, HS_C), lambda i: (i, 0, 0, 0))
    return pl.pallas_call(
        functools.partial(_rwkv_dec_body, nb=nb),
        grid=(B // nb,),
        in_specs=[tile] * 6 + [vec] * 5 + [sspec],
        out_specs=[tile, sspec],
        out_shape=[jax.ShapeDtypeStruct((B, D_MODEL), F32), jax.ShapeDtypeStruct(s0.shape, F32)],
        scratch_shapes=[pltpu.VMEM((n_head, nb, HS_C), F32)] * 7,
        compiler_params=_cparams(("arbitrary",)),
        name="rwkv_dec",
    )(*arrs, p["k_k"], p["k_a"], p["r_k"], p["gn_g"], p["gn_b"], s0)


def _from_blockdiag(s):
    B = s.shape[0]
    return jnp.stack([s[:, :, :HS_C, :HS_C], s[:, :, HS_C:, HS_C:]], axis=2).reshape(B, 2 * N_PAIR, HS_C, HS_C)


MAIN_TM = 2048
MAIN_TN = 512
RET_C = 128
LRU_TC = 256
FFN_TC = 256
FFN_TM = 512
FFN_TCOL = 256
PREP_TM = 256
RWKV_TC = 256
RWKV_C = 64
RWKV_PPB = 8
RWKV_UNROLL = 2
RWKV_DEC_NB = 8
RET_DEC_NB = 8


def kernel(x_prompt, x_sample, state_ret, state_lru, state_lru_conv, state_rwkv, state_shift, state_ffn_conv,
           meta_tokens, norm_mix_g, norm_ffn_g, norm_final_g,
           ev_w_in, ev_ret_gn_g, ev_lru_conv_w, ev_lru_conv_b, ev_lru_wa, ev_lru_ba, ev_lru_wx, ev_lru_bx,
           ev_lru_lambda, ev_w_out,
           od_mix, od_w_r, od_w_k, od_w_v, od_w0, od_w1, od_w2, od_a0, od_a1, od_a2, od_v0, od_v1, od_v2,
           od_g1, od_g2, od_k_k, od_k_a, od_r_k, od_gn_g, od_gn_b, od_w_o,
           ff_w_up, ff_conv_w, ff_conv_b, ff_w_down):
    BP, TP, _ = x_prompt.shape
    BS = x_sample.shape[0]
    NS = BS + N_META
    bf = lambda a: a.astype(BF16)
    row = lambda a: a.reshape(1, -1)

    def lora_in(w):
        return bf(jnp.pad(w, ((0, 0), (0, LORA_PAD - w.shape[1]))))

    def lora_out(w):
        return bf(jnp.pad(w, ((0, LORA_PAD - w.shape[0]), (0, 0))))

    xm = x_prompt.reshape(BP * TP, D_MODEL)
    xs = jnp.concatenate([x_sample.reshape(BS, D_MODEL), meta_tokens.astype(x_prompt.dtype)], axis=0)

    cos_m, sin_m = _rope_tables(N_META + jnp.arange(TP, dtype=jnp.int32))
    cos_t, sin_t = _rope_tables(jnp.arange(N_META, dtype=jnp.int32))
    cos_s, sin_s = _rope_tables(PAST_LEN + jnp.arange(PAD_T, dtype=jnp.int32))

    out = {k: [] for k in ("ret_p", "lru_p", "lconv_p", "rwkv_p", "shift_p", "ffn_p",
                           "ret_s", "lru_s", "lconv_s", "rwkv_s", "shift_s", "ffn_s")}
    v_first_m = v_first_s = None
    for li in range(4):
        j = li // 2
        if li % 2 == 0:
            w_in = bf(ev_w_in[j])
            w_out_a, w_out_b = bf(ev_w_out[j][:VG_W]), bf(ev_w_out[j][VG_W:])
            gn = ev_ret_gn_g[j]
            lp = dict(conv_w=ev_lru_conv_w[j], conv_b=row(ev_lru_conv_b[j]), wa=bf(ev_lru_wa[j]),
                      ba=row(ev_lru_ba[j]), wx=bf(ev_lru_wx[j]), bx=row(ev_lru_bx[j]), lam=row(ev_lru_lambda[j]))
            zs = _mm([xs], [w_in], g=norm_mix_g[li], tm=NS, tn=MAIN_TN)
            z_smp, z_meta = zs[:BS], zs[BS:]
            ya_t, ret_t = _retention(z_meta, cos_t, sin_t, jnp.zeros((1, H_A, DK_A, DV_A), F32), gn,
                                     B=1, T=N_META, C=N_META, c_true=N_META, shared=False)
            yb_t, lru_t, lconv_t = _lru_seq(z_meta, lp, jnp.zeros((1, 1, W_B), F32),
                                            jnp.zeros((1, CONV_B - 1, W_B), F32), B=1, T=N_META, Tc=N_META,
                                            shared=False)
            ya_s, ret_s = _retention(z_smp, cos_s, sin_s, state_ret[j], gn, B=BS, T=1, C=PAD_T, c_true=1,
                                     shared=False, bb=RET_DEC_NB, single=True)
            yb_s, lru_s = _lru_dec(z_smp, lp, state_lru[j], state_lru_conv[j])
            lconv_s = jnp.concatenate([state_lru_conv[j][:, 1:], z_smp[:, None, 3 * W_B:4 * W_B]], axis=1)
            ya = jnp.concatenate([ya_s, ya_t.astype(F32)], axis=0)
            yb = jnp.concatenate([yb_s, yb_t], axis=0)
            xs = _mm([ya, yb], [w_out_a, w_out_b], res=xs, tm=NS, tn=MAIN_TN)
            zm = _mm([xm], [w_in], g=norm_mix_g[li], tm=MAIN_TM, tn=MAIN_TN)
            ya_m, ret_m = _retention(zm, cos_m, sin_m, ret_t, gn, B=BP, T=TP, C=RET_C, c_true=RET_C, shared=True)
            yb_m, lru_m, lconv_m = _lru_seq(zm, lp, lru_t, lconv_t, B=BP, T=TP, Tc=LRU_TC, shared=True)
            xm = _mm([ya_m, yb_m], [w_out_a, w_out_b], res=xm, tm=MAIN_TM, tn=MAIN_TN)
            out["ret_p"].append(ret_m)
            out["lru_p"].append(lru_m[:, 0])
            out["lconv_p"].append(lconv_m)
            out["ret_s"].append(ret_s)
            out["lru_s"].append(lru_s)
            out["lconv_s"].append(lconv_s)
        else:
            p = dict(mix=od_mix[j], w_r=bf(od_w_r[j]), w_k=bf(od_w_k[j]), w_v=bf(od_w_v[j]), w0=row(od_w0[j]),
                     w1=lora_in(od_w1[j]), w2=lora_out(od_w2[j]), a0=row(od_a0[j]), a1=lora_in(od_a1[j]),
                     a2=lora_out(od_a2[j]), g1=lora_in(od_g1[j]), g2=lora_out(od_g2[j]),
                     k_k=row(od_k_k[j]), k_a=row(od_k_a[j]), r_k=row(od_r_k[j]), gn_g=row(od_gn_g[j]),
                     gn_b=row(od_gn_b[j]))
            vp = None
            if j > 0:
                vp = dict(v0=row(od_v0[j - 1]), v1=lora_in(od_v1[j - 1]), v2=lora_out(od_v2[j - 1]))
            w_o = bf(od_w_o[j])
            gmix = norm_mix_g[li]
            pre_t = _rwkv_prep(xs[BS:], gmix, p, vp, None if vp is None else v_first_s[BS:],
                               jnp.zeros((1, 1, D_MODEL), F32), B=1, T=N_META, tm=N_META, carry=True)
            o_t, rw_t = _rwkv_chunk(pre_t[:6], p, jnp.zeros((1, N_PAIR, LANES, LANES), F32),
                                    B=1, T=N_META, Tc=N_META, C=N_META, ppb=RWKV_PPB, shared=False)
            shift_t = pre_t[6]
            pre_s = _rwkv_prep(xs[:BS], gmix, p, vp, None if vp is None else v_first_s[:BS],
                               state_shift[j], B=BS, T=1, tm=BS, carry=False)
            o_s, rw_s = _rwkv_dec(pre_s[:6], p, state_rwkv[j], nb=RWKV_DEC_NB)
            if vp is None:
                v_first_s = jnp.concatenate([pre_s[3], pre_t[3]], axis=0)
            o_small = jnp.concatenate([o_s, o_t.astype(F32)], axis=0)
            xs = _mm([o_small], [w_o], res=xs, tm=NS, tn=MAIN_TN)
            pre_m = _rwkv_prep(xm, gmix, p, vp, v_first_m, shift_t, B=BP, T=TP, tm=PREP_TM, carry=True, shared=True)
            if vp is None:
                v_first_m = pre_m[3]
            o_m, rw_m = _rwkv_chunk(pre_m[:6], p, rw_t, B=BP, T=TP, Tc=RWKV_TC, C=RWKV_C, ppb=RWKV_PPB, shared=True,
                                    unroll=RWKV_UNROLL)
            xm = _mm([o_m], [w_o], res=xm, tm=MAIN_TM, tn=MAIN_TN)
            out["rwkv_p"].append(_from_blockdiag(rw_m))
            out["shift_p"].append(pre_m[6][:, 0])
            out["rwkv_s"].append(rw_s)
            out["shift_s"].append(pre_s[6])
        w_up, w_down = bf(ff_w_up[li]), bf(ff_w_down[li])
        cw, cb = ff_conv_w[li], row(ff_conv_b[li])
        us = _mm([xs], [w_up], g=norm_ffn_g[li], tm=NS, tn=MAIN_TN)
        m_t, ffn_t = _ffn_seq(us[BS:], cw, cb, jnp.zeros((1, CONV_F - 1, D_FF), F32), B=1, T=N_META, Tc=N_META,
                              shared=False)
        m_s = _ffn_dec(us[:BS], cw, cb, state_ffn_conv[li])
        ffn_s = jnp.concatenate([state_ffn_conv[li][:, 1:], us[:BS, None, :D_FF]], axis=1)
        xs = _mm([jnp.concatenate([m_s, m_t], axis=0)], [w_down], res=xs, tm=NS, tn=MAIN_TN)
        xm, ffn_m = _ffn_fused(xm, norm_ffn_g[li], w_up, cw, cb, w_down, ffn_t, B=BP, T=TP, tm=FFN_TM, tc=FFN_TCOL,
                               shared=True)
        out["ffn_p"].append(ffn_m)
        out["ffn_s"].append(ffn_s)

    y_prompt = _final_norm(xm, norm_final_g, MAIN_TM).reshape(BP, TP, D_MODEL)
    y_sample = _final_norm(xs, norm_final_g, NS)[:BS].reshape(BS, 1, D_MODEL)
    st = lambda k: jnp.stack(out[k])
    return (y_prompt, y_sample,
            st("ret_p"), st("lru_p"), st("lconv_p"), st("rwkv_p"), st("shift_p"), st("ffn_p"),
            st("ret_s"), st("lru_s"), st("lconv_s"), st("rwkv_s"), st("shift_s"), st("ffn_s"))
```

```python
import functools

import numpy as np
import jax
import jax.numpy as jnp
from jax import lax
from jax.experimental import pallas as pl
from jax.experimental.pallas import tpu as pltpu

F32 = jnp.float32
BF16 = jnp.bfloat16
HI = lax.Precision.HIGHEST

D_MODEL = 1024
N_META = 16
PAST_LEN = 16384
H_A, DK_A, DV_A = 4, 128, 256
W_B, NB_B, BS_B, CONV_B = 1024, 8, 128, 4
LRU_C = 8.0
HS_C = 64
N_PAIR = D_MODEL // (2 * HS_C)
LORA_PAD = 128
GN_EPS_C = 64e-5
D_FF, CONV_F = 2816, 3
EPS = 1e-6
ROPE_BASE = 10000.0
QK_W, VG_W = H_A * DK_A, H_A * DV_A

LANES = 128
TAIL = 8
VMEM_LIMIT = 52 * 1024 * 1024
PAD_T = 16
RWKV_SUB = 16


def _cparams(sem):
    return pltpu.CompilerParams(dimension_semantics=sem, vmem_limit_bytes=VMEM_LIMIT)


def _dot(a, b, prec=None):
    return jnp.dot(a, b, preferred_element_type=F32, precision=prec)


def _dot_nt(a, b, prec=None):
    return lax.dot_general(a, b, (((1,), (1,)), ((), ())), preferred_element_type=F32, precision=prec)


def _dot_tn(a, b, prec=None):
    return lax.dot_general(a, b, (((0,), (0,)), ((), ())), preferred_element_type=F32, precision=prec)


def _bf(x):
    return x.astype(BF16)


def _split(x):
    hi = x.astype(BF16)
    return hi, (x - hi.astype(F32)).astype(BF16)


def _dot3(a, b, dot):
    return dot(a[0], b[0]) + dot(a[0], b[1]) + dot(a[1], b[0])


def _dot3_wide(a, b):
    n = a[0].shape[1]
    lhs = jnp.concatenate([a[0], a[1]], axis=1)
    rhs = jnp.concatenate([jnp.concatenate([b[0], b[1]], axis=1),
                           jnp.concatenate([b[0], jnp.zeros_like(b[0])], axis=1)], axis=0)
    out = _dot(lhs, rhs)
    return out[:, :n] + out[:, n:]


def _dot3_wide_nt(a, b):
    n = b[0].shape[0]
    lhs = jnp.concatenate([a[0], a[1]], axis=1)
    rhs = jnp.concatenate([jnp.concatenate([b[0], b[0]], axis=1),
                           jnp.concatenate([b[1], jnp.zeros_like(b[1])], axis=1)], axis=0)
    out = _dot_nt(lhs, rhs)
    return out[:, :n] + out[:, n:]


def _rms(x, g):
    return x * lax.rsqrt(jnp.mean(x * x, axis=-1, keepdims=True) + EPS) * g


def _log_sigmoid(x):
    return jnp.minimum(x, 0.0) - jnp.log(1.0 + jnp.exp(-jnp.abs(x)))


def _mm_body(*refs, n_in, norm, res):
    it = iter(refs)
    x_refs = [next(it) for _ in range(n_in)]
    g_ref = next(it) if norm else None
    w_refs = [next(it) for _ in range(n_in)]
    r_ref = next(it) if res else None
    o_ref = next(it)
    if norm:
        xn_ref = next(it)

        @pl.when(pl.program_id(1) == 0)
        def _():
            xn_ref[...] = _rms(x_refs[0][...], g_ref[...]).astype(BF16)

        acc = _dot(xn_ref[...], w_refs[0][...])
    else:
        acc = _dot(x_refs[0][...].astype(BF16), w_refs[0][...])
        for x_ref, w_ref in zip(x_refs[1:], w_refs[1:]):
            acc = acc + _dot(x_ref[...].astype(BF16), w_ref[...])
    if res:
        acc = acc + r_ref[...]
    o_ref[...] = acc.astype(o_ref.dtype)


def _mm(xs, ws, *, g=None, res=None, tm, tn, out_dtype=F32):
    R = xs[0].shape[0]
    N = ws[0].shape[1]
    norm = g is not None
    ins, specs = [], []
    for x in xs:
        ins.append(x)
        specs.append(pl.BlockSpec((tm, x.shape[1]), lambda i, j: (i, 0)))
    if norm:
        ins.append(g.reshape(1, -1))
        specs.append(pl.BlockSpec((1, g.shape[-1]), lambda i, j: (0, 0)))
    for w in ws:
        ins.append(w)
        specs.append(pl.BlockSpec((w.shape[0], tn), lambda i, j: (0, j)))
    if res is not None:
        ins.append(res)
        specs.append(pl.BlockSpec((tm, tn), lambda i, j: (i, j)))
    scratch = [pltpu.VMEM((tm, xs[0].shape[1]), BF16)] if norm else []
    return pl.pallas_call(
        functools.partial(_mm_body, n_in=len(xs), norm=norm, res=res is not None),
        grid=(R // tm, N // tn),
        in_specs=specs,
        out_specs=pl.BlockSpec((tm, tn), lambda i, j: (i, j)),
        out_shape=jax.ShapeDtypeStruct((R, N), out_dtype),
        scratch_shapes=scratch,
        compiler_params=_cparams(("arbitrary", "arbitrary")),
        name="mm",
    )(*ins)


def _final_norm_body(x_ref, g_ref, o_ref):
    o_ref[...] = _rms(x_ref[...], g_ref[...])


def _final_norm(x, g, tm):
    R = x.shape[0]
    return pl.pallas_call(
        _final_norm_body,
        grid=(R // tm,),
        in_specs=[pl.BlockSpec((tm, D_MODEL), lambda i: (i, 0)), pl.BlockSpec((1, D_MODEL), lambda i: (0, 0))],
        out_specs=pl.BlockSpec((tm, D_MODEL), lambda i: (i, 0)),
        out_shape=jax.ShapeDtypeStruct((R, D_MODEL), F32),
        compiler_params=_cparams(("arbitrary",)),
        name="final_norm",
    )(x, g.reshape(1, -1))


def _ret_body(q_ref, k_ref, v_ref, ga_ref, cos_ref, sin_ref, s0_ref, gn_ref, y_ref, s_ref, *, C, c_true, bb, single):
    @pl.when(pl.program_id(1) == 0)
    def _():
        s_ref[...] = s0_ref[...]

    cosf, sinf = cos_ref[...], sin_ref[...]
    ti = lax.broadcasted_iota(jnp.int32, (C, C), 0)
    si = lax.broadcasted_iota(jnp.int32, (C, C), 1)
    dif = (ti - si).astype(F32)
    trow = lax.broadcasted_iota(jnp.int32, (C, 1), 0)
    tcol = trow.astype(F32)
    for h in range(H_A):
        lg = float(np.log1p(-(2.0 ** (-5.0 - h))))
        mask = jnp.where(dif >= 0, jnp.exp(lg * jnp.maximum(dif, 0.0)), 0.0)
        dec_in = jnp.exp((tcol + 1.0) * lg)
        dec_k = jnp.exp((c_true - 1.0 - tcol) * lg)
        for s in range(bb):
            rs = slice(s, s + 1) if single else slice(s * C, (s + 1) * C)
            chunk = (lambda x: jnp.where(trow == 0, x, 0.0)) if single else (lambda x: x)
            q = chunk(q_ref[rs, h * DK_A:(h + 1) * DK_A])
            k = chunk(k_ref[rs, h * DK_A:(h + 1) * DK_A])
            q = q * cosf + pltpu.roll(q, DK_A // 2, 1) * sinf
            k = (k * cosf + pltpu.roll(k, DK_A // 2, 1) * sinf) * (DK_A ** -0.5)
            vb = chunk(v_ref[rs, h * DV_A:(h + 1) * DV_A]).astype(BF16)
            qb = q.astype(BF16)
            sc = _dot_nt(qb, k.astype(BF16)) * mask
            S = s_ref[s, h]
            o = _dot(sc.astype(BF16), vb) + _dot(qb, S.astype(BF16)) * dec_in
            s_ref[s, h] = float(np.exp(c_true * lg)) * S + _dot_tn((k * dec_k).astype(BF16), vb)
            mu = jnp.mean(o, axis=-1, keepdims=True)
            d = o - mu
            var = jnp.mean(d * d, axis=-1, keepdims=True)
            on = d * lax.rsqrt(var + EPS) * gn_ref[:, h * DV_A:(h + 1) * DV_A]
            ga = ga_ref[rs, h * DV_A:(h + 1) * DV_A]
            if single:
                on = on[0:1]
            y_ref[rs, h * DV_A:(h + 1) * DV_A] = (ga * jax.nn.sigmoid(ga) * on).astype(y_ref.dtype)


def _drop_ref(body, i):
    def wrapped(*refs, **kw):
        return body(*refs[:i], *refs[i + 1:], **kw)
    return wrapped


def _retention(z, cosf, sinf, s0, gn, *, B, T, C, c_true, shared, bb=1, single=False, layer=None, prev=None):
    NC = 1 if single else T // C
    assert bb == 1 or (NC == 1 and not shared)
    assert not single or (T == 1 and c_true == 1)
    row = lambda b, c: b * NC + c
    R = bb if single else bb * C
    if layer is None:
        smap = (lambda b, c: (0, 0, 0, 0)) if shared else (lambda b, c: (b, 0, 0, 0))
        s_in = pl.BlockSpec((bb, H_A, DK_A, DV_A), smap)
        s_out = pl.BlockSpec((bb, H_A, DK_A, DV_A), lambda b, c: (b, 0, 0, 0))
        s_shape = (B, H_A, DK_A, DV_A)
    else:
        s_in = s_out = pl.BlockSpec((None, bb, H_A, DK_A, DV_A), lambda b, c: (layer, b, 0, 0, 0))
        s_shape = s0.shape
    ins = [z, z, z, z, cosf, sinf, s0, gn.reshape(1, -1)]
    specs = [
        pl.BlockSpec((R, QK_W), lambda b, c: (row(b, c), 0)),
        pl.BlockSpec((R, QK_W), lambda b, c: (row(b, c), 1)),
        pl.BlockSpec((R, VG_W), lambda b, c: (row(b, c), 1)),
        pl.BlockSpec((R, VG_W), lambda b, c: (row(b, c), 2)),
        pl.BlockSpec((C, DK_A), lambda b, c: (c, 0)),
        pl.BlockSpec((C, DK_A), lambda b, c: (c, 0)),
        s_in,
        pl.BlockSpec((1, VG_W), lambda b, c: (0, 0)),
    ]
    body = functools.partial(_ret_body, C=C, c_true=c_true, bb=bb, single=single)
    aliases = {}
    if prev is not None:
        ins.append(prev)
        specs.append(pl.BlockSpec(memory_space=pl.ANY))
        aliases = {len(ins) - 1: 1}
        body = _drop_ref(body, len(ins) - 1)
    return pl.pallas_call(
        body,
        grid=(B // bb, NC),
        in_specs=specs,
        out_specs=[pl.BlockSpec((R, VG_W), lambda b, c: (row(b, c), 0)), s_out],
        out_shape=[
            jax.ShapeDtypeStruct((B * T, VG_W), F32 if single else BF16),
            jax.ShapeDtypeStruct(s_shape, F32),
        ],
        input_output_aliases=aliases,
        compiler_params=_cparams(("arbitrary", "arbitrary")),
        name="retention",
    )(*ins)


def _rope_tables(pos):
    half = DK_A // 2
    inv = ROPE_BASE ** (-jnp.linspace(0.0, 1.0, half, dtype=F32))
    ang = pos.astype(F32)[:, None] * inv[None, :]
    cos, sin = jnp.cos(ang), jnp.sin(ang)
    return jnp.concatenate([cos, cos], axis=-1), jnp.concatenate([-sin, sin], axis=-1)


def _lru_gates(xc, wa, ba, wx, bx, lam):
    xg = xc.astype(BF16)
    r = jax.nn.sigmoid(_dot(xg, wa) + ba)
    i = jax.nn.sigmoid(_dot(xg, wx) + bx)
    log_a = (-LRU_C) * r * (-_log_sigmoid(lam))
    a = jnp.exp(log_a)
    u = jnp.sqrt(1.0 - a * a) * (i * xc)
    return a, u


def _lru_tile(x_of, gb_of, prm, h0_ref, tail0_ref, y_ref, hout_ref, cout_ref, xe_ref, hc_ref, Tc, between=None):
    cw_ref, cb_ref, wa_ref, ba_ref, wx_ref, bx_ref, lam_ref = prm

    @pl.when(pl.program_id(1) == 0)
    def _():
        xe_ref[0:TAIL, :] = tail0_ref[0]
        hc_ref[...] = h0_ref[0]

    sub = lax.broadcasted_iota(jnp.int32, (TAIL, 1), 0)
    for g in range(NB_B):
        cs = slice(g * BS_B, (g + 1) * BS_B)
        x = x_of(g)
        xe_ref[TAIL:TAIL + Tc, cs] = x
        xc = cb_ref[:, cs]
        for j in range(CONV_B - 1):
            off = TAIL - (CONV_B - 1) + j
            xc = xc + cw_ref[j:j + 1, cs] * xe_ref[off:off + Tc, cs]
        xc = xc + cw_ref[CONV_B - 1:CONV_B, cs] * x
        a, u = _lru_gates(xc, wa_ref[g], ba_ref[:, cs], wx_ref[g], bx_ref[:, cs], lam_ref[:, cs])
        h_prev = hc_ref[:, cs]
        tiles = []
        for t in range(Tc // TAIL):
            a8, u8 = a[t * TAIL:(t + 1) * TAIL], u[t * TAIL:(t + 1) * TAIL]
            d = 1
            while d < TAIL:
                valid = sub >= d
                u8 = jnp.where(valid, a8 * pltpu.roll(u8, d, 0) + u8, u8)
                a8 = jnp.where(valid, a8 * pltpu.roll(a8, d, 0), a8)
                d *= 2
            h8 = a8 * h_prev + u8
            h_prev = h8[TAIL - 1:TAIL]
            tiles.append(h8)
        hs = jnp.concatenate(tiles, axis=0)
        y_ref[:, cs] = (hs * jax.nn.gelu(gb_of(g))).astype(BF16)
        hc_ref[:, cs] = hs[Tc - 1:Tc]
        hout_ref[0, :, cs] = hs[Tc - 1:Tc]
        cout_ref[0, :, cs] = xe_ref[Tc + TAIL - (CONV_B - 1):Tc + TAIL, cs]
        xe_ref[0:TAIL, cs] = xe_ref[Tc:Tc + TAIL, cs]
        if between is not None:
            between(g)


def _lru_seq_body(xb_ref, gb_ref, cw_ref, cb_ref, wa_ref, ba_ref, wx_ref, bx_ref, lam_ref, h0_ref, tail0_ref,
                  y_ref, hout_ref, cout_ref, xe_ref, hc_ref, *, Tc):
    blk = lambda ref: (lambda g: ref[:, g * BS_B:(g + 1) * BS_B])
    _lru_tile(blk(xb_ref), blk(gb_ref), (cw_ref, cb_ref, wa_ref, ba_ref, wx_ref, bx_ref, lam_ref),
              h0_ref, tail0_ref, y_ref, hout_ref, cout_ref, xe_ref, hc_ref, Tc)


QKVG_W = 2 * QK_W + 2 * VG_W


def _ein_lru_body(x_ref, g_ref, w_ref, cw_ref, cb_ref, wa_ref, ba_ref, wx_ref, bx_ref, lam_ref, h0_ref, tail0_ref,
                  zq_ref, y_ref, hout_ref, cout_ref, xn_ref, xe_ref, hc_ref, *, Tc, tn):
    xn_ref[...] = _rms(x_ref[...], g_ref[...]).astype(BF16)
    pairs = {}

    def col(start):
        def block(g):
            key = (start, g // 2)
            if key not in pairs:
                lo = start + (g // 2) * 2 * BS_B
                pairs[key] = _dot(xn_ref[...], w_ref[:, lo:lo + 2 * BS_B])
            return pairs[key][:, (g % 2) * BS_B:(g % 2 + 1) * BS_B]
        return block

    def qkvg_chunk(c):
        if c < QKVG_W // tn:
            zq_ref[:, c * tn:(c + 1) * tn] = _dot(xn_ref[...], w_ref[:, c * tn:(c + 1) * tn])

    _lru_tile(col(QKVG_W), col(QKVG_W + W_B), (cw_ref, cb_ref, wa_ref, ba_ref, wx_ref, bx_ref, lam_ref),
              h0_ref, tail0_ref, y_ref, hout_ref, cout_ref, xe_ref, hc_ref, Tc, between=qkvg_chunk)


def _ein_lru(x, g, w_in, p, h0, conv0, *, B, T, Tc, tn, shared):
    NC = T // Tc
    smap = (lambda b, c: (0, 0, 0)) if shared else (lambda b, c: (b, 0, 0))
    row = lambda b, c: (b * NC + c, 0)
    vec = pl.BlockSpec((1, W_B), lambda b, c: (0, 0))
    wspec = pl.BlockSpec((NB_B, BS_B, BS_B), lambda b, c: (0, 0, 0))
    return pl.pallas_call(
        functools.partial(_ein_lru_body, Tc=Tc, tn=tn),
        grid=(B, NC),
        in_specs=[
            pl.BlockSpec((Tc, D_MODEL), row), pl.BlockSpec((1, D_MODEL), lambda b, c: (0, 0)),
            pl.BlockSpec(w_in.shape, lambda b, c: (0, 0), pipeline_mode=pl.Buffered(1)),
            pl.BlockSpec((CONV_B, W_B), lambda b, c: (0, 0)), vec,
            wspec, vec, wspec, vec, vec,
            pl.BlockSpec((1, 1, W_B), smap),
            pl.BlockSpec((1, TAIL, W_B), smap),
        ],
        out_specs=[
            pl.BlockSpec((Tc, QKVG_W), row),
            pl.BlockSpec((Tc, W_B), row),
            pl.BlockSpec((1, 1, W_B), lambda b, c: (b, 0, 0)),
            pl.BlockSpec((1, CONV_B - 1, W_B), lambda b, c: (b, 0, 0)),
        ],
        out_shape=[
            jax.ShapeDtypeStruct((B * T, QKVG_W), F32),
            jax.ShapeDtypeStruct((B * T, W_B), BF16),
            jax.ShapeDtypeStruct((B, 1, W_B), F32),
            jax.ShapeDtypeStruct((B, CONV_B - 1, W_B), F32),
        ],
        scratch_shapes=[pltpu.VMEM((Tc, D_MODEL), BF16), pltpu.VMEM((Tc + TAIL, W_B), F32),
                        pltpu.VMEM((1, W_B), F32)],
        compiler_params=_cparams(("arbitrary", "arbitrary")),
        name="ein_lru",
    )(x, g.reshape(1, -1), w_in, p["conv_w"], p["conv_b"], p["wa"], p["ba"], p["wx"], p["bx"], p["lam"], h0,
      _tail_rows(conv0))


def _tail_rows(state):
    return jnp.pad(state, ((0, 0), (TAIL - state.shape[1], 0), (0, 0)))


def _lru_seq(z, p, h0, conv0, *, B, T, Tc, shared):
    NC = T // Tc
    smap = (lambda b, c: (0, 0, 0)) if shared else (lambda b, c: (b, 0, 0))
    row = lambda b, c: b * NC + c
    vec = lambda a: pl.BlockSpec((1, W_B), lambda b, c: (0, 0))
    wspec = pl.BlockSpec((NB_B, BS_B, BS_B), lambda b, c: (0, 0, 0))
    return pl.pallas_call(
        functools.partial(_lru_seq_body, Tc=Tc),
        grid=(B, NC),
        in_specs=[
            pl.BlockSpec((Tc, W_B), lambda b, c: (row(b, c), 3)),
            pl.BlockSpec((Tc, W_B), lambda b, c: (row(b, c), 4)),
            pl.BlockSpec((CONV_B, W_B), lambda b, c: (0, 0)), vec(0),
            wspec, vec(0), wspec, vec(0), vec(0),
            pl.BlockSpec((1, 1, W_B), smap),
            pl.BlockSpec((1, TAIL, W_B), smap),
        ],
        out_specs=[
            pl.BlockSpec((Tc, W_B), lambda b, c: (row(b, c), 0)),
            pl.BlockSpec((1, 1, W_B), lambda b, c: (b, 0, 0)),
            pl.BlockSpec((1, CONV_B - 1, W_B), lambda b, c: (b, 0, 0)),
        ],
        out_shape=[
            jax.ShapeDtypeStruct((B * T, W_B), BF16),
            jax.ShapeDtypeStruct((B, 1, W_B), F32),
            jax.ShapeDtypeStruct((B, CONV_B - 1, W_B), F32),
        ],
        scratch_shapes=[pltpu.VMEM((Tc + TAIL, W_B), F32), pltpu.VMEM((1, W_B), F32)],
        compiler_params=_cparams(("arbitrary", "arbitrary")),
        name="lru_seq",
    )(z, z, p["conv_w"], p["conv_b"], p["wa"], p["ba"], p["wx"], p["bx"], p["lam"], h0, _tail_rows(conv0))


def _lru_dec_body(xb_ref, gb_ref, s0_ref, s1_ref, s2_ref, cw_ref, cb_ref, wa_ref, ba_ref, wx_ref, bx_ref, lam_ref,
                  h0_ref, y_ref, hout_ref):
    for g in range(NB_B):
        cs = slice(g * BS_B, (g + 1) * BS_B)
        xc = (cb_ref[:, cs] + cw_ref[0:1, cs] * s0_ref[:, cs] + cw_ref[1:2, cs] * s1_ref[:, cs]
              + cw_ref[2:3, cs] * s2_ref[:, cs] + cw_ref[3:4, cs] * xb_ref[:, cs])
        a, u = _lru_gates(xc, wa_ref[g], ba_ref[:, cs], wx_ref[g], bx_ref[:, cs], lam_ref[:, cs])
        hs = a * h0_ref[:, cs] + u
        y_ref[:, cs] = (hs * jax.nn.gelu(gb_ref[:, cs])).astype(BF16)
        hout_ref[:, cs] = hs


def _lru_dec(z, p, h0, conv0):
    B = z.shape[0]
    full = lambda shape: pl.BlockSpec(shape, lambda i: (0,) * len(shape))
    return pl.pallas_call(
        _lru_dec_body,
        grid=(1,),
        in_specs=[
            pl.BlockSpec((B, W_B), lambda i: (0, 3)), pl.BlockSpec((B, W_B), lambda i: (0, 4)),
            full((B, W_B)), full((B, W_B)), full((B, W_B)),
            full((CONV_B, W_B)), full((1, W_B)),
            full((NB_B, BS_B, BS_B)), full((1, W_B)), full((NB_B, BS_B, BS_B)), full((1, W_B)), full((1, W_B)),
            full((B, W_B)),
        ],
        out_specs=[full((B, W_B)), full((B, W_B))],
        out_shape=[jax.ShapeDtypeStruct((B, W_B), BF16), jax.ShapeDtypeStruct((B, W_B), F32)],
        compiler_params=_cparams(("arbitrary",)),
        name="lru_dec",
    )(z, z, conv0[:, 0], conv0[:, 1], conv0[:, 2], p["conv_w"], p["conv_b"], p["wa"], p["ba"], p["wx"], p["bx"],
      p["lam"], h0)


def _ffn_seq_body(ug_ref, uv_ref, cw_ref, cb_ref, tail0_ref, m_ref, cout_ref, xe_ref, *, Tc):
    @pl.when(pl.program_id(1) == 0)
    def _():
        xe_ref[0:TAIL, :] = tail0_ref[0]

    x = ug_ref[...]
    xe_ref[TAIL:TAIL + Tc, :] = x
    cw = cw_ref[...]
    c = cb_ref[...]
    for j in range(CONV_F - 1):
        off = TAIL - (CONV_F - 1) + j
        c = c + cw[j:j + 1] * xe_ref[off:off + Tc, :]
    c = c + cw[CONV_F - 1:CONV_F] * x
    m_ref[...] = (jax.nn.gelu(c) * uv_ref[...]).astype(BF16)
    cout_ref[0] = xe_ref[Tc + TAIL - (CONV_F - 1):Tc + TAIL, :]
    xe_ref[0:TAIL, :] = xe_ref[Tc:Tc + TAIL, :]


def _ffn_seq(u, cw, cb, conv0, *, B, T, Tc, shared):
    NC = T // Tc
    smap = (lambda b, c: (0, 0, 0)) if shared else (lambda b, c: (b, 0, 0))
    row = lambda b, c: b * NC + c
    return pl.pallas_call(
        functools.partial(_ffn_seq_body, Tc=Tc),
        grid=(B, NC),
        in_specs=[
            pl.BlockSpec((Tc, D_FF), lambda b, c: (row(b, c), 0)),
            pl.BlockSpec((Tc, D_FF), lambda b, c: (row(b, c), 1)),
            pl.BlockSpec((CONV_F, D_FF), lambda b, c: (0, 0)),
            pl.BlockSpec((1, D_FF), lambda b, c: (0, 0)),
            pl.BlockSpec((1, TAIL, D_FF), smap),
        ],
        out_specs=[
            pl.BlockSpec((Tc, D_FF), lambda b, c: (row(b, c), 0)),
            pl.BlockSpec((1, CONV_F - 1, D_FF), lambda b, c: (b, 0, 0)),
        ],
        out_shape=[
            jax.ShapeDtypeStruct((B * T, D_FF), BF16),
            jax.ShapeDtypeStruct((B, CONV_F - 1, D_FF), F32),
        ],
        scratch_shapes=[pltpu.VMEM((Tc + TAIL, D_FF), F32)],
        compiler_params=_cparams(("arbitrary", "arbitrary")),
        name="ffn_seq",
    )(u, u, cw, cb, _tail_rows(conv0))


def _ffn_dec_body(ug_ref, uv_ref, s0_ref, s1_ref, cw_ref, cb_ref, m_ref):
    cw = cw_ref[...]
    c = cb_ref[...] + cw[0:1] * s0_ref[...] + cw[1:2] * s1_ref[...] + cw[2:3] * ug_ref[...]
    m_ref[...] = (jax.nn.gelu(c) * uv_ref[...]).astype(BF16)


def _ffn_dec(u, cw, cb, conv0):
    B = u.shape[0]
    full = lambda shape: pl.BlockSpec(shape, lambda i: (0,) * len(shape))
    return pl.pallas_call(
        _ffn_dec_body,
        grid=(1,),
        in_specs=[pl.BlockSpec((B, D_FF), lambda i: (0, 0)), pl.BlockSpec((B, D_FF), lambda i: (0, 1)),
                  full((B, D_FF)), full((B, D_FF)), full((CONV_F, D_FF)), full((1, D_FF))],
        out_specs=full((B, D_FF)),
        out_shape=jax.ShapeDtypeStruct((B, D_FF), BF16),
        compiler_params=_cparams(("arbitrary",)),
        name="ffn_dec",
    )(u, u, conv0[:, 0], conv0[:, 1], cw, cb)


def _ffn_fused_body(x_ref, g_ref, wup_ref, cw_ref, cb_ref, wdn_ref, tail0_ref, o_ref, cout_ref,
                    xn_ref, m_ref, tail_ref, *, tm, tc):
    @pl.when(pl.program_id(1) == 0)
    def _():
        tail_ref[...] = tail0_ref[0]

    x = x_ref[...]
    xn_ref[...] = _rms(x, g_ref[...]).astype(BF16)
    row = lax.broadcasted_iota(jnp.int32, (tm, 1), 0)
    for j in range(D_FF // tc):
        cols = slice(j * tc, (j + 1) * tc)
        ug = _dot(xn_ref[...], wup_ref[:, cols])
        uv = _dot(xn_ref[...], wup_ref[:, D_FF + j * tc:D_FF + (j + 1) * tc])
        t1 = tail_ref[TAIL - 1:TAIL, cols]
        t2 = tail_ref[TAIL - 2:TAIL - 1, cols]
        s1 = jnp.where(row == 0, t1, pltpu.roll(ug, 1, 0))
        s2 = jnp.where(row == 0, t2, jnp.where(row == 1, t1, pltpu.roll(ug, 2, 0)))
        c = cb_ref[:, cols] + cw_ref[0:1, cols] * s2 + cw_ref[1:2, cols] * s1 + cw_ref[2:3, cols] * ug
        m_ref[:, cols] = (jax.nn.gelu(c) * uv).astype(BF16)
        tail_ref[:, cols] = ug[tm - TAIL:tm]
        cout_ref[0, :, cols] = ug[tm - (CONV_F - 1):tm]
    o_ref[...] = x + _dot(m_ref[...], wdn_ref[...])


def _ffn_fused(x, g, w_up, cw, cb, w_down, conv0, *, B, T, tm, tc, shared):
    NT = T // tm
    smap = (lambda b, t: (0, 0, 0)) if shared else (lambda b, t: (b, 0, 0))
    tile = pl.BlockSpec((tm, D_MODEL), lambda b, t: (b * NT + t, 0))
    const = lambda a: pl.BlockSpec(a.shape, lambda b, t: (0, 0), pipeline_mode=pl.Buffered(1))
    return pl.pallas_call(
        functools.partial(_ffn_fused_body, tm=tm, tc=tc),
        grid=(B, NT),
        in_specs=[tile, pl.BlockSpec((1, D_MODEL), lambda b, t: (0, 0)), const(w_up),
                  pl.BlockSpec((CONV_F, D_FF), lambda b, t: (0, 0)), pl.BlockSpec((1, D_FF), lambda b, t: (0, 0)),
                  const(w_down), pl.BlockSpec((1, TAIL, D_FF), smap)],
        out_specs=[tile, pl.BlockSpec((1, CONV_F - 1, D_FF), lambda b, t: (b, 0, 0))],
        out_shape=[jax.ShapeDtypeStruct((B * T, D_MODEL), F32),
                   jax.ShapeDtypeStruct((B, CONV_F - 1, D_FF), F32)],
        scratch_shapes=[pltpu.VMEM((tm, D_MODEL), BF16), pltpu.VMEM((tm, D_FF), BF16),
                        pltpu.VMEM((TAIL, D_FF), F32)],
        compiler_params=_cparams(("arbitrary", "arbitrary")),
        name="ffn_fused",
    )(x, g.reshape(1, -1), w_up, cw, cb, w_down, _tail_rows(conv0))


_PREP_W = ("mix", "w_r", "w_k", "w_v", "w0", "w1", "w2", "a0", "a1", "a2", "g1", "g2")
_PREP_VP = ("v0", "v1", "v2")


def _rwkv_prep_body(*refs, has_vp, carry):
    it = iter(refs)
    x_ref, gn_ref = next(it), next(it)
    p = {n: next(it) for n in _PREP_W}
    if has_vp:
        p.update({n: next(it) for n in _PREP_VP})
        vf_ref = next(it)
    prev_ref = next(it)
    r_ref, lw_ref, k_ref, v_ref, as_ref, g_ref, hn_ref = (next(it) for _ in range(7))
    h = _rms(x_ref[...], gn_ref[...])
    if carry:
        carry_ref = next(it)

        @pl.when(pl.program_id(1) == 0)
        def _():
            carry_ref[...] = prev_ref[0]

        row = lax.broadcasted_iota(jnp.int32, (h.shape[0], 1), 0)
        hprev = jnp.where(row == 0, carry_ref[...], pltpu.roll(h, 1, 0))
        carry_ref[...] = h[h.shape[0] - 1:]
        hn_ref[0] = h[h.shape[0] - 1:]
    else:
        hprev = prev_ref[...]
        hn_ref[...] = h
    xx = hprev - h
    mix = p["mix"][...]
    xs = [(h + xx * mix[n:n + 1]).astype(BF16) for n in range(6)]
    xr, xw, xk, xv, xa, xg = xs
    r_ref[...] = _dot(xr, p["w_r"][...])
    k_ref[...] = _dot(xk, p["w_k"][...])
    v = _dot(xv, p["w_v"][...])
    wl = p["w0"][...] + _dot(jnp.tanh(_dot(xw, p["w1"][...])).astype(BF16), p["w2"][...])
    lw_ref[...] = -jnp.exp(_log_sigmoid(wl) - 0.5)
    if has_vp:
        gate = jax.nn.sigmoid(p["v0"][...] + _dot(_dot(xv, p["v1"][...]).astype(BF16), p["v2"][...]))
        v = v + (vf_ref[...] - v) * gate
    v_ref[...] = v
    as_ref[...] = jax.nn.sigmoid(p["a0"][...] + _dot(_dot(xa, p["a1"][...]).astype(BF16), p["a2"][...]))
    g_ref[...] = _dot(jax.nn.sigmoid(_dot(xg, p["g1"][...])).astype(BF16), p["g2"][...])


def _rwkv_prep(x, gn, p, vp, v_first, prev, *, B, T, tm, carry, shared=False):
    R = x.shape[0]
    has_vp = vp is not None
    if carry:
        NT = T // tm
        grid = (B, NT)
        rowmap = lambda b, t: (b * NT + t, 0)
        cmap2 = lambda b, t: (0, 0)
        pmap = (lambda b, t: (0, 0, 0)) if shared else (lambda b, t: (b, 0, 0))
        prev_spec = pl.BlockSpec((1, 1, D_MODEL), pmap)
        hn_spec = pl.BlockSpec((1, 1, D_MODEL), lambda b, t: (b, 0, 0))
        hn_shape = jax.ShapeDtypeStruct((B, 1, D_MODEL), F32)
        sem = ("arbitrary", "arbitrary")
    else:
        grid = (R // tm,)
        rowmap = lambda i: (i, 0)
        cmap2 = lambda i: (0, 0)
        prev_spec = pl.BlockSpec((tm, D_MODEL), rowmap)
        hn_spec = pl.BlockSpec((tm, D_MODEL), rowmap)
        hn_shape = jax.ShapeDtypeStruct((R, D_MODEL), F32)
        sem = ("arbitrary",)
    tile = pl.BlockSpec((tm, D_MODEL), rowmap)
    ins = [x, gn.reshape(1, -1)]
    specs = [tile, pl.BlockSpec((1, D_MODEL), cmap2)]
    names = _PREP_W + (_PREP_VP if has_vp else ())
    src = dict(p)
    if has_vp:
        src.update(vp)
    for n in names:
        ins.append(src[n])
        specs.append(pl.BlockSpec(src[n].shape, cmap2))
    if has_vp:
        ins.append(v_first)
        specs.append(tile)
    ins.append(prev)
    specs.append(prev_spec)
    outs = pl.pallas_call(
        functools.partial(_rwkv_prep_body, has_vp=has_vp, carry=carry),
        grid=grid,
        in_specs=specs,
        out_specs=[tile] * 6 + [hn_spec],
        out_shape=[jax.ShapeDtypeStruct((R, D_MODEL), F32)] * 6 + [hn_shape],
        scratch_shapes=[pltpu.VMEM((1, D_MODEL), F32)] if carry else [],
        compiler_params=_cparams(sem),
        name="rwkv_prep",
    )(*ins)
    return outs


def _rwkv_chunk_body(r_ref, lw_ref, k_ref, v_ref, as_ref, g_ref, kk_ref, ka_ref, rk_ref, gg_ref, gb_ref, s0_ref,
                     o_ref, s_ref, *, Tc, C, n_pair, unroll):
    @pl.when(pl.program_id(2) == 0)
    def _():
        s_ref[...] = s0_ref[...]

    C2 = 2 * C
    lane = lax.broadcasted_iota(jnp.int32, (1, LANES), 1)
    m0 = (lane < HS_C).astype(F32)
    m1 = 1.0 - m0
    li = lax.broadcasted_iota(jnp.int32, (LANES, LANES), 0)
    lj = lax.broadcasted_iota(jnp.int32, (LANES, LANES), 1)
    same_head = (li >= HS_C) == (lj >= HS_C)
    bd_ones = same_head.astype(F32).astype(BF16)
    bd_avg = (same_head.astype(F32) * (1.0 / HS_C)).astype(BF16)
    ti = lax.broadcasted_iota(jnp.int32, (C, C), 0)
    si = lax.broadcasted_iota(jnp.int32, (C, C), 1)
    tri = (ti >= si).astype(F32).astype(BF16)
    tri2 = jnp.concatenate([tri, tri], axis=1)
    r2 = lax.broadcasted_iota(jnp.int32, (C2, C2), 0)
    c2 = lax.broadcasted_iota(jnp.int32, (C2, C2), 1)
    same = (r2 >= C) == (c2 >= C)
    tt, ss = r2 & (C - 1), c2 & (C - 1)
    m_strict = same & (tt > ss)
    m_incl = same & (tt >= ss)
    same_sub = (tt & -RWKV_SUB) == (ss & -RWKV_SUB)
    m_sub = m_strict & same_sub
    m_off = m_strict & jnp.logical_not(same_sub)
    eye = (r2 == c2).astype(F32)
    nb = C // RWKV_SUB
    stack = lambda x: jnp.concatenate([x * m0, x * m1], axis=0)
    dup = lambda x: jnp.concatenate([x, x], axis=0)

    def sub_chunk(ci, carry):
        row_l = [pl.ds(pl.multiple_of((ci * unroll + u) * C, C), C) for u in range(unroll)]
        rows = [rw for rw in row_l for _ in range(n_pair)]
        lanes = [slice(p * LANES, (p + 1) * LANES) for p in range(n_pair)] * unroll
        each = lambda f, *cols: [f(*xs) for xs in zip(*cols)]

        def load(rw, ls):
            r, lw, k, v = r_ref[rw, ls], lw_ref[rw, ls], k_ref[rw, ls], v_ref[rw, ls]
            asig = as_ref[rw, ls]
            kk = k * kk_ref[:, ls]
            k2 = k * (1.0 + (asig - 1.0) * ka_ref[:, ls])
            return r, lw, k2, v, asig, kk

        r, lw, k2, v, asig, kk = zip(*each(load, rows, lanes))
        ssq = each(lambda x: _dot(_bf(x * x), bd_ones), kk)
        cs = each(lambda x: _dot(tri2, jnp.concatenate(_split(x), axis=0)), lw)
        kk = each(lambda x, q: x * lax.rsqrt(jnp.maximum(q, 1e-24)), kk, ssq)
        b = each(lambda x, q: x * q, kk, asig)
        cend = [x[C - 1:C] for x in cs]
        e_neg = each(lambda x: jnp.exp(-x), cs)
        e_end = each(lambda x, y: jnp.exp(y - x), cs, cend)
        As = each(lambda x, c, l: _split(stack(-x * jnp.exp(c - l))), kk, cs, lw)
        Bd = each(lambda x, e: _split(dup(x * e)), b, e_neg)
        Rs = each(lambda x, c: _bf(stack(x * jnp.exp(c))), r, cs)
        Vs = each(lambda x: _bf(stack(x)), v)
        Kd = each(lambda x, e: _bf(dup(x * e)), k2, e_neg)
        G = each(_dot3_wide_nt, As, Bd)
        Aak = each(lambda x, y: _bf(jnp.where(m_strict, _dot_nt(x[0], y), 0.0)), As, Kd)
        Arb = each(lambda x, y: _bf(jnp.where(m_incl, _dot_nt(x, y[0]), 0.0)), Rs, Bd)
        Ark = each(lambda x, y: _bf(jnp.where(m_incl, _dot_nt(x, y), 0.0)), Rs, Kd)
        Nd = each(lambda x: jnp.where(m_sub, x, 0.0), G)
        P = each(lambda x: eye + x, Nd)
        Qs = each(_split, Nd)
        for _ in range(3):
            Qs = each(lambda q: _split(_dot3_wide(q, q)), Qs)
            P = each(lambda x, q: x + _dot3_wide(_split(x), q), P, Qs)
        if nb > 1:
            Pb = each(_bf, P)
            M = each(lambda x, y: _dot(x, _bf(jnp.where(m_off, y, 0.0))), Pb, G)
            Tm = each(lambda x: eye + x, M)
            for _ in range(int(np.ceil(np.log2(nb))) - 1):
                M = each(lambda x: _dot(_bf(x), _bf(x)), M)
                Tm = each(lambda x, y: x + _dot(_bf(x), _bf(y)), Tm, M)
            Tinv = each(lambda x, y: _bf(_dot(_bf(x), y)), Tm, Pb)
        else:
            Tinv = each(_bf, P)
        bE = each(lambda x, e: _bf(stack(x * e)), b, e_end)
        kE = each(lambda x, e: _bf(stack(x * e)), k2, e_end)
        gC = each(jnp.exp, cend)
        Ys = []
        for u in range(unroll):
            sl = slice(u * n_pair, (u + 1) * n_pair)
            S = [s_ref[0, p] for p in range(n_pair)]
            Sb = each(_bf, S)
            X = each(lambda a_, s_, k_, v_: _bf(_dot_nt(a_[0], s_) + _dot(k_, v_)), As[sl], Sb, Aak[sl], Vs[sl])
            Us = each(lambda t_, x_: _bf(_dot(t_, x_)), Tinv[sl], X)
            Ys += each(lambda r_, s_, b_, u_, k_, v_: _dot_nt(r_, s_) + _dot(b_, u_) + _dot(k_, v_),
                       Rs[sl], Sb, Arb[sl], Us, Ark[sl], Vs[sl])
            Sn = each(lambda s_, g_, u_, b_, v_, k_: s_ * g_ + _dot_tn(u_, b_) + _dot_tn(v_, k_),
                      S, gC[sl], Us, bE[sl], Vs[sl], kE[sl])
            for p in range(n_pair):
                s_ref[0, p] = Sn[p]
        y = [x[:C] + x[C:] for x in Ys]
        mu = each(lambda x: _dot(_bf(x), bd_avg), y)
        d = each(lambda x, m: x - m, y, mu)
        var = each(lambda x: _dot(_bf(x * x), bd_avg), d)
        bonus = each(lambda r_, k_, ls: _dot(_bf(r_ * k_ * rk_ref[:, ls]), bd_ones), r, k2, lanes)
        for i, (rw, ls) in enumerate(zip(rows, lanes)):
            on = d[i] * lax.rsqrt(var[i] + GN_EPS_C) * gg_ref[:, ls] + gb_ref[:, ls]
            o_ref[rw, ls] = ((on + bonus[i] * v[i]) * g_ref[rw, ls]).astype(BF16)
        return carry

    lax.fori_loop(0, Tc // (C * unroll), sub_chunk, 0)


def _rwkv_chunk(arrs, p, s0, *, B, T, Tc, C, ppb, shared, unroll=1):
    NC = T // Tc
    NP = N_PAIR // ppb
    Wb = ppb * LANES
    tile = pl.BlockSpec((Tc, Wb), lambda b, q, c: (b * NC + c, q))
    vec = pl.BlockSpec((1, Wb), lambda b, q, c: (0, q))
    smap = (lambda b, q, c: (0, q, 0, 0)) if shared else (lambda b, q, c: (b, q, 0, 0))
    return pl.pallas_call(
        functools.partial(_rwkv_chunk_body, Tc=Tc, C=C, n_pair=ppb, unroll=unroll),
        grid=(B, NP, NC),
        in_specs=[tile] * 6 + [vec] * 5 + [pl.BlockSpec((1, ppb, LANES, LANES), smap)],
        out_specs=[tile, pl.BlockSpec((1, ppb, LANES, LANES), lambda b, q, c: (b, q, 0, 0))],
        out_shape=[jax.ShapeDtypeStruct((B * T, D_MODEL), BF16),
                   jax.ShapeDtypeStruct((B, N_PAIR, LANES, LANES), F32)],
        compiler_params=_cparams(("arbitrary", "arbitrary", "arbitrary")),
        name="rwkv_chunk",
    )(*arrs, p["k_k"], p["k_a"], p["r_k"], p["gn_g"], p["gn_b"], s0)


def _rwkv_dec_body(r_ref, lw_ref, k_ref, v_ref, as_ref, g_ref, kk_ref, ka_ref, rk_ref, gg_ref, gb_ref, s0_ref,
                   o_ref, s_ref, r_scr, w_scr, k2_scr, v_scr, a_scr, b_scr, y_scr, *, nb):
    n_head = 2 * N_PAIR
    ri = lax.broadcasted_iota(jnp.int32, (HS_C, HS_C), 0)
    ci = lax.broadcasted_iota(jnp.int32, (HS_C, HS_C), 1)
    eye = (ri == ci).astype(F32)
    for h in range(n_head):
        ls = slice(h * HS_C, (h + 1) * HS_C)
        k, asig = k_ref[:, ls], as_ref[:, ls]
        kk = k * kk_ref[:, ls]
        kk = kk * lax.rsqrt(jnp.maximum(jnp.sum(kk * kk, axis=1, keepdims=True), 1e-24))
        a_scr[h] = -kk
        b_scr[h] = kk * asig
        k2_scr[h] = k * (1.0 + (asig - 1.0) * ka_ref[:, ls])
        r_scr[h] = r_ref[:, ls]
        v_scr[h] = v_ref[:, ls]
        w_scr[h] = jnp.exp(lw_ref[:, ls])

    def per_sample(s, carry):
        row = pl.ds(s, 1)
        heads = range(n_head)
        S = [s0_ref[s, h] for h in heads]
        sa = [jnp.sum(S[h] * a_scr[h, row, :], axis=1, keepdims=True) for h in heads]
        v_col = [jnp.sum(eye * v_scr[h, row, :], axis=1, keepdims=True) for h in heads]
        Sn = [S[h] * w_scr[h, row, :] + sa[h] * b_scr[h, row, :] + v_col[h] * k2_scr[h, row, :] for h in heads]
        for h in heads:
            s_ref[s, h] = Sn[h]
        y_col = [jnp.sum(Sn[h] * r_scr[h, row, :], axis=1, keepdims=True) for h in heads]
        for h in heads:
            y_scr[h, row, :] = jnp.sum(eye * y_col[h], axis=0, keepdims=True)
        return carry

    lax.fori_loop(0, nb, per_sample, 0)
    for h in range(n_head):
        ls = slice(h * HS_C, (h + 1) * HS_C)
        y = y_scr[h]
        d = y - jnp.mean(y, axis=1, keepdims=True)
        var = jnp.mean(d * d, axis=1, keepdims=True)
        on = d * lax.rsqrt(var + GN_EPS_C) * gg_ref[:, ls] + gb_ref[:, ls]
        bonus = jnp.sum(r_scr[h] * k2_scr[h] * rk_ref[:, ls], axis=1, keepdims=True) * v_scr[h]
        o_ref[:, ls] = (on + bonus) * g_ref[:, ls]


def _rwkv_dec(arrs, p, s0, *, nb, layer, prev=None):
    B = s0.shape[1]
    n_head = 2 * N_PAIR
    tile = pl.BlockSpec((nb, D_MODEL), lambda i: (i, 0))
    vec = pl.BlockSpec((1, D_MODEL), lambda i: (0, 0))
    sspec = pl.BlockSpec((None, nb, n_head, HS_C, HS_C), lambda i: (layer, i, 0, 0, 0))
    ins = [*arrs, p["k_k"], p["k_a"], p["r_k"], p["gn_g"], p["gn_b"], s0]
    specs = [tile] * 6 + [vec] * 5 + [sspec]
    body = functools.partial(_rwkv_dec_body, nb=nb)
    aliases = {}
    if prev is not None:
        ins.append(prev)
        specs.append(pl.BlockSpec(memory_space=pl.ANY))
        aliases = {len(ins) - 1: 1}
        body = _drop_ref(body, len(ins) - 1)
    return pl.pallas_call(
        body,
        grid=(B // nb,),
        in_specs=specs,
        out_specs=[tile, sspec],
        out_shape=[jax.ShapeDtypeStruct((B, D_MODEL), F32), jax.ShapeDtypeStruct(s0.shape, F32)],
        scratch_shapes=[pltpu.VMEM((n_head, nb, HS_C), F32)] * 7,
        input_output_aliases=aliases,
        compiler_params=_cparams(("arbitrary",)),
        name="rwkv_dec",
    )(*ins)


def _from_blockdiag(s):
    B = s.shape[0]
    return jnp.stack([s[:, :, :HS_C, :HS_C], s[:, :, HS_C:, HS_C:]], axis=2).reshape(B, 2 * N_PAIR, HS_C, HS_C)


MAIN_TM = 2048
MAIN_TN = 512
RET_C = 128
LRU_TC = 256
FFN_TC = 256
FFN_TM = 512
FFN_TCOL = 256
PREP_TM = 256
RWKV_TC = 256
RWKV_C = 64
RWKV_PPB = 8
RWKV_UNROLL = 2
RWKV_DEC_NB = 8
RET_DEC_NB = 8


def kernel(x_prompt, x_sample, state_ret, state_lru, state_lru_conv, state_rwkv, state_shift, state_ffn_conv,
           meta_tokens, norm_mix_g, norm_ffn_g, norm_final_g,
           ev_w_in, ev_ret_gn_g, ev_lru_conv_w, ev_lru_conv_b, ev_lru_wa, ev_lru_ba, ev_lru_wx, ev_lru_bx,
           ev_lru_lambda, ev_w_out,
           od_mix, od_w_r, od_w_k, od_w_v, od_w0, od_w1, od_w2, od_a0, od_a1, od_a2, od_v0, od_v1, od_v2,
           od_g1, od_g2, od_k_k, od_k_a, od_r_k, od_gn_g, od_gn_b, od_w_o,
           ff_w_up, ff_conv_w, ff_conv_b, ff_w_down):
    BP, TP, _ = x_prompt.shape
    BS = x_sample.shape[0]
    NS = BS + N_META
    bf = lambda a: a.astype(BF16)
    row = lambda a: a.reshape(1, -1)

    def lora_in(w):
        return bf(jnp.pad(w, ((0, 0), (0, LORA_PAD - w.shape[1]))))

    def lora_out(w):
        return bf(jnp.pad(w, ((0, LORA_PAD - w.shape[0]), (0, 0))))

    xm = x_prompt.reshape(BP * TP, D_MODEL)
    xs = jnp.concatenate([x_sample.reshape(BS, D_MODEL), meta_tokens.astype(x_prompt.dtype)], axis=0)

    cos_m, sin_m = _rope_tables(N_META + jnp.arange(TP, dtype=jnp.int32))
    cos_t, sin_t = _rope_tables(jnp.arange(N_META, dtype=jnp.int32))
    cos_s, sin_s = _rope_tables(PAST_LEN + jnp.arange(PAD_T, dtype=jnp.int32))

    out = {k: [] for k in ("ret_p", "lru_p", "lconv_p", "rwkv_p", "shift_p", "ffn_p",
                           "ret_s", "lru_s", "lconv_s", "rwkv_s", "shift_s", "ffn_s")}
    v_first_m = v_first_s = None
    ret_s_all = rwkv_s_all = None
    for li in range(4):
        j = li // 2
        if li % 2 == 0:
            w_in = bf(ev_w_in[j])
            w_out_a, w_out_b = bf(ev_w_out[j][:VG_W]), bf(ev_w_out[j][VG_W:])
            gn = ev_ret_gn_g[j]
            lp = dict(conv_w=ev_lru_conv_w[j], conv_b=row(ev_lru_conv_b[j]), wa=bf(ev_lru_wa[j]),
                      ba=row(ev_lru_ba[j]), wx=bf(ev_lru_wx[j]), bx=row(ev_lru_bx[j]), lam=row(ev_lru_lambda[j]))
            zs = _mm([xs], [w_in], g=norm_mix_g[li], tm=NS, tn=MAIN_TN)
            z_smp, z_meta = zs[:BS], zs[BS:]
            ya_t, ret_t = _retention(z_meta, cos_t, sin_t, jnp.zeros((1, H_A, DK_A, DV_A), F32), gn,
                                     B=1, T=N_META, C=N_META, c_true=N_META, shared=False)
            yb_t, lru_t, lconv_t = _lru_seq(z_meta, lp, jnp.zeros((1, 1, W_B), F32),
                                            jnp.zeros((1, CONV_B - 1, W_B), F32), B=1, T=N_META, Tc=N_META,
                                            shared=False)
            ya_s, ret_s_all = _retention(z_smp, cos_s, sin_s, state_ret, gn, B=BS, T=1, C=PAD_T, c_true=1,
                                         shared=False, bb=RET_DEC_NB, single=True, layer=j, prev=ret_s_all)
            yb_s, lru_s = _lru_dec(z_smp, lp, state_lru[j], state_lru_conv[j])
            lconv_s = jnp.concatenate([state_lru_conv[j][:, 1:], z_smp[:, None, 3 * W_B:4 * W_B]], axis=1)
            ya = jnp.concatenate([ya_s, ya_t.astype(F32)], axis=0)
            yb = jnp.concatenate([yb_s, yb_t], axis=0)
            xs = _mm([ya, yb], [w_out_a, w_out_b], res=xs, tm=NS, tn=MAIN_TN)
            zm, yb_m, lru_m, lconv_m = _ein_lru(xm, norm_mix_g[li], w_in, lp, lru_t, lconv_t, B=BP, T=TP, Tc=LRU_TC,
                                                tn=MAIN_TN, shared=True)
            ya_m, ret_m = _retention(zm, cos_m, sin_m, ret_t, gn, B=BP, T=TP, C=RET_C, c_true=RET_C, shared=True)
            xm = _mm([ya_m, yb_m], [w_out_a, w_out_b], res=xm, tm=MAIN_TM, tn=MAIN_TN)
            out["ret_p"].append(ret_m)
            out["lru_p"].append(lru_m[:, 0])
            out["lconv_p"].append(lconv_m)
            out["lru_s"].append(lru_s)
            out["lconv_s"].append(lconv_s)
        else:
            p = dict(mix=od_mix[j], w_r=bf(od_w_r[j]), w_k=bf(od_w_k[j]), w_v=bf(od_w_v[j]), w0=row(od_w0[j]),
                     w1=lora_in(od_w1[j]), w2=lora_out(od_w2[j]), a0=row(od_a0[j]), a1=lora_in(od_a1[j]),
                     a2=lora_out(od_a2[j]), g1=lora_in(od_g1[j]), g2=lora_out(od_g2[j]),
                     k_k=row(od_k_k[j]), k_a=row(od_k_a[j]), r_k=row(od_r_k[j]), gn_g=row(od_gn_g[j]),
                     gn_b=row(od_gn_b[j]))
            vp = None
            if j > 0:
                vp = dict(v0=row(od_v0[j - 1]), v1=lora_in(od_v1[j - 1]), v2=lora_out(od_v2[j - 1]))
            w_o = bf(od_w_o[j])
            gmix = norm_mix_g[li]
            pre_t = _rwkv_prep(xs[BS:], gmix, p, vp, None if vp is None else v_first_s[BS:],
                               jnp.zeros((1, 1, D_MODEL), F32), B=1, T=N_META, tm=N_META, carry=True)
            o_t, rw_t = _rwkv_chunk(pre_t[:6], p, jnp.zeros((1, N_PAIR, LANES, LANES), F32),
                                    B=1, T=N_META, Tc=N_META, C=N_META, ppb=RWKV_PPB, shared=False)
            shift_t = pre_t[6]
            pre_s = _rwkv_prep(xs[:BS], gmix, p, vp, None if vp is None else v_first_s[:BS],
                               state_shift[j], B=BS, T=1, tm=BS, carry=False)
            o_s, rwkv_s_all = _rwkv_dec(pre_s[:6], p, state_rwkv, nb=RWKV_DEC_NB, layer=j, prev=rwkv_s_all)
            if vp is None:
                v_first_s = jnp.concatenate([pre_s[3], pre_t[3]], axis=0)
            o_small = jnp.concatenate([o_s, o_t.astype(F32)], axis=0)
            xs = _mm([o_small], [w_o], res=xs, tm=NS, tn=MAIN_TN)
            pre_m = _rwkv_prep(xm, gmix, p, vp, v_first_m, shift_t, B=BP, T=TP, tm=PREP_TM, carry=True, shared=True)
            if vp is None:
                v_first_m = pre_m[3]
            o_m, rw_m = _rwkv_chunk(pre_m[:6], p, rw_t, B=BP, T=TP, Tc=RWKV_TC, C=RWKV_C, ppb=RWKV_PPB, shared=True,
                                    unroll=RWKV_UNROLL)
            xm = _mm([o_m], [w_o], res=xm, tm=MAIN_TM, tn=MAIN_TN)
            out["rwkv_p"].append(_from_blockdiag(rw_m))
            out["shift_p"].append(pre_m[6][:, 0])
            out["shift_s"].append(pre_s[6])
        w_up, w_down = bf(ff_w_up[li]), bf(ff_w_down[li])
        cw, cb = ff_conv_w[li], row(ff_conv_b[li])
        us = _mm([xs], [w_up], g=norm_ffn_g[li], tm=NS, tn=MAIN_TN)
        m_t, ffn_t = _ffn_seq(us[BS:], cw, cb, jnp.zeros((1, CONV_F - 1, D_FF), F32), B=1, T=N_META, Tc=N_META,
                              shared=False)
        m_s = _ffn_dec(us[:BS], cw, cb, state_ffn_conv[li])
        ffn_s = jnp.concatenate([state_ffn_conv[li][:, 1:], us[:BS, None, :D_FF]], axis=1)
        xs = _mm([jnp.concatenate([m_s, m_t], axis=0)], [w_down], res=xs, tm=NS, tn=MAIN_TN)
        xm, ffn_m = _ffn_fused(xm, norm_ffn_g[li], w_up, cw, cb, w_down, ffn_t, B=BP, T=TP, tm=FFN_TM, tc=FFN_TCOL,
                               shared=True)
        out["ffn_p"].append(ffn_m)
        out["ffn_s"].append(ffn_s)

    y_prompt = _final_norm(xm, norm_final_g, MAIN_TM).reshape(BP, TP, D_MODEL)
    y_sample = _final_norm(xs, norm_final_g, NS)[:BS].reshape(BS, 1, D_MODEL)
    st = lambda k: jnp.stack(out[k])
    return (y_prompt, y_sample,
            st("ret_p"), st("lru_p"), st("lconv_p"), st("rwkv_p"), st("shift_p"), st("ffn_p"),
            ret_s_all, st("lru_s"), st("lconv_s"), rwkv_s_all, st("shift_s"), st("ffn_s"))
```

```python
import functools

import numpy as np
import jax
import jax.numpy as jnp
from jax import lax
from jax.experimental import pallas as pl
from jax.experimental.pallas import tpu as pltpu

F32 = jnp.float32
BF16 = jnp.bfloat16
HI = lax.Precision.HIGHEST

D_MODEL = 1024
N_META = 16
PAST_LEN = 16384
H_A, DK_A, DV_A = 4, 128, 256
W_B, NB_B, BS_B, CONV_B = 1024, 8, 128, 4
LRU_C = 8.0
HS_C = 64
N_PAIR = D_MODEL // (2 * HS_C)
LORA_PAD = 128
GN_EPS_C = 64e-5
D_FF, CONV_F = 2816, 3
EPS = 1e-6
ROPE_BASE = 10000.0
QK_W, VG_W = H_A * DK_A, H_A * DV_A

LANES = 128
TAIL = 8
VMEM_LIMIT = 52 * 1024 * 1024
PAD_T = 16
RWKV_SUB = 16


def _cparams(sem):
    return pltpu.CompilerParams(dimension_semantics=sem, vmem_limit_bytes=VMEM_LIMIT)


def _dot(a, b, prec=None):
    return jnp.dot(a, b, preferred_element_type=F32, precision=prec)


def _dot_nt(a, b, prec=None):
    return lax.dot_general(a, b, (((1,), (1,)), ((), ())), preferred_element_type=F32, precision=prec)


def _dot_tn(a, b, prec=None):
    return lax.dot_general(a, b, (((0,), (0,)), ((), ())), preferred_element_type=F32, precision=prec)


def _bf(x):
    return x.astype(BF16)


def _split(x):
    hi = x.astype(BF16)
    return hi, (x - hi.astype(F32)).astype(BF16)


def _dot3(a, b, dot):
    return dot(a[0], b[0]) + dot(a[0], b[1]) + dot(a[1], b[0])


def _dot3_wide(a, b):
    n = a[0].shape[1]
    lhs = jnp.concatenate([a[0], a[1]], axis=1)
    rhs = jnp.concatenate([jnp.concatenate([b[0], b[1]], axis=1),
                           jnp.concatenate([b[0], jnp.zeros_like(b[0])], axis=1)], axis=0)
    out = _dot(lhs, rhs)
    return out[:, :n] + out[:, n:]


def _dot3_wide_nt(a, b):
    n = b[0].shape[0]
    lhs = jnp.concatenate([a[0], a[1]], axis=1)
    rhs = jnp.concatenate([jnp.concatenate([b[0], b[0]], axis=1),
                           jnp.concatenate([b[1], jnp.zeros_like(b[1])], axis=1)], axis=0)
    out = _dot_nt(lhs, rhs)
    return out[:, :n] + out[:, n:]


def _rms(x, g):
    return x * lax.rsqrt(jnp.mean(x * x, axis=-1, keepdims=True) + EPS) * g


def _log_sigmoid(x):
    return jnp.minimum(x, 0.0) - jnp.log(1.0 + jnp.exp(-jnp.abs(x)))


def _mm_body(*refs, n_in, norm, res):
    it = iter(refs)
    x_refs = [next(it) for _ in range(n_in)]
    g_ref = next(it) if norm else None
    w_refs = [next(it) for _ in range(n_in)]
    r_ref = next(it) if res else None
    o_ref = next(it)
    if norm:
        xn_ref = next(it)

        @pl.when(pl.program_id(1) == 0)
        def _():
            xn_ref[...] = _rms(x_refs[0][...], g_ref[...]).astype(BF16)

        acc = _dot(xn_ref[...], w_refs[0][...])
    else:
        acc = _dot(x_refs[0][...].astype(BF16), w_refs[0][...])
        for x_ref, w_ref in zip(x_refs[1:], w_refs[1:]):
            acc = acc + _dot(x_ref[...].astype(BF16), w_ref[...])
    if res:
        acc = acc + r_ref[...]
    o_ref[...] = acc.astype(o_ref.dtype)


def _mm(xs, ws, *, g=None, res=None, tm, tn, out_dtype=F32):
    R = xs[0].shape[0]
    N = ws[0].shape[1]
    norm = g is not None
    ins, specs = [], []
    for x in xs:
        ins.append(x)
        specs.append(pl.BlockSpec((tm, x.shape[1]), lambda i, j: (i, 0)))
    if norm:
        ins.append(g.reshape(1, -1))
        specs.append(pl.BlockSpec((1, g.shape[-1]), lambda i, j: (0, 0)))
    for w in ws:
        ins.append(w)
        specs.append(pl.BlockSpec((w.shape[0], tn), lambda i, j: (0, j)))
    if res is not None:
        ins.append(res)
        specs.append(pl.BlockSpec((tm, tn), lambda i, j: (i, j)))
    scratch = [pltpu.VMEM((tm, xs[0].shape[1]), BF16)] if norm else []
    return pl.pallas_call(
        functools.partial(_mm_body, n_in=len(xs), norm=norm, res=res is not None),
        grid=(R // tm, N // tn),
        in_specs=specs,
        out_specs=pl.BlockSpec((tm, tn), lambda i, j: (i, j)),
        out_shape=jax.ShapeDtypeStruct((R, N), out_dtype),
        scratch_shapes=scratch,
        compiler_params=_cparams(("arbitrary", "arbitrary")),
        name="mm",
    )(*ins)


def _final_norm_body(x_ref, g_ref, o_ref):
    o_ref[...] = _rms(x_ref[...], g_ref[...])


def _final_norm(x, g, tm):
    R = x.shape[0]
    return pl.pallas_call(
        _final_norm_body,
        grid=(R // tm,),
        in_specs=[pl.BlockSpec((tm, D_MODEL), lambda i: (i, 0)), pl.BlockSpec((1, D_MODEL), lambda i: (0, 0))],
        out_specs=pl.BlockSpec((tm, D_MODEL), lambda i: (i, 0)),
        out_shape=jax.ShapeDtypeStruct((R, D_MODEL), F32),
        compiler_params=_cparams(("arbitrary",)),
        name="final_norm",
    )(x, g.reshape(1, -1))


def _ret_heads(q_ref, k_ref, v_ref, ga_ref, offs, chunks, cos_ref, sin_ref, s_ref, gn_ref, y_ref, *, C, c_true, single):
    qo, ko, vo, go = offs
    ti = lax.broadcasted_iota(jnp.int32, (C, C), 0)
    si = lax.broadcasted_iota(jnp.int32, (C, C), 1)
    dif = (ti - si).astype(F32)
    trow = lax.broadcasted_iota(jnp.int32, (C, 1), 0)
    tcol = trow.astype(F32)
    chunk = (lambda x: jnp.where(trow == 0, x, 0.0)) if single else (lambda x: x)
    for h in range(H_A):
        lg = float(np.log1p(-(2.0 ** (-5.0 - h))))
        mask = jnp.where(dif >= 0, jnp.exp(lg * jnp.maximum(dif, 0.0)), 0.0)
        dec_in = jnp.exp((tcol + 1.0) * lg)
        dec_k = jnp.exp((c_true - 1.0 - tcol) * lg)
        for rs, sidx, ts in chunks:
            cosf, sinf = cos_ref[ts, :], sin_ref[ts, :]
            q = chunk(q_ref[rs, qo + h * DK_A:qo + (h + 1) * DK_A])
            k = chunk(k_ref[rs, ko + h * DK_A:ko + (h + 1) * DK_A])
            q = q * cosf + pltpu.roll(q, DK_A // 2, 1) * sinf
            k = (k * cosf + pltpu.roll(k, DK_A // 2, 1) * sinf) * (DK_A ** -0.5)
            vb = chunk(v_ref[rs, vo + h * DV_A:vo + (h + 1) * DV_A]).astype(BF16)
            qb = q.astype(BF16)
            sc = _dot_nt(qb, k.astype(BF16)) * mask
            S = s_ref[sidx, h]
            o = _dot(sc.astype(BF16), vb) + _dot(qb, S.astype(BF16)) * dec_in
            s_ref[sidx, h] = float(np.exp(c_true * lg)) * S + _dot_tn((k * dec_k).astype(BF16), vb)
            mu = jnp.mean(o, axis=-1, keepdims=True)
            d = o - mu
            var = jnp.mean(d * d, axis=-1, keepdims=True)
            on = d * lax.rsqrt(var + EPS) * gn_ref[:, h * DV_A:(h + 1) * DV_A]
            ga = ga_ref[rs, go + h * DV_A:go + (h + 1) * DV_A]
            if single:
                on = on[0:1]
            y_ref[rs, h * DV_A:(h + 1) * DV_A] = (ga * jax.nn.sigmoid(ga) * on).astype(y_ref.dtype)


def _ret_body(q_ref, k_ref, v_ref, ga_ref, cos_ref, sin_ref, s0_ref, gn_ref, y_ref, s_ref, *, C, c_true, bb, single):
    @pl.when(pl.program_id(1) == 0)
    def _():
        s_ref[...] = s0_ref[...]

    rows = lambda s: slice(s, s + 1) if single else slice(s * C, (s + 1) * C)
    _ret_heads(q_ref, k_ref, v_ref, ga_ref, (0, 0, 0, 0), [(rows(s), s, slice(None)) for s in range(bb)],
               cos_ref, sin_ref, s_ref, gn_ref, y_ref, C=C, c_true=c_true, single=single)


def _drop_ref(body, i):
    def wrapped(*refs, **kw):
        return body(*refs[:i], *refs[i + 1:], **kw)
    return wrapped


def _retention(z, cosf, sinf, s0, gn, *, B, T, C, c_true, shared, bb=1, single=False, layer=None, prev=None):
    NC = 1 if single else T // C
    assert bb == 1 or (NC == 1 and not shared)
    assert not single or (T == 1 and c_true == 1)
    row = lambda b, c: b * NC + c
    R = bb if single else bb * C
    if layer is None:
        smap = (lambda b, c: (0, 0, 0, 0)) if shared else (lambda b, c: (b, 0, 0, 0))
        s_in = pl.BlockSpec((bb, H_A, DK_A, DV_A), smap)
        s_out = pl.BlockSpec((bb, H_A, DK_A, DV_A), lambda b, c: (b, 0, 0, 0))
        s_shape = (B, H_A, DK_A, DV_A)
    else:
        s_in = s_out = pl.BlockSpec((None, bb, H_A, DK_A, DV_A), lambda b, c: (layer, b, 0, 0, 0))
        s_shape = s0.shape
    ins = [z, z, z, z, cosf, sinf, s0, gn.reshape(1, -1)]
    specs = [
        pl.BlockSpec((R, QK_W), lambda b, c: (row(b, c), 0)),
        pl.BlockSpec((R, QK_W), lambda b, c: (row(b, c), 1)),
        pl.BlockSpec((R, VG_W), lambda b, c: (row(b, c), 1)),
        pl.BlockSpec((R, VG_W), lambda b, c: (row(b, c), 2)),
        pl.BlockSpec((C, DK_A), lambda b, c: (c, 0)),
        pl.BlockSpec((C, DK_A), lambda b, c: (c, 0)),
        s_in,
        pl.BlockSpec((1, VG_W), lambda b, c: (0, 0)),
    ]
    body = functools.partial(_ret_body, C=C, c_true=c_true, bb=bb, single=single)
    aliases = {}
    if prev is not None:
        ins.append(prev)
        specs.append(pl.BlockSpec(memory_space=pl.ANY))
        aliases = {len(ins) - 1: 1}
        body = _drop_ref(body, len(ins) - 1)
    return pl.pallas_call(
        body,
        grid=(B // bb, NC),
        in_specs=specs,
        out_specs=[pl.BlockSpec((R, VG_W), lambda b, c: (row(b, c), 0)), s_out],
        out_shape=[
            jax.ShapeDtypeStruct((B * T, VG_W), F32 if single else BF16),
            jax.ShapeDtypeStruct(s_shape, F32),
        ],
        input_output_aliases=aliases,
        compiler_params=_cparams(("arbitrary", "arbitrary")),
        name="retention",
    )(*ins)


def _rope_tables(pos):
    half = DK_A // 2
    inv = ROPE_BASE ** (-jnp.linspace(0.0, 1.0, half, dtype=F32))
    ang = pos.astype(F32)[:, None] * inv[None, :]
    cos, sin = jnp.cos(ang), jnp.sin(ang)
    return jnp.concatenate([cos, cos], axis=-1), jnp.concatenate([-sin, sin], axis=-1)


def _lru_gates(xc, wa, ba, wx, bx, lam):
    xg = xc.astype(BF16)
    r = jax.nn.sigmoid(_dot(xg, wa) + ba)
    i = jax.nn.sigmoid(_dot(xg, wx) + bx)
    log_a = (-LRU_C) * r * (-_log_sigmoid(lam))
    a = jnp.exp(log_a)
    u = jnp.sqrt(1.0 - a * a) * (i * xc)
    return a, u


def _lru_tile(x_of, gb_of, prm, h0_ref, tail0_ref, y_ref, hout_ref, cout_ref, xe_ref, hc_ref, Tc, between=None):
    cw_ref, cb_ref, wa_ref, ba_ref, wx_ref, bx_ref, lam_ref = prm

    @pl.when(pl.program_id(1) == 0)
    def _():
        xe_ref[0:TAIL, :] = tail0_ref[0]
        hc_ref[...] = h0_ref[0]

    sub = lax.broadcasted_iota(jnp.int32, (TAIL, 1), 0)
    for g in range(NB_B):
        cs = slice(g * BS_B, (g + 1) * BS_B)
        x = x_of(g)
        xe_ref[TAIL:TAIL + Tc, cs] = x
        xc = cb_ref[:, cs]
        for j in range(CONV_B - 1):
            off = TAIL - (CONV_B - 1) + j
            xc = xc + cw_ref[j:j + 1, cs] * xe_ref[off:off + Tc, cs]
        xc = xc + cw_ref[CONV_B - 1:CONV_B, cs] * x
        a, u = _lru_gates(xc, wa_ref[g], ba_ref[:, cs], wx_ref[g], bx_ref[:, cs], lam_ref[:, cs])
        h_prev = hc_ref[:, cs]
        tiles = []
        for t in range(Tc // TAIL):
            a8, u8 = a[t * TAIL:(t + 1) * TAIL], u[t * TAIL:(t + 1) * TAIL]
            d = 1
            while d < TAIL:
                valid = sub >= d
                u8 = jnp.where(valid, a8 * pltpu.roll(u8, d, 0) + u8, u8)
                a8 = jnp.where(valid, a8 * pltpu.roll(a8, d, 0), a8)
                d *= 2
            h8 = a8 * h_prev + u8
            h_prev = h8[TAIL - 1:TAIL]
            tiles.append(h8)
        hs = jnp.concatenate(tiles, axis=0)
        y_ref[:, cs] = (hs * jax.nn.gelu(gb_of(g))).astype(BF16)
        hc_ref[:, cs] = hs[Tc - 1:Tc]
        hout_ref[0, :, cs] = hs[Tc - 1:Tc]
        cout_ref[0, :, cs] = xe_ref[Tc + TAIL - (CONV_B - 1):Tc + TAIL, cs]
        xe_ref[0:TAIL, cs] = xe_ref[Tc:Tc + TAIL, cs]
        if between is not None:
            between(g)


def _lru_seq_body(xb_ref, gb_ref, cw_ref, cb_ref, wa_ref, ba_ref, wx_ref, bx_ref, lam_ref, h0_ref, tail0_ref,
                  y_ref, hout_ref, cout_ref, xe_ref, hc_ref, *, Tc):
    blk = lambda ref: (lambda g: ref[:, g * BS_B:(g + 1) * BS_B])
    _lru_tile(blk(xb_ref), blk(gb_ref), (cw_ref, cb_ref, wa_ref, ba_ref, wx_ref, bx_ref, lam_ref),
              h0_ref, tail0_ref, y_ref, hout_ref, cout_ref, xe_ref, hc_ref, Tc)


QKVG_W = 2 * QK_W + 2 * VG_W


def _even_body(x_ref, g_ref, w_ref, cw_ref, cb_ref, wa_ref, ba_ref, wx_ref, bx_ref, lam_ref, h0_ref, tail0_ref,
               cos_ref, sin_ref, s0_ref, gn_ref, woa_ref, wob_ref, o_ref, hout_ref, cout_ref, s_ref,
               xn_ref, zq_ref, ya_ref, yb_ref, xe_ref, hc_ref, *, Tc, tn, C):
    @pl.when(pl.program_id(1) == 0)
    def _():
        s_ref[...] = s0_ref[...]

    xn_ref[...] = _rms(x_ref[...], g_ref[...]).astype(BF16)
    pairs = {}

    def col(start):
        def block(g):
            key = (start, g // 2)
            if key not in pairs:
                lo = start + (g // 2) * 2 * BS_B
                pairs[key] = _dot(xn_ref[...], w_ref[:, lo:lo + 2 * BS_B])
            return pairs[key][:, (g % 2) * BS_B:(g % 2 + 1) * BS_B]
        return block

    def qkvg_chunk(c):
        if c < QKVG_W // tn:
            zq_ref[:, c * tn:(c + 1) * tn] = _dot(xn_ref[...], w_ref[:, c * tn:(c + 1) * tn])

    _lru_tile(col(QKVG_W), col(QKVG_W + W_B), (cw_ref, cb_ref, wa_ref, ba_ref, wx_ref, bx_ref, lam_ref),
              h0_ref, tail0_ref, yb_ref, hout_ref, cout_ref, xe_ref, hc_ref, Tc, between=qkvg_chunk)
    o_ref[...] = x_ref[...] + _dot(yb_ref[...], wob_ref[...])
    chunks = [(slice(c * C, (c + 1) * C), 0, slice(c * C, (c + 1) * C)) for c in range(Tc // C)]
    _ret_heads(zq_ref, zq_ref, zq_ref, zq_ref, (0, QK_W, 2 * QK_W, 2 * QK_W + VG_W), chunks, cos_ref, sin_ref,
               s_ref, gn_ref, ya_ref, C=C, c_true=C, single=False)
    o_ref[...] += _dot(ya_ref[...], woa_ref[...])


def _even_fused(x, g, w_in, p, h0, conv0, cosf, sinf, s0, gn, w_out_a, w_out_b, *, B, T, Tc, tn, C):
    NC = T // Tc
    shared3 = lambda b, c: (0, 0, 0)
    row = lambda b, c: (b * NC + c, 0)
    vec = pl.BlockSpec((1, W_B), lambda b, c: (0, 0))
    wspec = pl.BlockSpec((NB_B, BS_B, BS_B), lambda b, c: (0, 0, 0))
    const = lambda a: pl.BlockSpec(a.shape, lambda b, c: (0, 0), pipeline_mode=pl.Buffered(1))
    tile = pl.BlockSpec((Tc, D_MODEL), row)
    return pl.pallas_call(
        functools.partial(_even_body, Tc=Tc, tn=tn, C=C),
        grid=(B, NC),
        in_specs=[
            tile, pl.BlockSpec((1, D_MODEL), lambda b, c: (0, 0)), const(w_in),
            pl.BlockSpec((CONV_B, W_B), lambda b, c: (0, 0)), vec, wspec, vec, wspec, vec, vec,
            pl.BlockSpec((1, 1, W_B), shared3), pl.BlockSpec((1, TAIL, W_B), shared3),
            pl.BlockSpec((Tc, DK_A), lambda b, c: (c, 0)), pl.BlockSpec((Tc, DK_A), lambda b, c: (c, 0)),
            pl.BlockSpec((1, H_A, DK_A, DV_A), lambda b, c: (0, 0, 0, 0)),
            pl.BlockSpec((1, VG_W), lambda b, c: (0, 0)), const(w_out_a), const(w_out_b),
        ],
        out_specs=[
            tile,
            pl.BlockSpec((1, 1, W_B), lambda b, c: (b, 0, 0)),
            pl.BlockSpec((1, CONV_B - 1, W_B), lambda b, c: (b, 0, 0)),
            pl.BlockSpec((1, H_A, DK_A, DV_A), lambda b, c: (b, 0, 0, 0)),
        ],
        out_shape=[
            jax.ShapeDtypeStruct((B * T, D_MODEL), F32),
            jax.ShapeDtypeStruct((B, 1, W_B), F32),
            jax.ShapeDtypeStruct((B, CONV_B - 1, W_B), F32),
            jax.ShapeDtypeStruct((B, H_A, DK_A, DV_A), F32),
        ],
        scratch_shapes=[pltpu.VMEM((Tc, D_MODEL), BF16), pltpu.VMEM((Tc, QKVG_W), F32),
                        pltpu.VMEM((Tc, VG_W), BF16), pltpu.VMEM((Tc, W_B), BF16),
                        pltpu.VMEM((Tc + TAIL, W_B), F32), pltpu.VMEM((1, W_B), F32)],
        compiler_params=_cparams(("arbitrary", "arbitrary")),
        name="even_fused",
    )(x, g.reshape(1, -1), w_in, p["conv_w"], p["conv_b"], p["wa"], p["ba"], p["wx"], p["bx"], p["lam"], h0,
      _tail_rows(conv0), cosf, sinf, s0, gn.reshape(1, -1), w_out_a, w_out_b)


def _tail_rows(state):
    return jnp.pad(state, ((0, 0), (TAIL - state.shape[1], 0), (0, 0)))


def _lru_seq(z, p, h0, conv0, *, B, T, Tc, shared):
    NC = T // Tc
    smap = (lambda b, c: (0, 0, 0)) if shared else (lambda b, c: (b, 0, 0))
    row = lambda b, c: b * NC + c
    vec = lambda a: pl.BlockSpec((1, W_B), lambda b, c: (0, 0))
    wspec = pl.BlockSpec((NB_B, BS_B, BS_B), lambda b, c: (0, 0, 0))
    return pl.pallas_call(
        functools.partial(_lru_seq_body, Tc=Tc),
        grid=(B, NC),
        in_specs=[
            pl.BlockSpec((Tc, W_B), lambda b, c: (row(b, c), 3)),
            pl.BlockSpec((Tc, W_B), lambda b, c: (row(b, c), 4)),
            pl.BlockSpec((CONV_B, W_B), lambda b, c: (0, 0)), vec(0),
            wspec, vec(0), wspec, vec(0), vec(0),
            pl.BlockSpec((1, 1, W_B), smap),
            pl.BlockSpec((1, TAIL, W_B), smap),
        ],
        out_specs=[
            pl.BlockSpec((Tc, W_B), lambda b, c: (row(b, c), 0)),
            pl.BlockSpec((1, 1, W_B), lambda b, c: (b, 0, 0)),
            pl.BlockSpec((1, CONV_B - 1, W_B), lambda b, c: (b, 0, 0)),
        ],
        out_shape=[
            jax.ShapeDtypeStruct((B * T, W_B), BF16),
            jax.ShapeDtypeStruct((B, 1, W_B), F32),
            jax.ShapeDtypeStruct((B, CONV_B - 1, W_B), F32),
        ],
        scratch_shapes=[pltpu.VMEM((Tc + TAIL, W_B), F32), pltpu.VMEM((1, W_B), F32)],
        compiler_params=_cparams(("arbitrary", "arbitrary")),
        name="lru_seq",
    )(z, z, p["conv_w"], p["conv_b"], p["wa"], p["ba"], p["wx"], p["bx"], p["lam"], h0, _tail_rows(conv0))


def _lru_dec_body(xb_ref, gb_ref, s0_ref, s1_ref, s2_ref, cw_ref, cb_ref, wa_ref, ba_ref, wx_ref, bx_ref, lam_ref,
                  h0_ref, y_ref, hout_ref):
    for g in range(NB_B):
        cs = slice(g * BS_B, (g + 1) * BS_B)
        xc = (cb_ref[:, cs] + cw_ref[0:1, cs] * s0_ref[:, cs] + cw_ref[1:2, cs] * s1_ref[:, cs]
              + cw_ref[2:3, cs] * s2_ref[:, cs] + cw_ref[3:4, cs] * xb_ref[:, cs])
        a, u = _lru_gates(xc, wa_ref[g], ba_ref[:, cs], wx_ref[g], bx_ref[:, cs], lam_ref[:, cs])
        hs = a * h0_ref[:, cs] + u
        y_ref[:, cs] = (hs * jax.nn.gelu(gb_ref[:, cs])).astype(BF16)
        hout_ref[:, cs] = hs


def _lru_dec(z, p, h0, conv0):
    B = z.shape[0]
    full = lambda shape: pl.BlockSpec(shape, lambda i: (0,) * len(shape))
    return pl.pallas_call(
        _lru_dec_body,
        grid=(1,),
        in_specs=[
            pl.BlockSpec((B, W_B), lambda i: (0, 3)), pl.BlockSpec((B, W_B), lambda i: (0, 4)),
            full((B, W_B)), full((B, W_B)), full((B, W_B)),
            full((CONV_B, W_B)), full((1, W_B)),
            full((NB_B, BS_B, BS_B)), full((1, W_B)), full((NB_B, BS_B, BS_B)), full((1, W_B)), full((1, W_B)),
            full((B, W_B)),
        ],
        out_specs=[full((B, W_B)), full((B, W_B))],
        out_shape=[jax.ShapeDtypeStruct((B, W_B), BF16), jax.ShapeDtypeStruct((B, W_B), F32)],
        compiler_params=_cparams(("arbitrary",)),
        name="lru_dec",
    )(z, z, conv0[:, 0], conv0[:, 1], conv0[:, 2], p["conv_w"], p["conv_b"], p["wa"], p["ba"], p["wx"], p["bx"],
      p["lam"], h0)


def _ffn_seq_body(ug_ref, uv_ref, cw_ref, cb_ref, tail0_ref, m_ref, cout_ref, xe_ref, *, Tc):
    @pl.when(pl.program_id(1) == 0)
    def _():
        xe_ref[0:TAIL, :] = tail0_ref[0]

    x = ug_ref[...]
    xe_ref[TAIL:TAIL + Tc, :] = x
    cw = cw_ref[...]
    c = cb_ref[...]
    for j in range(CONV_F - 1):
        off = TAIL - (CONV_F - 1) + j
        c = c + cw[j:j + 1] * xe_ref[off:off + Tc, :]
    c = c + cw[CONV_F - 1:CONV_F] * x
    m_ref[...] = (jax.nn.gelu(c) * uv_ref[...]).astype(BF16)
    cout_ref[0] = xe_ref[Tc + TAIL - (CONV_F - 1):Tc + TAIL, :]
    xe_ref[0:TAIL, :] = xe_ref[Tc:Tc + TAIL, :]


def _ffn_seq(u, cw, cb, conv0, *, B, T, Tc, shared):
    NC = T // Tc
    smap = (lambda b, c: (0, 0, 0)) if shared else (lambda b, c: (b, 0, 0))
    row = lambda b, c: b * NC + c
    return pl.pallas_call(
        functools.partial(_ffn_seq_body, Tc=Tc),
        grid=(B, NC),
        in_specs=[
            pl.BlockSpec((Tc, D_FF), lambda b, c: (row(b, c), 0)),
            pl.BlockSpec((Tc, D_FF), lambda b, c: (row(b, c), 1)),
            pl.BlockSpec((CONV_F, D_FF), lambda b, c: (0, 0)),
            pl.BlockSpec((1, D_FF), lambda b, c: (0, 0)),
            pl.BlockSpec((1, TAIL, D_FF), smap),
        ],
        out_specs=[
            pl.BlockSpec((Tc, D_FF), lambda b, c: (row(b, c), 0)),
            pl.BlockSpec((1, CONV_F - 1, D_FF), lambda b, c: (b, 0, 0)),
        ],
        out_shape=[
            jax.ShapeDtypeStruct((B * T, D_FF), BF16),
            jax.ShapeDtypeStruct((B, CONV_F - 1, D_FF), F32),
        ],
        scratch_shapes=[pltpu.VMEM((Tc + TAIL, D_FF), F32)],
        compiler_params=_cparams(("arbitrary", "arbitrary")),
        name="ffn_seq",
    )(u, u, cw, cb, _tail_rows(conv0))


def _ffn_dec_body(ug_ref, uv_ref, s0_ref, s1_ref, cw_ref, cb_ref, m_ref):
    cw = cw_ref[...]
    c = cb_ref[...] + cw[0:1] * s0_ref[...] + cw[1:2] * s1_ref[...] + cw[2:3] * ug_ref[...]
    m_ref[...] = (jax.nn.gelu(c) * uv_ref[...]).astype(BF16)


def _ffn_dec(u, cw, cb, conv0):
    B = u.shape[0]
    full = lambda shape: pl.BlockSpec(shape, lambda i: (0,) * len(shape))
    return pl.pallas_call(
        _ffn_dec_body,
        grid=(1,),
        in_specs=[pl.BlockSpec((B, D_FF), lambda i: (0, 0)), pl.BlockSpec((B, D_FF), lambda i: (0, 1)),
                  full((B, D_FF)), full((B, D_FF)), full((CONV_F, D_FF)), full((1, D_FF))],
        out_specs=full((B, D_FF)),
        out_shape=jax.ShapeDtypeStruct((B, D_FF), BF16),
        compiler_params=_cparams(("arbitrary",)),
        name="ffn_dec",
    )(u, u, conv0[:, 0], conv0[:, 1], cw, cb)


def _ffn_fused_body(x_ref, g_ref, wup_ref, cw_ref, cb_ref, wdn_ref, tail0_ref, gout_ref, o_ref, cout_ref,
                    xn_ref, m_ref, tail_ref, *, tm, tc, out_norm):
    @pl.when(pl.program_id(1) == 0)
    def _():
        tail_ref[...] = tail0_ref[0]

    x = x_ref[...]
    xn_ref[...] = _rms(x, g_ref[...]).astype(BF16)
    row = lax.broadcasted_iota(jnp.int32, (tm, 1), 0)
    for j in range(D_FF // tc):
        cols = slice(j * tc, (j + 1) * tc)
        ug = _dot(xn_ref[...], wup_ref[:, cols])
        uv = _dot(xn_ref[...], wup_ref[:, D_FF + j * tc:D_FF + (j + 1) * tc])
        t1 = tail_ref[TAIL - 1:TAIL, cols]
        t2 = tail_ref[TAIL - 2:TAIL - 1, cols]
        s1 = jnp.where(row == 0, t1, pltpu.roll(ug, 1, 0))
        s2 = jnp.where(row == 0, t2, jnp.where(row == 1, t1, pltpu.roll(ug, 2, 0)))
        c = cb_ref[:, cols] + cw_ref[0:1, cols] * s2 + cw_ref[1:2, cols] * s1 + cw_ref[2:3, cols] * ug
        m_ref[:, cols] = (jax.nn.gelu(c) * uv).astype(BF16)
        tail_ref[:, cols] = ug[tm - TAIL:tm]
        cout_ref[0, :, cols] = ug[tm - (CONV_F - 1):tm]
    y = x + _dot(m_ref[...], wdn_ref[...])
    o_ref[...] = _rms(y, gout_ref[...]) if out_norm else y


def _ffn_fused(x, g, w_up, cw, cb, w_down, conv0, *, B, T, tm, tc, shared, g_out=None):
    NT = T // tm
    out_norm = g_out is not None
    g_out = g if g_out is None else g_out
    smap = (lambda b, t: (0, 0, 0)) if shared else (lambda b, t: (b, 0, 0))
    tile = pl.BlockSpec((tm, D_MODEL), lambda b, t: (b * NT + t, 0))
    const = lambda a: pl.BlockSpec(a.shape, lambda b, t: (0, 0), pipeline_mode=pl.Buffered(1))
    return pl.pallas_call(
        functools.partial(_ffn_fused_body, tm=tm, tc=tc, out_norm=out_norm),
        grid=(B, NT),
        in_specs=[tile, pl.BlockSpec((1, D_MODEL), lambda b, t: (0, 0)), const(w_up),
                  pl.BlockSpec((CONV_F, D_FF), lambda b, t: (0, 0)), pl.BlockSpec((1, D_FF), lambda b, t: (0, 0)),
                  const(w_down), pl.BlockSpec((1, TAIL, D_FF), smap),
                  pl.BlockSpec((1, D_MODEL), lambda b, t: (0, 0))],
        out_specs=[tile, pl.BlockSpec((1, CONV_F - 1, D_FF), lambda b, t: (b, 0, 0))],
        out_shape=[jax.ShapeDtypeStruct((B * T, D_MODEL), F32),
                   jax.ShapeDtypeStruct((B, CONV_F - 1, D_FF), F32)],
        scratch_shapes=[pltpu.VMEM((tm, D_MODEL), BF16), pltpu.VMEM((tm, D_FF), BF16),
                        pltpu.VMEM((TAIL, D_FF), F32)],
        compiler_params=_cparams(("arbitrary", "arbitrary")),
        name="ffn_fused",
    )(x, g.reshape(1, -1), w_up, cw, cb, w_down, _tail_rows(conv0), g_out.reshape(1, -1))


_PREP_W = ("mix", "w_r", "w_k", "w_v", "w0", "w1", "w2", "a0", "a1", "a2", "g1", "g2")
_PREP_VP = ("v0", "v1", "v2")


def _rwkv_prep_body(*refs, has_vp, carry):
    it = iter(refs)
    x_ref, gn_ref = next(it), next(it)
    p = {n: next(it) for n in _PREP_W}
    if has_vp:
        p.update({n: next(it) for n in _PREP_VP})
        vf_ref = next(it)
    prev_ref = next(it)
    r_ref, lw_ref, k_ref, v_ref, as_ref, g_ref, hn_ref = (next(it) for _ in range(7))
    h = _rms(x_ref[...], gn_ref[...])
    if carry:
        carry_ref = next(it)

        @pl.when(pl.program_id(1) == 0)
        def _():
            carry_ref[...] = prev_ref[0]

        row = lax.broadcasted_iota(jnp.int32, (h.shape[0], 1), 0)
        hprev = jnp.where(row == 0, carry_ref[...], pltpu.roll(h, 1, 0))
        carry_ref[...] = h[h.shape[0] - 1:]
        hn_ref[0] = h[h.shape[0] - 1:]
    else:
        hprev = prev_ref[...]
        hn_ref[...] = h
    xx = hprev - h
    mix = p["mix"][...]
    xs = [(h + xx * mix[n:n + 1]).astype(BF16) for n in range(6)]
    xr, xw, xk, xv, xa, xg = xs
    r_ref[...] = _dot(xr, p["w_r"][...])
    k_ref[...] = _dot(xk, p["w_k"][...])
    v = _dot(xv, p["w_v"][...])
    wl = p["w0"][...] + _dot(jnp.tanh(_dot(xw, p["w1"][...])).astype(BF16), p["w2"][...])
    lw_ref[...] = -jnp.exp(_log_sigmoid(wl) - 0.5)
    if has_vp:
        gate = jax.nn.sigmoid(p["v0"][...] + _dot(_dot(xv, p["v1"][...]).astype(BF16), p["v2"][...]))
        v = v + (vf_ref[...] - v) * gate
    v_ref[...] = v
    as_ref[...] = jax.nn.sigmoid(p["a0"][...] + _dot(_dot(xa, p["a1"][...]).astype(BF16), p["a2"][...]))
    g_ref[...] = _dot(jax.nn.sigmoid(_dot(xg, p["g1"][...])).astype(BF16), p["g2"][...])


def _rwkv_prep(x, gn, p, vp, v_first, prev, *, B, T, tm, carry, shared=False):
    R = x.shape[0]
    has_vp = vp is not None
    if carry:
        NT = T // tm
        grid = (B, NT)
        rowmap = lambda b, t: (b * NT + t, 0)
        cmap2 = lambda b, t: (0, 0)
        pmap = (lambda b, t: (0, 0, 0)) if shared else (lambda b, t: (b, 0, 0))
        prev_spec = pl.BlockSpec((1, 1, D_MODEL), pmap)
        hn_spec = pl.BlockSpec((1, 1, D_MODEL), lambda b, t: (b, 0, 0))
        hn_shape = jax.ShapeDtypeStruct((B, 1, D_MODEL), F32)
        sem = ("arbitrary", "arbitrary")
    else:
        grid = (R // tm,)
        rowmap = lambda i: (i, 0)
        cmap2 = lambda i: (0, 0)
        prev_spec = pl.BlockSpec((tm, D_MODEL), rowmap)
        hn_spec = pl.BlockSpec((tm, D_MODEL), rowmap)
        hn_shape = jax.ShapeDtypeStruct((R, D_MODEL), F32)
        sem = ("arbitrary",)
    tile = pl.BlockSpec((tm, D_MODEL), rowmap)
    ins = [x, gn.reshape(1, -1)]
    specs = [tile, pl.BlockSpec((1, D_MODEL), cmap2)]
    names = _PREP_W + (_PREP_VP if has_vp else ())
    src = dict(p)
    if has_vp:
        src.update(vp)
    for n in names:
        ins.append(src[n])
        specs.append(pl.BlockSpec(src[n].shape, cmap2))
    if has_vp:
        ins.append(v_first)
        specs.append(tile)
    ins.append(prev)
    specs.append(prev_spec)
    outs = pl.pallas_call(
        functools.partial(_rwkv_prep_body, has_vp=has_vp, carry=carry),
        grid=grid,
        in_specs=specs,
        out_specs=[tile] * 6 + [hn_spec],
        out_shape=[jax.ShapeDtypeStruct((R, D_MODEL), F32)] * 6 + [hn_shape],
        scratch_shapes=[pltpu.VMEM((1, D_MODEL), F32)] if carry else [],
        compiler_params=_cparams(sem),
        name="rwkv_prep",
    )(*ins)
    return outs


def _rwkv_chunk_body(r_ref, lw_ref, k_ref, v_ref, as_ref, g_ref, kk_ref, ka_ref, rk_ref, gg_ref, gb_ref, s0_ref,
                     o_ref, s_ref, *, Tc, C, n_pair, unroll):
    @pl.when(pl.program_id(2) == 0)
    def _():
        s_ref[...] = s0_ref[...]

    C2 = 2 * C
    lane = lax.broadcasted_iota(jnp.int32, (1, LANES), 1)
    m0 = (lane < HS_C).astype(F32)
    m1 = 1.0 - m0
    li = lax.broadcasted_iota(jnp.int32, (LANES, LANES), 0)
    lj = lax.broadcasted_iota(jnp.int32, (LANES, LANES), 1)
    same_head = (li >= HS_C) == (lj >= HS_C)
    bd_ones = same_head.astype(F32).astype(BF16)
    bd_avg = (same_head.astype(F32) * (1.0 / HS_C)).astype(BF16)
    ti = lax.broadcasted_iota(jnp.int32, (C, C), 0)
    si = lax.broadcasted_iota(jnp.int32, (C, C), 1)
    tri = (ti >= si).astype(F32).astype(BF16)
    tri2 = jnp.concatenate([tri, tri], axis=1)
    r2 = lax.broadcasted_iota(jnp.int32, (C2, C2), 0)
    c2 = lax.broadcasted_iota(jnp.int32, (C2, C2), 1)
    same = (r2 >= C) == (c2 >= C)
    tt, ss = r2 & (C - 1), c2 & (C - 1)
    m_strict = same & (tt > ss)
    m_incl = same & (tt >= ss)
    same_sub = (tt & -RWKV_SUB) == (ss & -RWKV_SUB)
    m_sub = m_strict & same_sub
    m_off = m_strict & jnp.logical_not(same_sub)
    eye = (r2 == c2).astype(F32)
    nb = C // RWKV_SUB
    stack = lambda x: jnp.concatenate([x * m0, x * m1], axis=0)
    dup = lambda x: jnp.concatenate([x, x], axis=0)

    def sub_chunk(ci, carry):
        row_l = [pl.ds(pl.multiple_of((ci * unroll + u) * C, C), C) for u in range(unroll)]
        rows = [rw for rw in row_l for _ in range(n_pair)]
        lanes = [slice(p * LANES, (p + 1) * LANES) for p in range(n_pair)] * unroll
        each = lambda f, *cols: [f(*xs) for xs in zip(*cols)]

        def load(rw, ls):
            r, lw, k, v = r_ref[rw, ls], lw_ref[rw, ls], k_ref[rw, ls], v_ref[rw, ls]
            asig = as_ref[rw, ls]
            kk = k * kk_ref[:, ls]
            k2 = k * (1.0 + (asig - 1.0) * ka_ref[:, ls])
            return r, lw, k2, v, asig, kk

        r, lw, k2, v, asig, kk = zip(*each(load, rows, lanes))
        ssq = each(lambda x: _dot(_bf(x * x), bd_ones), kk)
        cs = each(lambda x: _dot(tri2, jnp.concatenate(_split(x), axis=0)), lw)
        kk = each(lambda x, q: x * lax.rsqrt(jnp.maximum(q, 1e-24)), kk, ssq)
        b = each(lambda x, q: x * q, kk, asig)
        cend = [x[C - 1:C] for x in cs]
        e_neg = each(lambda x: jnp.exp(-x), cs)
        e_end = each(lambda x, y: jnp.exp(y - x), cs, cend)
        As = each(lambda x, c, l: _split(stack(-x * jnp.exp(c - l))), kk, cs, lw)
        Bd = each(lambda x, e: _split(dup(x * e)), b, e_neg)
        Rs = each(lambda x, c: _bf(stack(x * jnp.exp(c))), r, cs)
        Vs = each(lambda x: _bf(stack(x)), v)
        Kd = each(lambda x, e: _bf(dup(x * e)), k2, e_neg)
        G = each(_dot3_wide_nt, As, Bd)
        Aak = each(lambda x, y: _bf(jnp.where(m_strict, _dot_nt(x[0], y), 0.0)), As, Kd)
        Arb = each(lambda x, y: _bf(jnp.where(m_incl, _dot_nt(x, y[0]), 0.0)), Rs, Bd)
        Ark = each(lambda x, y: _bf(jnp.where(m_incl, _dot_nt(x, y), 0.0)), Rs, Kd)
        Nd = each(lambda x: jnp.where(m_sub, x, 0.0), G)
        P = each(lambda x: eye + x, Nd)
        Qs = each(_split, Nd)
        for _ in range(3):
            Qs = each(lambda q: _split(_dot3_wide(q, q)), Qs)
            P = each(lambda x, q: x + _dot3_wide(_split(x), q), P, Qs)
        if nb > 1:
            Pb = each(_bf, P)
            M = each(lambda x, y: _dot(x, _bf(jnp.where(m_off, y, 0.0))), Pb, G)
            Tm = each(lambda x: eye + x, M)
            for _ in range(int(np.ceil(np.log2(nb))) - 1):
                M = each(lambda x: _dot(_bf(x), _bf(x)), M)
                Tm = each(lambda x, y: x + _dot(_bf(x), _bf(y)), Tm, M)
            Tinv = each(lambda x, y: _bf(_dot(_bf(x), y)), Tm, Pb)
        else:
            Tinv = each(_bf, P)
        bE = each(lambda x, e: _bf(stack(x * e)), b, e_end)
        kE = each(lambda x, e: _bf(stack(x * e)), k2, e_end)
        gC = each(jnp.exp, cend)
        Ys = []
        for u in range(unroll):
            sl = slice(u * n_pair, (u + 1) * n_pair)
            S = [s_ref[0, p] for p in range(n_pair)]
            Sb = each(_bf, S)
            X = each(lambda a_, s_, k_, v_: _bf(_dot_nt(a_[0], s_) + _dot(k_, v_)), As[sl], Sb, Aak[sl], Vs[sl])
            Us = each(lambda t_, x_: _bf(_dot(t_, x_)), Tinv[sl], X)
            Ys += each(lambda r_, s_, b_, u_, k_, v_: _dot_nt(r_, s_) + _dot(b_, u_) + _dot(k_, v_),
                       Rs[sl], Sb, Arb[sl], Us, Ark[sl], Vs[sl])
            Sn = each(lambda s_, g_, u_, b_, v_, k_: s_ * g_ + _dot_tn(u_, b_) + _dot_tn(v_, k_),
                      S, gC[sl], Us, bE[sl], Vs[sl], kE[sl])
            for p in range(n_pair):
                s_ref[0, p] = Sn[p]
        y = [x[:C] + x[C:] for x in Ys]
        mu = each(lambda x: _dot(_bf(x), bd_avg), y)
        d = each(lambda x, m: x - m, y, mu)
        var = each(lambda x: _dot(_bf(x * x), bd_avg), d)
        bonus = each(lambda r_, k_, ls: _dot(_bf(r_ * k_ * rk_ref[:, ls]), bd_ones), r, k2, lanes)
        for i, (rw, ls) in enumerate(zip(rows, lanes)):
            on = d[i] * lax.rsqrt(var[i] + GN_EPS_C) * gg_ref[:, ls] + gb_ref[:, ls]
            o_ref[rw, ls] = ((on + bonus[i] * v[i]) * g_ref[rw, ls]).astype(BF16)
        return carry

    lax.fori_loop(0, Tc // (C * unroll), sub_chunk, 0)


def _rwkv_chunk(arrs, p, s0, *, B, T, Tc, C, ppb, shared, unroll=1):
    NC = T // Tc
    NP = N_PAIR // ppb
    Wb = ppb * LANES
    tile = pl.BlockSpec((Tc, Wb), lambda b, q, c: (b * NC + c, q))
    vec = pl.BlockSpec((1, Wb), lambda b, q, c: (0, q))
    smap = (lambda b, q, c: (0, q, 0, 0)) if shared else (lambda b, q, c: (b, q, 0, 0))
    return pl.pallas_call(
        functools.partial(_rwkv_chunk_body, Tc=Tc, C=C, n_pair=ppb, unroll=unroll),
        grid=(B, NP, NC),
        in_specs=[tile] * 6 + [vec] * 5 + [pl.BlockSpec((1, ppb, LANES, LANES), smap)],
        out_specs=[tile, pl.BlockSpec((1, ppb, LANES, LANES), lambda b, q, c: (b, q, 0, 0))],
        out_shape=[jax.ShapeDtypeStruct((B * T, D_MODEL), BF16),
                   jax.ShapeDtypeStruct((B, N_PAIR, LANES, LANES), F32)],
        compiler_params=_cparams(("arbitrary", "arbitrary", "arbitrary")),
        name="rwkv_chunk",
    )(*arrs, p["k_k"], p["k_a"], p["r_k"], p["gn_g"], p["gn_b"], s0)


def _rwkv_dec_body(r_ref, lw_ref, k_ref, v_ref, as_ref, g_ref, kk_ref, ka_ref, rk_ref, gg_ref, gb_ref, s0_ref,
                   o_ref, s_ref, r_scr, w_scr, k2_scr, v_scr, a_scr, b_scr, y_scr, *, nb):
    n_head = 2 * N_PAIR
    ri = lax.broadcasted_iota(jnp.int32, (HS_C, HS_C), 0)
    ci = lax.broadcasted_iota(jnp.int32, (HS_C, HS_C), 1)
    eye = (ri == ci).astype(F32)
    for h in range(n_head):
        ls = slice(h * HS_C, (h + 1) * HS_C)
        k, asig = k_ref[:, ls], as_ref[:, ls]
        kk = k * kk_ref[:, ls]
        kk = kk * lax.rsqrt(jnp.maximum(jnp.sum(kk * kk, axis=1, keepdims=True), 1e-24))
        a_scr[h] = -kk
        b_scr[h] = kk * asig
        k2_scr[h] = k * (1.0 + (asig - 1.0) * ka_ref[:, ls])
        r_scr[h] = r_ref[:, ls]
        v_scr[h] = v_ref[:, ls]
        w_scr[h] = jnp.exp(lw_ref[:, ls])

    def per_sample(s, carry):
        row = pl.ds(s, 1)
        heads = range(n_head)
        S = [s0_ref[s, h] for h in heads]
        sa = [jnp.sum(S[h] * a_scr[h, row, :], axis=1, keepdims=True) for h in heads]
        v_col = [jnp.sum(eye * v_scr[h, row, :], axis=1, keepdims=True) for h in heads]
        Sn = [S[h] * w_scr[h, row, :] + sa[h] * b_scr[h, row, :] + v_col[h] * k2_scr[h, row, :] for h in heads]
        for h in heads:
            s_ref[s, h] = Sn[h]
        y_col = [jnp.sum(Sn[h] * r_scr[h, row, :], axis=1, keepdims=True) for h in heads]
        for h in heads:
            y_scr[h, row, :] = jnp.sum(eye * y_col[h], axis=0, keepdims=True)
        return carry

    lax.fori_loop(0, nb, per_sample, 0)
    for h in range(n_head):
        ls = slice(h * HS_C, (h + 1) * HS_C)
        y = y_scr[h]
        d = y - jnp.mean(y, axis=1, keepdims=True)
        var = jnp.mean(d * d, axis=1, keepdims=True)
        on = d * lax.rsqrt(var + GN_EPS_C) * gg_ref[:, ls] + gb_ref[:, ls]
        bonus = jnp.sum(r_scr[h] * k2_scr[h] * rk_ref[:, ls], axis=1, keepdims=True) * v_scr[h]
        o_ref[:, ls] = (on + bonus) * g_ref[:, ls]


def _rwkv_dec(arrs, p, s0, *, nb, layer, prev=None):
    B = s0.shape[1]
    n_head = 2 * N_PAIR
    tile = pl.BlockSpec((nb, D_MODEL), lambda i: (i, 0))
    vec = pl.BlockSpec((1, D_MODEL), lambda i: (0, 0))
    sspec = pl.BlockSpec((None, nb, n_head, HS_C, HS_C), lambda i: (layer, i, 0, 0, 0))
    ins = [*arrs, p["k_k"], p["k_a"], p["r_k"], p["gn_g"], p["gn_b"], s0]
    specs = [tile] * 6 + [vec] * 5 + [sspec]
    body = functools.partial(_rwkv_dec_body, nb=nb)
    aliases = {}
    if prev is not None:
        ins.append(prev)
        specs.append(pl.BlockSpec(memory_space=pl.ANY))
        aliases = {len(ins) - 1: 1}
        body = _drop_ref(body, len(ins) - 1)
    return pl.pallas_call(
        body,
        grid=(B // nb,),
        in_specs=specs,
        out_specs=[tile, sspec],
        out_shape=[jax.ShapeDtypeStruct((B, D_MODEL), F32), jax.ShapeDtypeStruct(s0.shape, F32)],
        scratch_shapes=[pltpu.VMEM((n_head, nb, HS_C), F32)] * 7,
        input_output_aliases=aliases,
        compiler_params=_cparams(("arbitrary",)),
        name="rwkv_dec",
    )(*ins)


def _from_blockdiag(s):
    B = s.shape[0]
    return jnp.stack([s[:, :, :HS_C, :HS_C], s[:, :, HS_C:, HS_C:]], axis=2).reshape(B, 2 * N_PAIR, HS_C, HS_C)


MAIN_TM = 2048
MAIN_TN = 512
RET_C = 128
LRU_TC = 256
FFN_TC = 256
FFN_TM = 512
FFN_TCOL = 256
PREP_TM = 256
RWKV_TC = 256
RWKV_C = 64
RWKV_PPB = 8
RWKV_UNROLL = 2
RWKV_DEC_NB = 8
RET_DEC_NB = 8


def kernel(x_prompt, x_sample, state_ret, state_lru, state_lru_conv, state_rwkv, state_shift, state_ffn_conv,
           meta_tokens, norm_mix_g, norm_ffn_g, norm_final_g,
           ev_w_in, ev_ret_gn_g, ev_lru_conv_w, ev_lru_conv_b, ev_lru_wa, ev_lru_ba, ev_lru_wx, ev_lru_bx,
           ev_lru_lambda, ev_w_out,
           od_mix, od_w_r, od_w_k, od_w_v, od_w0, od_w1, od_w2, od_a0, od_a1, od_a2, od_v0, od_v1, od_v2,
           od_g1, od_g2, od_k_k, od_k_a, od_r_k, od_gn_g, od_gn_b, od_w_o,
           ff_w_up, ff_conv_w, ff_conv_b, ff_w_down):
    BP, TP, _ = x_prompt.shape
    BS = x_sample.shape[0]
    NS = BS + N_META
    bf = lambda a: a.astype(BF16)
    row = lambda a: a.reshape(1, -1)

    def lora_in(w):
        return bf(jnp.pad(w, ((0, 0), (0, LORA_PAD - w.shape[1]))))

    def lora_out(w):
        return bf(jnp.pad(w, ((0, LORA_PAD - w.shape[0]), (0, 0))))

    xm = x_prompt.reshape(BP * TP, D_MODEL)
    xs = jnp.concatenate([x_sample.reshape(BS, D_MODEL), meta_tokens.astype(x_prompt.dtype)], axis=0)

    cos_m, sin_m = _rope_tables(N_META + jnp.arange(TP, dtype=jnp.int32))
    cos_t, sin_t = _rope_tables(jnp.arange(N_META, dtype=jnp.int32))
    cos_s, sin_s = _rope_tables(PAST_LEN + jnp.arange(PAD_T, dtype=jnp.int32))

    out = {k: [] for k in ("ret_p", "lru_p", "lconv_p", "rwkv_p", "shift_p", "ffn_p",
                           "ret_s", "lru_s", "lconv_s", "rwkv_s", "shift_s", "ffn_s")}
    v_first_m = v_first_s = None
    ret_s_all = rwkv_s_all = None
    for li in range(4):
        j = li // 2
        if li % 2 == 0:
            w_in = bf(ev_w_in[j])
            w_out_a, w_out_b = bf(ev_w_out[j][:VG_W]), bf(ev_w_out[j][VG_W:])
            gn = ev_ret_gn_g[j]
            lp = dict(conv_w=ev_lru_conv_w[j], conv_b=row(ev_lru_conv_b[j]), wa=bf(ev_lru_wa[j]),
                      ba=row(ev_lru_ba[j]), wx=bf(ev_lru_wx[j]), bx=row(ev_lru_bx[j]), lam=row(ev_lru_lambda[j]))
            zs = _mm([xs], [w_in], g=norm_mix_g[li], tm=NS, tn=MAIN_TN)
            z_smp, z_meta = zs[:BS], zs[BS:]
            ya_t, ret_t = _retention(z_meta, cos_t, sin_t, jnp.zeros((1, H_A, DK_A, DV_A), F32), gn,
                                     B=1, T=N_META, C=N_META, c_true=N_META, shared=False)
            yb_t, lru_t, lconv_t = _lru_seq(z_meta, lp, jnp.zeros((1, 1, W_B), F32),
                                            jnp.zeros((1, CONV_B - 1, W_B), F32), B=1, T=N_META, Tc=N_META,
                                            shared=False)
            ya_s, ret_s_all = _retention(z_smp, cos_s, sin_s, state_ret, gn, B=BS, T=1, C=PAD_T, c_true=1,
                                         shared=False, bb=RET_DEC_NB, single=True, layer=j, prev=ret_s_all)
            yb_s, lru_s = _lru_dec(z_smp, lp, state_lru[j], state_lru_conv[j])
            lconv_s = jnp.concatenate([state_lru_conv[j][:, 1:], z_smp[:, None, 3 * W_B:4 * W_B]], axis=1)
            ya = jnp.concatenate([ya_s, ya_t.astype(F32)], axis=0)
            yb = jnp.concatenate([yb_s, yb_t], axis=0)
            xs = _mm([ya, yb], [w_out_a, w_out_b], res=xs, tm=NS, tn=MAIN_TN)
            xm, lru_m, lconv_m, ret_m = _even_fused(xm, norm_mix_g[li], w_in, lp, lru_t, lconv_t, cos_m, sin_m, ret_t,
                                                    gn, w_out_a, w_out_b, B=BP, T=TP, Tc=LRU_TC, tn=MAIN_TN,
                                                    C=RET_C)
            out["ret_p"].append(ret_m)
            out["lru_p"].append(lru_m[:, 0])
            out["lconv_p"].append(lconv_m)
            out["lru_s"].append(lru_s)
            out["lconv_s"].append(lconv_s)
        else:
            p = dict(mix=od_mix[j], w_r=bf(od_w_r[j]), w_k=bf(od_w_k[j]), w_v=bf(od_w_v[j]), w0=row(od_w0[j]),
                     w1=lora_in(od_w1[j]), w2=lora_out(od_w2[j]), a0=row(od_a0[j]), a1=lora_in(od_a1[j]),
                     a2=lora_out(od_a2[j]), g1=lora_in(od_g1[j]), g2=lora_out(od_g2[j]),
                     k_k=row(od_k_k[j]), k_a=row(od_k_a[j]), r_k=row(od_r_k[j]), gn_g=row(od_gn_g[j]),
                     gn_b=row(od_gn_b[j]))
            vp = None
            if j > 0:
                vp = dict(v0=row(od_v0[j - 1]), v1=lora_in(od_v1[j - 1]), v2=lora_out(od_v2[j - 1]))
            w_o = bf(od_w_o[j])
            gmix = norm_mix_g[li]
            pre_t = _rwkv_prep(xs[BS:], gmix, p, vp, None if vp is None else v_first_s[BS:],
                               jnp.zeros((1, 1, D_MODEL), F32), B=1, T=N_META, tm=N_META, carry=True)
            o_t, rw_t = _rwkv_chunk(pre_t[:6], p, jnp.zeros((1, N_PAIR, LANES, LANES), F32),
                                    B=1, T=N_META, Tc=N_META, C=N_META, ppb=RWKV_PPB, shared=False)
            shift_t = pre_t[6]
            pre_s = _rwkv_prep(xs[:BS], gmix, p, vp, None if vp is None else v_first_s[:BS],
                               state_shift[j], B=BS, T=1, tm=BS, carry=False)
            o_s, rwkv_s_all = _rwkv_dec(pre_s[:6], p, state_rwkv, nb=RWKV_DEC_NB, layer=j, prev=rwkv_s_all)
            if vp is None:
                v_first_s = jnp.concatenate([pre_s[3], pre_t[3]], axis=0)
            o_small = jnp.concatenate([o_s, o_t.astype(F32)], axis=0)
            xs = _mm([o_small], [w_o], res=xs, tm=NS, tn=MAIN_TN)
            pre_m = _rwkv_prep(xm, gmix, p, vp, v_first_m, shift_t, B=BP, T=TP, tm=PREP_TM, carry=True, shared=True)
            if vp is None:
                v_first_m = pre_m[3]
            o_m, rw_m = _rwkv_chunk(pre_m[:6], p, rw_t, B=BP, T=TP, Tc=RWKV_TC, C=RWKV_C, ppb=RWKV_PPB, shared=True,
                                    unroll=RWKV_UNROLL)
            xm = _mm([o_m], [w_o], res=xm, tm=MAIN_TM, tn=MAIN_TN)
            out["rwkv_p"].append(_from_blockdiag(rw_m))
            out["shift_p"].append(pre_m[6][:, 0])
            out["shift_s"].append(pre_s[6])
        w_up, w_down = bf(ff_w_up[li]), bf(ff_w_down[li])
        cw, cb = ff_conv_w[li], row(ff_conv_b[li])
        us = _mm([xs], [w_up], g=norm_ffn_g[li], tm=NS, tn=MAIN_TN)
        m_t, ffn_t = _ffn_seq(us[BS:], cw, cb, jnp.zeros((1, CONV_F - 1, D_FF), F32), B=1, T=N_META, Tc=N_META,
                              shared=False)
        m_s = _ffn_dec(us[:BS], cw, cb, state_ffn_conv[li])
        ffn_s = jnp.concatenate([state_ffn_conv[li][:, 1:], us[:BS, None, :D_FF]], axis=1)
        xs = _mm([jnp.concatenate([m_s, m_t], axis=0)], [w_down], res=xs, tm=NS, tn=MAIN_TN)
        xm, ffn_m = _ffn_fused(xm, norm_ffn_g[li], w_up, cw, cb, w_down, ffn_t, B=BP, T=TP, tm=FFN_TM, tc=FFN_TCOL,
                               shared=True, g_out=norm_final_g if li == 3 else None)
        out["ffn_p"].append(ffn_m)
        out["ffn_s"].append(ffn_s)

    y_prompt = xm.reshape(BP, TP, D_MODEL)
    y_sample = _final_norm(xs, norm_final_g, NS)[:BS].reshape(BS, 1, D_MODEL)
    st = lambda k: jnp.stack(out[k])
    return (y_prompt, y_sample,
            st("ret_p"), st("lru_p"), st("lconv_p"), st("rwkv_p"), st("shift_p"), st("ffn_p"),
            ret_s_all, st("lru_s"), st("lconv_s"), rwkv_s_all, st("shift_s"), st("ffn_s"))
```

```python
import functools

import numpy as np
import jax
import jax.numpy as jnp
from jax import lax
from jax.experimental import pallas as pl
from jax.experimental.pallas import tpu as pltpu

F32 = jnp.float32
BF16 = jnp.bfloat16
HI = lax.Precision.HIGHEST

D_MODEL = 1024
N_META = 16
PAST_LEN = 16384
H_A, DK_A, DV_A = 4, 128, 256
W_B, NB_B, BS_B, CONV_B = 1024, 8, 128, 4
LRU_C = 8.0
HS_C = 64
N_PAIR = D_MODEL // (2 * HS_C)
LORA_PAD = 128
GN_EPS_C = 64e-5
D_FF, CONV_F = 2816, 3
EPS = 1e-6
ROPE_BASE = 10000.0
QK_W, VG_W = H_A * DK_A, H_A * DV_A

LANES = 128
TAIL = 8
VMEM_LIMIT = 52 * 1024 * 1024
PAD_T = 16
RWKV_SUB = 16


def _cparams(sem):
    return pltpu.CompilerParams(dimension_semantics=sem, vmem_limit_bytes=VMEM_LIMIT)


def _dot(a, b, prec=None):
    return jnp.dot(a, b, preferred_element_type=F32, precision=prec)


def _dot_nt(a, b, prec=None):
    return lax.dot_general(a, b, (((1,), (1,)), ((), ())), preferred_element_type=F32, precision=prec)


def _dot_tn(a, b, prec=None):
    return lax.dot_general(a, b, (((0,), (0,)), ((), ())), preferred_element_type=F32, precision=prec)


def _bf(x):
    return x.astype(BF16)


def _split(x):
    hi = x.astype(BF16)
    return hi, (x - hi.astype(F32)).astype(BF16)


def _dot3(a, b, dot):
    return dot(a[0], b[0]) + dot(a[0], b[1]) + dot(a[1], b[0])


def _dot3_wide(a, b):
    n = a[0].shape[1]
    lhs = jnp.concatenate([a[0], a[1]], axis=1)
    rhs = jnp.concatenate([jnp.concatenate([b[0], b[1]], axis=1),
                           jnp.concatenate([b[0], jnp.zeros_like(b[0])], axis=1)], axis=0)
    out = _dot(lhs, rhs)
    return out[:, :n] + out[:, n:]


def _dot3_wide_nt(a, b):
    n = b[0].shape[0]
    lhs = jnp.concatenate([a[0], a[1]], axis=1)
    rhs = jnp.concatenate([jnp.concatenate([b[0], b[0]], axis=1),
                           jnp.concatenate([b[1], jnp.zeros_like(b[1])], axis=1)], axis=0)
    out = _dot_nt(lhs, rhs)
    return out[:, :n] + out[:, n:]


def _rms(x, g):
    return x * lax.rsqrt(jnp.mean(x * x, axis=-1, keepdims=True) + EPS) * g


def _log_sigmoid(x):
    return jnp.minimum(x, 0.0) - jnp.log(1.0 + jnp.exp(-jnp.abs(x)))


def _mm_body(*refs, n_in, norm, res):
    it = iter(refs)
    x_refs = [next(it) for _ in range(n_in)]
    g_ref = next(it) if norm else None
    w_refs = [next(it) for _ in range(n_in)]
    r_ref = next(it) if res else None
    o_ref = next(it)
    if norm:
        xn_ref = next(it)

        @pl.when(pl.program_id(1) == 0)
        def _():
            xn_ref[...] = _rms(x_refs[0][...], g_ref[...]).astype(BF16)

        acc = _dot(xn_ref[...], w_refs[0][...])
    else:
        acc = _dot(x_refs[0][...].astype(BF16), w_refs[0][...])
        for x_ref, w_ref in zip(x_refs[1:], w_refs[1:]):
            acc = acc + _dot(x_ref[...].astype(BF16), w_ref[...])
    if res:
        acc = acc + r_ref[...]
    o_ref[...] = acc.astype(o_ref.dtype)


def _mm(xs, ws, *, g=None, res=None, tm, tn, out_dtype=F32):
    R = xs[0].shape[0]
    N = ws[0].shape[1]
    norm = g is not None
    ins, specs = [], []
    for x in xs:
        ins.append(x)
        specs.append(pl.BlockSpec((tm, x.shape[1]), lambda i, j: (i, 0)))
    if norm:
        ins.append(g.reshape(1, -1))
        specs.append(pl.BlockSpec((1, g.shape[-1]), lambda i, j: (0, 0)))
    for w in ws:
        ins.append(w)
        specs.append(pl.BlockSpec((w.shape[0], tn), lambda i, j: (0, j)))
    if res is not None:
        ins.append(res)
        specs.append(pl.BlockSpec((tm, tn), lambda i, j: (i, j)))
    scratch = [pltpu.VMEM((tm, xs[0].shape[1]), BF16)] if norm else []
    return pl.pallas_call(
        functools.partial(_mm_body, n_in=len(xs), norm=norm, res=res is not None),
        grid=(R // tm, N // tn),
        in_specs=specs,
        out_specs=pl.BlockSpec((tm, tn), lambda i, j: (i, j)),
        out_shape=jax.ShapeDtypeStruct((R, N), out_dtype),
        scratch_shapes=scratch,
        compiler_params=_cparams(("arbitrary", "arbitrary")),
        name="mm",
    )(*ins)


def _final_norm_body(x_ref, g_ref, o_ref):
    o_ref[...] = _rms(x_ref[...], g_ref[...])


def _final_norm(x, g, tm):
    R = x.shape[0]
    return pl.pallas_call(
        _final_norm_body,
        grid=(R // tm,),
        in_specs=[pl.BlockSpec((tm, D_MODEL), lambda i: (i, 0)), pl.BlockSpec((1, D_MODEL), lambda i: (0, 0))],
        out_specs=pl.BlockSpec((tm, D_MODEL), lambda i: (i, 0)),
        out_shape=jax.ShapeDtypeStruct((R, D_MODEL), F32),
        compiler_params=_cparams(("arbitrary",)),
        name="final_norm",
    )(x, g.reshape(1, -1))


def _ret_heads(q_ref, k_ref, v_ref, ga_ref, offs, chunks, cos_ref, sin_ref, s_ref, gn_ref, y_ref, *, C, c_true, single,
               after_chunk=None):
    qo, ko, vo, go = offs
    ti = lax.broadcasted_iota(jnp.int32, (C, C), 0)
    si = lax.broadcasted_iota(jnp.int32, (C, C), 1)
    dif = (ti - si).astype(F32)
    trow = lax.broadcasted_iota(jnp.int32, (C, 1), 0)
    tcol = trow.astype(F32)
    chunk = (lambda x: jnp.where(trow == 0, x, 0.0)) if single else (lambda x: x)
    heads = range(H_A)
    lg = [float(np.log1p(-(2.0 ** (-5.0 - h)))) for h in heads]
    mask = [jnp.where(dif >= 0, jnp.exp(lg[h] * jnp.maximum(dif, 0.0)), 0.0) for h in heads]
    dec_in = [jnp.exp((tcol + 1.0) * lg[h]) for h in heads]
    dec_k = [jnp.exp((c_true - 1.0 - tcol) * lg[h]) for h in heads]
    qc = lambda h: slice(qo + h * DK_A, qo + (h + 1) * DK_A)
    kc = lambda h: slice(ko + h * DK_A, ko + (h + 1) * DK_A)
    vc = lambda h: slice(vo + h * DV_A, vo + (h + 1) * DV_A)
    gc = lambda h: slice(go + h * DV_A, go + (h + 1) * DV_A)
    yc = lambda h: slice(h * DV_A, (h + 1) * DV_A)
    for ci, (rs, sidx, ts) in enumerate(chunks):
        cosf, sinf = cos_ref[ts, :], sin_ref[ts, :]
        rot = lambda x: x * cosf + pltpu.roll(x, DK_A // 2, 1) * sinf
        q = [rot(chunk(q_ref[rs, qc(h)])) for h in heads]
        k = [rot(chunk(k_ref[rs, kc(h)])) * (DK_A ** -0.5) for h in heads]
        vb = [chunk(v_ref[rs, vc(h)]).astype(BF16) for h in heads]
        qb = [x.astype(BF16) for x in q]
        sc = [(_dot_nt(qb[h], k[h].astype(BF16)) * mask[h]).astype(BF16) for h in heads]
        S = [s_ref[sidx, h] for h in heads]
        o = [_dot(sc[h], vb[h]) + _dot(qb[h], S[h].astype(BF16)) * dec_in[h] for h in heads]
        for h in heads:
            s_ref[sidx, h] = float(np.exp(c_true * lg[h])) * S[h] + _dot_tn((k[h] * dec_k[h]).astype(BF16), vb[h])
        d = [x - jnp.mean(x, axis=-1, keepdims=True) for x in o]
        var = [jnp.mean(x * x, axis=-1, keepdims=True) for x in d]
        for h in heads:
            on = d[h] * lax.rsqrt(var[h] + EPS) * gn_ref[:, yc(h)]
            ga = ga_ref[rs, gc(h)]
            if single:
                on = on[0:1]
            y_ref[rs, yc(h)] = (ga * jax.nn.sigmoid(ga) * on).astype(y_ref.dtype)
        if after_chunk is not None:
            after_chunk(ci, rs)


def _ret_body(q_ref, k_ref, v_ref, ga_ref, cos_ref, sin_ref, s0_ref, gn_ref, y_ref, s_ref, *, C, c_true, bb, single):
    @pl.when(pl.program_id(1) == 0)
    def _():
        s_ref[...] = s0_ref[...]

    rows = lambda s: slice(s, s + 1) if single else slice(s * C, (s + 1) * C)
    _ret_heads(q_ref, k_ref, v_ref, ga_ref, (0, 0, 0, 0), [(rows(s), s, slice(None)) for s in range(bb)],
               cos_ref, sin_ref, s_ref, gn_ref, y_ref, C=C, c_true=c_true, single=single)


def _drop_ref(body, i):
    def wrapped(*refs, **kw):
        return body(*refs[:i], *refs[i + 1:], **kw)
    return wrapped


def _retention(z, cosf, sinf, s0, gn, *, B, T, C, c_true, shared, bb=1, single=False, layer=None, prev=None):
    NC = 1 if single else T // C
    assert bb == 1 or (NC == 1 and not shared)
    assert not single or (T == 1 and c_true == 1)
    row = lambda b, c: b * NC + c
    R = bb if single else bb * C
    if layer is None:
        smap = (lambda b, c: (0, 0, 0, 0)) if shared else (lambda b, c: (b, 0, 0, 0))
        s_in = pl.BlockSpec((bb, H_A, DK_A, DV_A), smap)
        s_out = pl.BlockSpec((bb, H_A, DK_A, DV_A), lambda b, c: (b, 0, 0, 0))
        s_shape = (B, H_A, DK_A, DV_A)
    else:
        s_in = s_out = pl.BlockSpec((None, bb, H_A, DK_A, DV_A), lambda b, c: (layer, b, 0, 0, 0))
        s_shape = s0.shape
    ins = [z, z, z, z, cosf, sinf, s0, gn.reshape(1, -1)]
    specs = [
        pl.BlockSpec((R, QK_W), lambda b, c: (row(b, c), 0)),
        pl.BlockSpec((R, QK_W), lambda b, c: (row(b, c), 1)),
        pl.BlockSpec((R, VG_W), lambda b, c: (row(b, c), 1)),
        pl.BlockSpec((R, VG_W), lambda b, c: (row(b, c), 2)),
        pl.BlockSpec((C, DK_A), lambda b, c: (c, 0)),
        pl.BlockSpec((C, DK_A), lambda b, c: (c, 0)),
        s_in,
        pl.BlockSpec((1, VG_W), lambda b, c: (0, 0)),
    ]
    body = functools.partial(_ret_body, C=C, c_true=c_true, bb=bb, single=single)
    aliases = {}
    if prev is not None:
        ins.append(prev)
        specs.append(pl.BlockSpec(memory_space=pl.ANY))
        aliases = {len(ins) - 1: 1}
        body = _drop_ref(body, len(ins) - 1)
    return pl.pallas_call(
        body,
        grid=(B // bb, NC),
        in_specs=specs,
        out_specs=[pl.BlockSpec((R, VG_W), lambda b, c: (row(b, c), 0)), s_out],
        out_shape=[
            jax.ShapeDtypeStruct((B * T, VG_W), F32 if single else BF16),
            jax.ShapeDtypeStruct(s_shape, F32),
        ],
        input_output_aliases=aliases,
        compiler_params=_cparams(("arbitrary", "arbitrary")),
        name="retention",
    )(*ins)


def _rope_tables(pos):
    half = DK_A // 2
    inv = ROPE_BASE ** (-jnp.linspace(0.0, 1.0, half, dtype=F32))
    ang = pos.astype(F32)[:, None] * inv[None, :]
    cos, sin = jnp.cos(ang), jnp.sin(ang)
    return jnp.concatenate([cos, cos], axis=-1), jnp.concatenate([-sin, sin], axis=-1)


def _lru_gates(xc, wa, ba, wx, bx, lam):
    xg = xc.astype(BF16)
    r = jax.nn.sigmoid(_dot(xg, wa) + ba)
    i = jax.nn.sigmoid(_dot(xg, wx) + bx)
    log_a = (-LRU_C) * r * (-_log_sigmoid(lam))
    a = jnp.exp(log_a)
    u = jnp.sqrt(1.0 - a * a) * (i * xc)
    return a, u


def _lru_tile(x_of, gb_of, prm, h0_ref, tail0_ref, y_ref, hout_ref, cout_ref, xe_ref, hc_ref, Tc, between=None):
    cw_ref, cb_ref, wa_ref, ba_ref, wx_ref, bx_ref, lam_ref = prm

    @pl.when(pl.program_id(1) == 0)
    def _():
        xe_ref[0:TAIL, :] = tail0_ref[0]
        hc_ref[...] = h0_ref[0]

    sub = lax.broadcasted_iota(jnp.int32, (TAIL, 1), 0)
    for g in range(NB_B):
        cs = slice(g * BS_B, (g + 1) * BS_B)
        x = x_of(g)
        xe_ref[TAIL:TAIL + Tc, cs] = x
        xc = cb_ref[:, cs]
        for j in range(CONV_B - 1):
            off = TAIL - (CONV_B - 1) + j
            xc = xc + cw_ref[j:j + 1, cs] * xe_ref[off:off + Tc, cs]
        xc = xc + cw_ref[CONV_B - 1:CONV_B, cs] * x
        a, u = _lru_gates(xc, wa_ref[g], ba_ref[:, cs], wx_ref[g], bx_ref[:, cs], lam_ref[:, cs])
        h_prev = hc_ref[:, cs]
        tiles = []
        for t in range(Tc // TAIL):
            a8, u8 = a[t * TAIL:(t + 1) * TAIL], u[t * TAIL:(t + 1) * TAIL]
            d = 1
            while d < TAIL:
                valid = sub >= d
                u8 = jnp.where(valid, a8 * pltpu.roll(u8, d, 0) + u8, u8)
                a8 = jnp.where(valid, a8 * pltpu.roll(a8, d, 0), a8)
                d *= 2
            h8 = a8 * h_prev + u8
            h_prev = h8[TAIL - 1:TAIL]
            tiles.append(h8)
        hs = jnp.concatenate(tiles, axis=0)
        y_ref[:, cs] = (hs * jax.nn.gelu(gb_of(g))).astype(BF16)
        hc_ref[:, cs] = hs[Tc - 1:Tc]
        hout_ref[0, :, cs] = hs[Tc - 1:Tc]
        cout_ref[0, :, cs] = xe_ref[Tc + TAIL - (CONV_B - 1):Tc + TAIL, cs]
        xe_ref[0:TAIL, cs] = xe_ref[Tc:Tc + TAIL, cs]
        if between is not None:
            between(g)


def _lru_seq_body(xb_ref, gb_ref, cw_ref, cb_ref, wa_ref, ba_ref, wx_ref, bx_ref, lam_ref, h0_ref, tail0_ref,
                  y_ref, hout_ref, cout_ref, xe_ref, hc_ref, *, Tc):
    blk = lambda ref: (lambda g: ref[:, g * BS_B:(g + 1) * BS_B])
    _lru_tile(blk(xb_ref), blk(gb_ref), (cw_ref, cb_ref, wa_ref, ba_ref, wx_ref, bx_ref, lam_ref),
              h0_ref, tail0_ref, y_ref, hout_ref, cout_ref, xe_ref, hc_ref, Tc)


QKVG_W = 2 * QK_W + 2 * VG_W


def _even_body(x_ref, g_ref, w_ref, cw_ref, cb_ref, wa_ref, ba_ref, wx_ref, bx_ref, lam_ref, h0_ref, tail0_ref,
               cos_ref, sin_ref, s0_ref, gn_ref, woa_ref, wob_ref, o_ref, hout_ref, cout_ref, s_ref,
               xn_ref, zq_ref, ya_ref, yb_ref, xe_ref, hc_ref, *, Tc, tn, C):
    @pl.when(pl.program_id(1) == 0)
    def _():
        s_ref[...] = s0_ref[...]

    xn_ref[...] = _rms(x_ref[...], g_ref[...]).astype(BF16)
    pairs = {}

    def col(start):
        def block(g):
            key = (start, g // 2)
            if key not in pairs:
                lo = start + (g // 2) * 2 * BS_B
                pairs[key] = _dot(xn_ref[...], w_ref[:, lo:lo + 2 * BS_B])
            return pairs[key][:, (g % 2) * BS_B:(g % 2 + 1) * BS_B]
        return block

    def qkvg_chunk(c):
        if c < QKVG_W // tn:
            zq_ref[:, c * tn:(c + 1) * tn] = _dot(xn_ref[...], w_ref[:, c * tn:(c + 1) * tn])

    _lru_tile(col(QKVG_W), col(QKVG_W + W_B), (cw_ref, cb_ref, wa_ref, ba_ref, wx_ref, bx_ref, lam_ref),
              h0_ref, tail0_ref, yb_ref, hout_ref, cout_ref, xe_ref, hc_ref, Tc, between=qkvg_chunk)
    o_ref[...] = x_ref[...] + _dot(yb_ref[...], wob_ref[...])
    chunks = [(slice(c * C, (c + 1) * C), 0, slice(c * C, (c + 1) * C)) for c in range(Tc // C)]
    def project_rows(ci, rs):
        o_ref[rs, :] += _dot(ya_ref[rs, :], woa_ref[...])

    _ret_heads(zq_ref, zq_ref, zq_ref, zq_ref, (0, QK_W, 2 * QK_W, 2 * QK_W + VG_W), chunks, cos_ref, sin_ref,
               s_ref, gn_ref, ya_ref, C=C, c_true=C, single=False, after_chunk=project_rows)


def _even_fused(x, g, w_in, p, h0, conv0, cosf, sinf, s0, gn, w_out_a, w_out_b, *, B, T, Tc, tn, C):
    NC = T // Tc
    shared3 = lambda b, c: (0, 0, 0)
    row = lambda b, c: (b * NC + c, 0)
    vec = pl.BlockSpec((1, W_B), lambda b, c: (0, 0))
    wspec = pl.BlockSpec((NB_B, BS_B, BS_B), lambda b, c: (0, 0, 0))
    const = lambda a: pl.BlockSpec(a.shape, lambda b, c: (0, 0), pipeline_mode=pl.Buffered(1))
    tile = pl.BlockSpec((Tc, D_MODEL), row)
    return pl.pallas_call(
        functools.partial(_even_body, Tc=Tc, tn=tn, C=C),
        grid=(B, NC),
        in_specs=[
            tile, pl.BlockSpec((1, D_MODEL), lambda b, c: (0, 0)), const(w_in),
            pl.BlockSpec((CONV_B, W_B), lambda b, c: (0, 0)), vec, wspec, vec, wspec, vec, vec,
            pl.BlockSpec((1, 1, W_B), shared3), pl.BlockSpec((1, TAIL, W_B), shared3),
            pl.BlockSpec((Tc, DK_A), lambda b, c: (c, 0)), pl.BlockSpec((Tc, DK_A), lambda b, c: (c, 0)),
            pl.BlockSpec((1, H_A, DK_A, DV_A), lambda b, c: (0, 0, 0, 0)),
            pl.BlockSpec((1, VG_W), lambda b, c: (0, 0)), const(w_out_a), const(w_out_b),
        ],
        out_specs=[
            tile,
            pl.BlockSpec((1, 1, W_B), lambda b, c: (b, 0, 0)),
            pl.BlockSpec((1, CONV_B - 1, W_B), lambda b, c: (b, 0, 0)),
            pl.BlockSpec((1, H_A, DK_A, DV_A), lambda b, c: (b, 0, 0, 0)),
        ],
        out_shape=[
            jax.ShapeDtypeStruct((B * T, D_MODEL), F32),
            jax.ShapeDtypeStruct((B, 1, W_B), F32),
            jax.ShapeDtypeStruct((B, CONV_B - 1, W_B), F32),
            jax.ShapeDtypeStruct((B, H_A, DK_A, DV_A), F32),
        ],
        scratch_shapes=[pltpu.VMEM((Tc, D_MODEL), BF16), pltpu.VMEM((Tc, QKVG_W), F32),
                        pltpu.VMEM((Tc, VG_W), BF16), pltpu.VMEM((Tc, W_B), BF16),
                        pltpu.VMEM((Tc + TAIL, W_B), F32), pltpu.VMEM((1, W_B), F32)],
        compiler_params=_cparams(("arbitrary", "arbitrary")),
        name="even_fused",
    )(x, g.reshape(1, -1), w_in, p["conv_w"], p["conv_b"], p["wa"], p["ba"], p["wx"], p["bx"], p["lam"], h0,
      _tail_rows(conv0), cosf, sinf, s0, gn.reshape(1, -1), w_out_a, w_out_b)


def _tail_rows(state):
    return jnp.pad(state, ((0, 0), (TAIL - state.shape[1], 0), (0, 0)))


def _lru_seq(z, p, h0, conv0, *, B, T, Tc, shared):
    NC = T // Tc
    smap = (lambda b, c: (0, 0, 0)) if shared else (lambda b, c: (b, 0, 0))
    row = lambda b, c: b * NC + c
    vec = lambda a: pl.BlockSpec((1, W_B), lambda b, c: (0, 0))
    wspec = pl.BlockSpec((NB_B, BS_B, BS_B), lambda b, c: (0, 0, 0))
    return pl.pallas_call(
        functools.partial(_lru_seq_body, Tc=Tc),
        grid=(B, NC),
        in_specs=[
            pl.BlockSpec((Tc, W_B), lambda b, c: (row(b, c), 3)),
            pl.BlockSpec((Tc, W_B), lambda b, c: (row(b, c), 4)),
            pl.BlockSpec((CONV_B, W_B), lambda b, c: (0, 0)), vec(0),
            wspec, vec(0), wspec, vec(0), vec(0),
            pl.BlockSpec((1, 1, W_B), smap),
            pl.BlockSpec((1, TAIL, W_B), smap),
        ],
        out_specs=[
            pl.BlockSpec((Tc, W_B), lambda b, c: (row(b, c), 0)),
            pl.BlockSpec((1, 1, W_B), lambda b, c: (b, 0, 0)),
            pl.BlockSpec((1, CONV_B - 1, W_B), lambda b, c: (b, 0, 0)),
        ],
        out_shape=[
            jax.ShapeDtypeStruct((B * T, W_B), BF16),
            jax.ShapeDtypeStruct((B, 1, W_B), F32),
            jax.ShapeDtypeStruct((B, CONV_B - 1, W_B), F32),
        ],
        scratch_shapes=[pltpu.VMEM((Tc + TAIL, W_B), F32), pltpu.VMEM((1, W_B), F32)],
        compiler_params=_cparams(("arbitrary", "arbitrary")),
        name="lru_seq",
    )(z, z, p["conv_w"], p["conv_b"], p["wa"], p["ba"], p["wx"], p["bx"], p["lam"], h0, _tail_rows(conv0))


def _lru_dec_body(xb_ref, gb_ref, s0_ref, s1_ref, s2_ref, cw_ref, cb_ref, wa_ref, ba_ref, wx_ref, bx_ref, lam_ref,
                  h0_ref, y_ref, hout_ref):
    for g in range(NB_B):
        cs = slice(g * BS_B, (g + 1) * BS_B)
        xc = (cb_ref[:, cs] + cw_ref[0:1, cs] * s0_ref[:, cs] + cw_ref[1:2, cs] * s1_ref[:, cs]
              + cw_ref[2:3, cs] * s2_ref[:, cs] + cw_ref[3:4, cs] * xb_ref[:, cs])
        a, u = _lru_gates(xc, wa_ref[g], ba_ref[:, cs], wx_ref[g], bx_ref[:, cs], lam_ref[:, cs])
        hs = a * h0_ref[:, cs] + u
        y_ref[:, cs] = (hs * jax.nn.gelu(gb_ref[:, cs])).astype(BF16)
        hout_ref[:, cs] = hs


def _lru_dec(z, p, h0, conv0):
    B = z.shape[0]
    full = lambda shape: pl.BlockSpec(shape, lambda i: (0,) * len(shape))
    return pl.pallas_call(
        _lru_dec_body,
        grid=(1,),
        in_specs=[
            pl.BlockSpec((B, W_B), lambda i: (0, 3)), pl.BlockSpec((B, W_B), lambda i: (0, 4)),
            full((B, W_B)), full((B, W_B)), full((B, W_B)),
            full((CONV_B, W_B)), full((1, W_B)),
            full((NB_B, BS_B, BS_B)), full((1, W_B)), full((NB_B, BS_B, BS_B)), full((1, W_B)), full((1, W_B)),
            full((B, W_B)),
        ],
        out_specs=[full((B, W_B)), full((B, W_B))],
        out_shape=[jax.ShapeDtypeStruct((B, W_B), BF16), jax.ShapeDtypeStruct((B, W_B), F32)],
        compiler_params=_cparams(("arbitrary",)),
        name="lru_dec",
    )(z, z, conv0[:, 0], conv0[:, 1], conv0[:, 2], p["conv_w"], p["conv_b"], p["wa"], p["ba"], p["wx"], p["bx"],
      p["lam"], h0)


def _ffn_seq_body(ug_ref, uv_ref, cw_ref, cb_ref, tail0_ref, m_ref, cout_ref, xe_ref, *, Tc):
    @pl.when(pl.program_id(1) == 0)
    def _():
        xe_ref[0:TAIL, :] = tail0_ref[0]

    x = ug_ref[...]
    xe_ref[TAIL:TAIL + Tc, :] = x
    cw = cw_ref[...]
    c = cb_ref[...]
    for j in range(CONV_F - 1):
        off = TAIL - (CONV_F - 1) + j
        c = c + cw[j:j + 1] * xe_ref[off:off + Tc, :]
    c = c + cw[CONV_F - 1:CONV_F] * x
    m_ref[...] = (jax.nn.gelu(c) * uv_ref[...]).astype(BF16)
    cout_ref[0] = xe_ref[Tc + TAIL - (CONV_F - 1):Tc + TAIL, :]
    xe_ref[0:TAIL, :] = xe_ref[Tc:Tc + TAIL, :]


def _ffn_seq(u, cw, cb, conv0, *, B, T, Tc, shared):
    NC = T // Tc
    smap = (lambda b, c: (0, 0, 0)) if shared else (lambda b, c: (b, 0, 0))
    row = lambda b, c: b * NC + c
    return pl.pallas_call(
        functools.partial(_ffn_seq_body, Tc=Tc),
        grid=(B, NC),
        in_specs=[
            pl.BlockSpec((Tc, D_FF), lambda b, c: (row(b, c), 0)),
            pl.BlockSpec((Tc, D_FF), lambda b, c: (row(b, c), 1)),
            pl.BlockSpec((CONV_F, D_FF), lambda b, c: (0, 0)),
            pl.BlockSpec((1, D_FF), lambda b, c: (0, 0)),
            pl.BlockSpec((1, TAIL, D_FF), smap),
        ],
        out_specs=[
            pl.BlockSpec((Tc, D_FF), lambda b, c: (row(b, c), 0)),
            pl.BlockSpec((1, CONV_F - 1, D_FF), lambda b, c: (b, 0, 0)),
        ],
        out_shape=[
            jax.ShapeDtypeStruct((B * T, D_FF), BF16),
            jax.ShapeDtypeStruct((B, CONV_F - 1, D_FF), F32),
        ],
        scratch_shapes=[pltpu.VMEM((Tc + TAIL, D_FF), F32)],
        compiler_params=_cparams(("arbitrary", "arbitrary")),
        name="ffn_seq",
    )(u, u, cw, cb, _tail_rows(conv0))


def _ffn_dec_body(ug_ref, uv_ref, s0_ref, s1_ref, cw_ref, cb_ref, m_ref):
    cw = cw_ref[...]
    c = cb_ref[...] + cw[0:1] * s0_ref[...] + cw[1:2] * s1_ref[...] + cw[2:3] * ug_ref[...]
    m_ref[...] = (jax.nn.gelu(c) * uv_ref[...]).astype(BF16)


def _ffn_dec(u, cw, cb, conv0):
    B = u.shape[0]
    full = lambda shape: pl.BlockSpec(shape, lambda i: (0,) * len(shape))
    return pl.pallas_call(
        _ffn_dec_body,
        grid=(1,),
        in_specs=[pl.BlockSpec((B, D_FF), lambda i: (0, 0)), pl.BlockSpec((B, D_FF), lambda i: (0, 1)),
                  full((B, D_FF)), full((B, D_FF)), full((CONV_F, D_FF)), full((1, D_FF))],
        out_specs=full((B, D_FF)),
        out_shape=jax.ShapeDtypeStruct((B, D_FF), BF16),
        compiler_params=_cparams(("arbitrary",)),
        name="ffn_dec",
    )(u, u, conv0[:, 0], conv0[:, 1], cw, cb)


def _ffn_fused_body(x_ref, g_ref, wup_ref, cw_ref, cb_ref, wdn_ref, tail0_ref, gout_ref, o_ref, cout_ref,
                    xn_ref, m_ref, tail_ref, *, tm, tc, out_norm):
    @pl.when(pl.program_id(1) == 0)
    def _():
        tail_ref[...] = tail0_ref[0]

    x = x_ref[...]
    xn_ref[...] = _rms(x, g_ref[...]).astype(BF16)
    row = lax.broadcasted_iota(jnp.int32, (tm, 1), 0)
    for j in range(D_FF // tc):
        cols = slice(j * tc, (j + 1) * tc)
        ug = _dot(xn_ref[...], wup_ref[:, cols])
        uv = _dot(xn_ref[...], wup_ref[:, D_FF + j * tc:D_FF + (j + 1) * tc])
        t1 = tail_ref[TAIL - 1:TAIL, cols]
        t2 = tail_ref[TAIL - 2:TAIL - 1, cols]
        s1 = jnp.where(row == 0, t1, pltpu.roll(ug, 1, 0))
        s2 = jnp.where(row == 0, t2, jnp.where(row == 1, t1, pltpu.roll(ug, 2, 0)))
        c = cb_ref[:, cols] + cw_ref[0:1, cols] * s2 + cw_ref[1:2, cols] * s1 + cw_ref[2:3, cols] * ug
        m_ref[:, cols] = (jax.nn.gelu(c) * uv).astype(BF16)
        tail_ref[:, cols] = ug[tm - TAIL:tm]
        cout_ref[0, :, cols] = ug[tm - (CONV_F - 1):tm]
    y = x + _dot(m_ref[...], wdn_ref[...])
    o_ref[...] = _rms(y, gout_ref[...]) if out_norm else y


def _ffn_fused(x, g, w_up, cw, cb, w_down, conv0, *, B, T, tm, tc, shared, g_out=None):
    NT = T // tm
    out_norm = g_out is not None
    g_out = g if g_out is None else g_out
    smap = (lambda b, t: (0, 0, 0)) if shared else (lambda b, t: (b, 0, 0))
    tile = pl.BlockSpec((tm, D_MODEL), lambda b, t: (b * NT + t, 0))
    const = lambda a: pl.BlockSpec(a.shape, lambda b, t: (0, 0), pipeline_mode=pl.Buffered(1))
    return pl.pallas_call(
        functools.partial(_ffn_fused_body, tm=tm, tc=tc, out_norm=out_norm),
        grid=(B, NT),
        in_specs=[tile, pl.BlockSpec((1, D_MODEL), lambda b, t: (0, 0)), const(w_up),
                  pl.BlockSpec((CONV_F, D_FF), lambda b, t: (0, 0)), pl.BlockSpec((1, D_FF), lambda b, t: (0, 0)),
                  const(w_down), pl.BlockSpec((1, TAIL, D_FF), smap),
                  pl.BlockSpec((1, D_MODEL), lambda b, t: (0, 0))],
        out_specs=[tile, pl.BlockSpec((1, CONV_F - 1, D_FF), lambda b, t: (b, 0, 0))],
        out_shape=[jax.ShapeDtypeStruct((B * T, D_MODEL), F32),
                   jax.ShapeDtypeStruct((B, CONV_F - 1, D_FF), F32)],
        scratch_shapes=[pltpu.VMEM((tm, D_MODEL), BF16), pltpu.VMEM((tm, D_FF), BF16),
                        pltpu.VMEM((TAIL, D_FF), F32)],
        compiler_params=_cparams(("arbitrary", "arbitrary")),
        name="ffn_fused",
    )(x, g.reshape(1, -1), w_up, cw, cb, w_down, _tail_rows(conv0), g_out.reshape(1, -1))


_PREP_W = ("mix", "w_r", "w_k", "w_v", "w0", "w1", "w2", "a0", "a1", "a2", "g1", "g2")
_PREP_VP = ("v0", "v1", "v2")


def _rwkv_prep_body(*refs, has_vp, carry):
    it = iter(refs)
    x_ref, gn_ref = next(it), next(it)
    p = {n: next(it) for n in _PREP_W}
    if has_vp:
        p.update({n: next(it) for n in _PREP_VP})
        vf_ref = next(it)
    prev_ref = next(it)
    r_ref, lw_ref, k_ref, v_ref, as_ref, g_ref, hn_ref = (next(it) for _ in range(7))
    h = _rms(x_ref[...], gn_ref[...])
    if carry:
        carry_ref = next(it)

        @pl.when(pl.program_id(1) == 0)
        def _():
            carry_ref[...] = prev_ref[0]

        row = lax.broadcasted_iota(jnp.int32, (h.shape[0], 1), 0)
        hprev = jnp.where(row == 0, carry_ref[...], pltpu.roll(h, 1, 0))
        carry_ref[...] = h[h.shape[0] - 1:]
        hn_ref[0] = h[h.shape[0] - 1:]
    else:
        hprev = prev_ref[...]
        hn_ref[...] = h
    xx = hprev - h
    mix = p["mix"][...]
    xs = [(h + xx * mix[n:n + 1]).astype(BF16) for n in range(6)]
    xr, xw, xk, xv, xa, xg = xs
    r_ref[...] = _dot(xr, p["w_r"][...])
    k_ref[...] = _dot(xk, p["w_k"][...])
    v = _dot(xv, p["w_v"][...])
    wl = p["w0"][...] + _dot(jnp.tanh(_dot(xw, p["w1"][...])).astype(BF16), p["w2"][...])
    lw_ref[...] = -jnp.exp(_log_sigmoid(wl) - 0.5)
    if has_vp:
        gate = jax.nn.sigmoid(p["v0"][...] + _dot(_dot(xv, p["v1"][...]).astype(BF16), p["v2"][...]))
        v = v + (vf_ref[...] - v) * gate
    v_ref[...] = v
    as_ref[...] = jax.nn.sigmoid(p["a0"][...] + _dot(_dot(xa, p["a1"][...]).astype(BF16), p["a2"][...]))
    g_ref[...] = _dot(jax.nn.sigmoid(_dot(xg, p["g1"][...])).astype(BF16), p["g2"][...])


def _rwkv_prep(x, gn, p, vp, v_first, prev, *, B, T, tm, carry, shared=False):
    R = x.shape[0]
    has_vp = vp is not None
    if carry:
        NT = T // tm
        grid = (B, NT)
        rowmap = lambda b, t: (b * NT + t, 0)
        cmap2 = lambda b, t: (0, 0)
        pmap = (lambda b, t: (0, 0, 0)) if shared else (lambda b, t: (b, 0, 0))
        prev_spec = pl.BlockSpec((1, 1, D_MODEL), pmap)
        hn_spec = pl.BlockSpec((1, 1, D_MODEL), lambda b, t: (b, 0, 0))
        hn_shape = jax.ShapeDtypeStruct((B, 1, D_MODEL), F32)
        sem = ("arbitrary", "arbitrary")
    else:
        grid = (R // tm,)
        rowmap = lambda i: (i, 0)
        cmap2 = lambda i: (0, 0)
        prev_spec = pl.BlockSpec((tm, D_MODEL), rowmap)
        hn_spec = pl.BlockSpec((tm, D_MODEL), rowmap)
        hn_shape = jax.ShapeDtypeStruct((R, D_MODEL), F32)
        sem = ("arbitrary",)
    tile = pl.BlockSpec((tm, D_MODEL), rowmap)
    ins = [x, gn.reshape(1, -1)]
    specs = [tile, pl.BlockSpec((1, D_MODEL), cmap2)]
    names = _PREP_W + (_PREP_VP if has_vp else ())
    src = dict(p)
    if has_vp:
        src.update(vp)
    for n in names:
        ins.append(src[n])
        specs.append(pl.BlockSpec(src[n].shape, cmap2))
    if has_vp:
        ins.append(v_first)
        specs.append(tile)
    ins.append(prev)
    specs.append(prev_spec)
    outs = pl.pallas_call(
        functools.partial(_rwkv_prep_body, has_vp=has_vp, carry=carry),
        grid=grid,
        in_specs=specs,
        out_specs=[tile] * 6 + [hn_spec],
        out_shape=[jax.ShapeDtypeStruct((R, D_MODEL), F32)] * 6 + [hn_shape],
        scratch_shapes=[pltpu.VMEM((1, D_MODEL), F32)] if carry else [],
        compiler_params=_cparams(sem),
        name="rwkv_prep",
    )(*ins)
    return outs


def _rwkv_chunk_body(r_ref, lw_ref, k_ref, v_ref, as_ref, g_ref, kk_ref, ka_ref, rk_ref, gg_ref, gb_ref, s0_ref,
                     o_ref, s_ref, *, Tc, C, n_pair, unroll):
    @pl.when(pl.program_id(2) == 0)
    def _():
        s_ref[...] = s0_ref[...]

    C2 = 2 * C
    lane = lax.broadcasted_iota(jnp.int32, (1, LANES), 1)
    m0 = (lane < HS_C).astype(F32)
    m1 = 1.0 - m0
    li = lax.broadcasted_iota(jnp.int32, (LANES, LANES), 0)
    lj = lax.broadcasted_iota(jnp.int32, (LANES, LANES), 1)
    same_head = (li >= HS_C) == (lj >= HS_C)
    bd_ones = same_head.astype(F32).astype(BF16)
    bd_avg = (same_head.astype(F32) * (1.0 / HS_C)).astype(BF16)
    ti = lax.broadcasted_iota(jnp.int32, (C, C), 0)
    si = lax.broadcasted_iota(jnp.int32, (C, C), 1)
    tri = (ti >= si).astype(F32).astype(BF16)
    tri2 = jnp.concatenate([tri, tri], axis=1)
    r2 = lax.broadcasted_iota(jnp.int32, (C2, C2), 0)
    c2 = lax.broadcasted_iota(jnp.int32, (C2, C2), 1)
    same = (r2 >= C) == (c2 >= C)
    tt, ss = r2 & (C - 1), c2 & (C - 1)
    m_strict = same & (tt > ss)
    m_incl = same & (tt >= ss)
    same_sub = (tt & -RWKV_SUB) == (ss & -RWKV_SUB)
    m_sub = m_strict & same_sub
    m_off = m_strict & jnp.logical_not(same_sub)
    eye = (r2 == c2).astype(F32)
    nb = C // RWKV_SUB
    stack = lambda x: jnp.concatenate([x * m0, x * m1], axis=0)
    dup = lambda x: jnp.concatenate([x, x], axis=0)

    def sub_chunk(ci, carry):
        row_l = [pl.ds(pl.multiple_of((ci * unroll + u) * C, C), C) for u in range(unroll)]
        rows = [rw for rw in row_l for _ in range(n_pair)]
        lanes = [slice(p * LANES, (p + 1) * LANES) for p in range(n_pair)] * unroll
        each = lambda f, *cols: [f(*xs) for xs in zip(*cols)]

        def load(rw, ls):
            r, lw, k, v = r_ref[rw, ls], lw_ref[rw, ls], k_ref[rw, ls], v_ref[rw, ls]
            asig = as_ref[rw, ls]
            kk = k * kk_ref[:, ls]
            k2 = k * (1.0 + (asig - 1.0) * ka_ref[:, ls])
            return r, lw, k2, v, asig, kk

        r, lw, k2, v, asig, kk = zip(*each(load, rows, lanes))
        ssq = each(lambda x: _dot(_bf(x * x), bd_ones), kk)
        cs = each(lambda x: _dot(tri2, jnp.concatenate(_split(x), axis=0)), lw)
        kk = each(lambda x, q: x * lax.rsqrt(jnp.maximum(q, 1e-24)), kk, ssq)
        b = each(lambda x, q: x * q, kk, asig)
        cend = [x[C - 1:C] for x in cs]
        e_neg = each(lambda x: jnp.exp(-x), cs)
        e_end = each(lambda x, y: jnp.exp(y - x), cs, cend)
        As = each(lambda x, c, l: _split(stack(-x * jnp.exp(c - l))), kk, cs, lw)
        Bd = each(lambda x, e: _split(dup(x * e)), b, e_neg)
        Rs = each(lambda x, c: _bf(stack(x * jnp.exp(c))), r, cs)
        Vs = each(lambda x: _bf(stack(x)), v)
        Kd = each(lambda x, e: _bf(dup(x * e)), k2, e_neg)
        G = each(_dot3_wide_nt, As, Bd)
        Aak = each(lambda x, y: _bf(jnp.where(m_strict, _dot_nt(x[0], y), 0.0)), As, Kd)
        Arb = each(lambda x, y: _bf(jnp.where(m_incl, _dot_nt(x, y[0]), 0.0)), Rs, Bd)
        Ark = each(lambda x, y: _bf(jnp.where(m_incl, _dot_nt(x, y), 0.0)), Rs, Kd)
        Nd = each(lambda x: jnp.where(m_sub, x, 0.0), G)
        P = each(lambda x: eye + x, Nd)
        Qs = each(_split, Nd)
        for _ in range(3):
            Qs = each(lambda q: _split(_dot3_wide(q, q)), Qs)
            P = each(lambda x, q: x + _dot3_wide(_split(x), q), P, Qs)
        if nb > 1:
            Pb = each(_bf, P)
            M = each(lambda x, y: _dot(x, _bf(jnp.where(m_off, y, 0.0))), Pb, G)
            Tm = each(lambda x: eye + x, M)
            for _ in range(int(np.ceil(np.log2(nb))) - 1):
                M = each(lambda x: _dot(_bf(x), _bf(x)), M)
                Tm = each(lambda x, y: x + _dot(_bf(x), _bf(y)), Tm, M)
            Tinv = each(lambda x, y: _bf(_dot(_bf(x), y)), Tm, Pb)
        else:
            Tinv = each(_bf, P)
        bE = each(lambda x, e: _bf(stack(x * e)), b, e_end)
        kE = each(lambda x, e: _bf(stack(x * e)), k2, e_end)
        gC = each(jnp.exp, cend)
        Ys = []
        for u in range(unroll):
            sl = slice(u * n_pair, (u + 1) * n_pair)
            S = [s_ref[0, p] for p in range(n_pair)]
            Sb = each(_bf, S)
            X = each(lambda a_, s_, k_, v_: _bf(_dot_nt(a_[0], s_) + _dot(k_, v_)), As[sl], Sb, Aak[sl], Vs[sl])
            Us = each(lambda t_, x_: _bf(_dot(t_, x_)), Tinv[sl], X)
            Ys += each(lambda r_, s_, b_, u_, k_, v_: _dot_nt(r_, s_) + _dot(b_, u_) + _dot(k_, v_),
                       Rs[sl], Sb, Arb[sl], Us, Ark[sl], Vs[sl])
            Sn = each(lambda s_, g_, u_, b_, v_, k_: s_ * g_ + _dot_tn(u_, b_) + _dot_tn(v_, k_),
                      S, gC[sl], Us, bE[sl], Vs[sl], kE[sl])
            for p in range(n_pair):
                s_ref[0, p] = Sn[p]
        y = [x[:C] + x[C:] for x in Ys]
        mu = each(lambda x: _dot(_bf(x), bd_avg), y)
        d = each(lambda x, m: x - m, y, mu)
        var = each(lambda x: _dot(_bf(x * x), bd_avg), d)
        bonus = each(lambda r_, k_, ls: _dot(_bf(r_ * k_ * rk_ref[:, ls]), bd_ones), r, k2, lanes)
        for i, (rw, ls) in enumerate(zip(rows, lanes)):
            on = d[i] * lax.rsqrt(var[i] + GN_EPS_C) * gg_ref[:, ls] + gb_ref[:, ls]
            o_ref[rw, ls] = ((on + bonus[i] * v[i]) * g_ref[rw, ls]).astype(BF16)
        return carry

    lax.fori_loop(0, Tc // (C * unroll), sub_chunk, 0)


def _rwkv_chunk(arrs, p, s0, *, B, T, Tc, C, ppb, shared, unroll=1):
    NC = T // Tc
    NP = N_PAIR // ppb
    Wb = ppb * LANES
    tile = pl.BlockSpec((Tc, Wb), lambda b, q, c: (b * NC + c, q))
    vec = pl.BlockSpec((1, Wb), lambda b, q, c: (0, q))
    smap = (lambda b, q, c: (0, q, 0, 0)) if shared else (lambda b, q, c: (b, q, 0, 0))
    return pl.pallas_call(
        functools.partial(_rwkv_chunk_body, Tc=Tc, C=C, n_pair=ppb, unroll=unroll),
        grid=(B, NP, NC),
        in_specs=[tile] * 6 + [vec] * 5 + [pl.BlockSpec((1, ppb, LANES, LANES), smap)],
        out_specs=[tile, pl.BlockSpec((1, ppb, LANES, LANES), lambda b, q, c: (b, q, 0, 0))],
        out_shape=[jax.ShapeDtypeStruct((B * T, D_MODEL), BF16),
                   jax.ShapeDtypeStruct((B, N_PAIR, LANES, LANES), F32)],
        compiler_params=_cparams(("arbitrary", "arbitrary", "arbitrary")),
        name="rwkv_chunk",
    )(*arrs, p["k_k"], p["k_a"], p["r_k"], p["gn_g"], p["gn_b"], s0)


def _rwkv_dec_body(r_ref, lw_ref, k_ref, v_ref, as_ref, g_ref, kk_ref, ka_ref, rk_ref, gg_ref, gb_ref, s0_ref,
                   o_ref, s_ref, r_scr, w_scr, k2_scr, v_scr, a_scr, b_scr, y_scr, *, nb):
    n_head = 2 * N_PAIR
    ri = lax.broadcasted_iota(jnp.int32, (HS_C, HS_C), 0)
    ci = lax.broadcasted_iota(jnp.int32, (HS_C, HS_C), 1)
    eye = (ri == ci).astype(F32)
    for h in range(n_head):
        ls = slice(h * HS_C, (h + 1) * HS_C)
        k, asig = k_ref[:, ls], as_ref[:, ls]
        kk = k * kk_ref[:, ls]
        kk = kk * lax.rsqrt(jnp.maximum(jnp.sum(kk * kk, axis=1, keepdims=True), 1e-24))
        a_scr[h] = -kk
        b_scr[h] = kk * asig
        k2_scr[h] = k * (1.0 + (asig - 1.0) * ka_ref[:, ls])
        r_scr[h] = r_ref[:, ls]
        v_scr[h] = v_ref[:, ls]
        w_scr[h] = jnp.exp(lw_ref[:, ls])

    def per_sample(s, carry):
        row = pl.ds(s, 1)
        heads = range(n_head)
        S = [s0_ref[s, h] for h in heads]
        sa = [jnp.sum(S[h] * a_scr[h, row, :], axis=1, keepdims=True) for h in heads]
        v_col = [jnp.sum(eye * v_scr[h, row, :], axis=1, keepdims=True) for h in heads]
        Sn = [S[h] * w_scr[h, row, :] + sa[h] * b_scr[h, row, :] + v_col[h] * k2_scr[h, row, :] for h in heads]
        for h in heads:
            s_ref[s, h] = Sn[h]
        y_col = [jnp.sum(Sn[h] * r_scr[h, row, :], axis=1, keepdims=True) for h in heads]
        for h in heads:
            y_scr[h, row, :] = jnp.sum(eye * y_col[h], axis=0, keepdims=True)
        return carry

    lax.fori_loop(0, nb, per_sample, 0)
    for h in range(n_head):
        ls = slice(h * HS_C, (h + 1) * HS_C)
        y = y_scr[h]
        d = y - jnp.mean(y, axis=1, keepdims=True)
        var = jnp.mean(d * d, axis=1, keepdims=True)
        on = d * lax.rsqrt(var + GN_EPS_C) * gg_ref[:, ls] + gb_ref[:, ls]
        bonus = jnp.sum(r_scr[h] * k2_scr[h] * rk_ref[:, ls], axis=1, keepdims=True) * v_scr[h]
        o_ref[:, ls] = (on + bonus) * g_ref[:, ls]


def _rwkv_dec(arrs, p, s0, *, nb, layer, prev=None):
    B = s0.shape[1]
    n_head = 2 * N_PAIR
    tile = pl.BlockSpec((nb, D_MODEL), lambda i: (i, 0))
    vec = pl.BlockSpec((1, D_MODEL), lambda i: (0, 0))
    sspec = pl.BlockSpec((None, nb, n_head, HS_C, HS_C), lambda i: (layer, i, 0, 0, 0))
    ins = [*arrs, p["k_k"], p["k_a"], p["r_k"], p["gn_g"], p["gn_b"], s0]
    specs = [tile] * 6 + [vec] * 5 + [sspec]
    body = functools.partial(_rwkv_dec_body, nb=nb)
    aliases = {}
    if prev is not None:
        ins.append(prev)
        specs.append(pl.BlockSpec(memory_space=pl.ANY))
        aliases = {len(ins) - 1: 1}
        body = _drop_ref(body, len(ins) - 1)
    return pl.pallas_call(
        body,
        grid=(B // nb,),
        in_specs=specs,
        out_specs=[tile, sspec],
        out_shape=[jax.ShapeDtypeStruct((B, D_MODEL), F32), jax.ShapeDtypeStruct(s0.shape, F32)],
        scratch_shapes=[pltpu.VMEM((n_head, nb, HS_C), F32)] * 7,
        input_output_aliases=aliases,
        compiler_params=_cparams(("arbitrary",)),
        name="rwkv_dec",
    )(*ins)


def _from_blockdiag(s):
    B = s.shape[0]
    return jnp.stack([s[:, :, :HS_C, :HS_C], s[:, :, HS_C:, HS_C:]], axis=2).reshape(B, 2 * N_PAIR, HS_C, HS_C)


MAIN_TM = 2048
MAIN_TN = 512
RET_C = 128
LRU_TC = 256
FFN_TC = 256
FFN_TM = 512
FFN_TCOL = 256
PREP_TM = 256
RWKV_TC = 256
RWKV_C = 64
RWKV_PPB = 8
RWKV_UNROLL = 2
RWKV_DEC_NB = 16
RET_DEC_NB = 8


def kernel(x_prompt, x_sample, state_ret, state_lru, state_lru_conv, state_rwkv, state_shift, state_ffn_conv,
           meta_tokens, norm_mix_g, norm_ffn_g, norm_final_g,
           ev_w_in, ev_ret_gn_g, ev_lru_conv_w, ev_lru_conv_b, ev_lru_wa, ev_lru_ba, ev_lru_wx, ev_lru_bx,
           ev_lru_lambda, ev_w_out,
           od_mix, od_w_r, od_w_k, od_w_v, od_w0, od_w1, od_w2, od_a0, od_a1, od_a2, od_v0, od_v1, od_v2,
           od_g1, od_g2, od_k_k, od_k_a, od_r_k, od_gn_g, od_gn_b, od_w_o,
           ff_w_up, ff_conv_w, ff_conv_b, ff_w_down):
    BP, TP, _ = x_prompt.shape
    BS = x_sample.shape[0]
    NS = BS + N_META
    bf = lambda a: a.astype(BF16)
    row = lambda a: a.reshape(1, -1)

    def lora_in(w):
        return bf(jnp.pad(w, ((0, 0), (0, LORA_PAD - w.shape[1]))))

    def lora_out(w):
        return bf(jnp.pad(w, ((0, LORA_PAD - w.shape[0]), (0, 0))))

    xm = x_prompt.reshape(BP * TP, D_MODEL)
    xs = jnp.concatenate([x_sample.reshape(BS, D_MODEL), meta_tokens.astype(x_prompt.dtype)], axis=0)

    cos_m, sin_m = _rope_tables(N_META + jnp.arange(TP, dtype=jnp.int32))
    cos_t, sin_t = _rope_tables(jnp.arange(N_META, dtype=jnp.int32))
    cos_s, sin_s = _rope_tables(PAST_LEN + jnp.arange(PAD_T, dtype=jnp.int32))

    out = {k: [] for k in ("ret_p", "lru_p", "lconv_p", "rwkv_p", "shift_p", "ffn_p",
                           "ret_s", "lru_s", "lconv_s", "rwkv_s", "shift_s", "ffn_s")}
    v_first_m = v_first_s = None
    ret_s_all = jnp.zeros(state_ret.shape, F32)
    rwkv_s_all = jnp.zeros(state_rwkv.shape, F32)
    for li in range(4):
        j = li // 2
        if li % 2 == 0:
            w_in = bf(ev_w_in[j])
            w_out_a, w_out_b = bf(ev_w_out[j][:VG_W]), bf(ev_w_out[j][VG_W:])
            gn = ev_ret_gn_g[j]
            lp = dict(conv_w=ev_lru_conv_w[j], conv_b=row(ev_lru_conv_b[j]), wa=bf(ev_lru_wa[j]),
                      ba=row(ev_lru_ba[j]), wx=bf(ev_lru_wx[j]), bx=row(ev_lru_bx[j]), lam=row(ev_lru_lambda[j]))
            zs = _mm([xs], [w_in], g=norm_mix_g[li], tm=NS, tn=MAIN_TN)
            z_smp, z_meta = zs[:BS], zs[BS:]
            ya_t, ret_t = _retention(z_meta, cos_t, sin_t, jnp.zeros((1, H_A, DK_A, DV_A), F32), gn,
                                     B=1, T=N_META, C=N_META, c_true=N_META, shared=False)
            yb_t, lru_t, lconv_t = _lru_seq(z_meta, lp, jnp.zeros((1, 1, W_B), F32),
                                            jnp.zeros((1, CONV_B - 1, W_B), F32), B=1, T=N_META, Tc=N_META,
                                            shared=False)
            ya_s, ret_s_all = _retention(z_smp, cos_s, sin_s, state_ret, gn, B=BS, T=1, C=PAD_T, c_true=1,
                                         shared=False, bb=RET_DEC_NB, single=True, layer=j, prev=ret_s_all)
            yb_s, lru_s = _lru_dec(z_smp, lp, state_lru[j], state_lru_conv[j])
            lconv_s = jnp.concatenate([state_lru_conv[j][:, 1:], z_smp[:, None, 3 * W_B:4 * W_B]], axis=1)
            ya = jnp.concatenate([ya_s, ya_t.astype(F32)], axis=0)
            yb = jnp.concatenate([yb_s, yb_t], axis=0)
            xs = _mm([ya, yb], [w_out_a, w_out_b], res=xs, tm=NS, tn=MAIN_TN)
            xm, lru_m, lconv_m, ret_m = _even_fused(xm, norm_mix_g[li], w_in, lp, lru_t, lconv_t, cos_m, sin_m, ret_t,
                                                    gn, w_out_a, w_out_b, B=BP, T=TP, Tc=LRU_TC, tn=MAIN_TN,
                                                    C=RET_C)
            out["ret_p"].append(ret_m)
            out["lru_p"].append(lru_m[:, 0])
            out["lconv_p"].append(lconv_m)
            out["lru_s"].append(lru_s)
            out["lconv_s"].append(lconv_s)
        else:
            p = dict(mix=od_mix[j], w_r=bf(od_w_r[j]), w_k=bf(od_w_k[j]), w_v=bf(od_w_v[j]), w0=row(od_w0[j]),
                     w1=lora_in(od_w1[j]), w2=lora_out(od_w2[j]), a0=row(od_a0[j]), a1=lora_in(od_a1[j]),
                     a2=lora_out(od_a2[j]), g1=lora_in(od_g1[j]), g2=lora_out(od_g2[j]),
                     k_k=row(od_k_k[j]), k_a=row(od_k_a[j]), r_k=row(od_r_k[j]), gn_g=row(od_gn_g[j]),
                     gn_b=row(od_gn_b[j]))
            vp = None
            if j > 0:
                vp = dict(v0=row(od_v0[j - 1]), v1=lora_in(od_v1[j - 1]), v2=lora_out(od_v2[j - 1]))
            w_o = bf(od_w_o[j])
            gmix = norm_mix_g[li]
            pre_t = _rwkv_prep(xs[BS:], gmix, p, vp, None if vp is None else v_first_s[BS:],
                               jnp.zeros((1, 1, D_MODEL), F32), B=1, T=N_META, tm=N_META, carry=True)
            o_t, rw_t = _rwkv_chunk(pre_t[:6], p, jnp.zeros((1, N_PAIR, LANES, LANES), F32),
                                    B=1, T=N_META, Tc=N_META, C=N_META, ppb=RWKV_PPB, shared=False)
            shift_t = pre_t[6]
            pre_s = _rwkv_prep(xs[:BS], gmix, p, vp, None if vp is None else v_first_s[:BS],
                               state_shift[j], B=BS, T=1, tm=BS, carry=False)
            o_s, rwkv_s_all = _rwkv_dec(pre_s[:6], p, state_rwkv, nb=RWKV_DEC_NB, layer=j, prev=rwkv_s_all)
            if vp is None:
                v_first_s = jnp.concatenate([pre_s[3], pre_t[3]], axis=0)
            o_small = jnp.concatenate([o_s, o_t.astype(F32)], axis=0)
            xs = _mm([o_small], [w_o], res=xs, tm=NS, tn=MAIN_TN)
            pre_m = _rwkv_prep(xm, gmix, p, vp, v_first_m, shift_t, B=BP, T=TP, tm=PREP_TM, carry=True, shared=True)
            if vp is None:
                v_first_m = pre_m[3]
            o_m, rw_m = _rwkv_chunk(pre_m[:6], p, rw_t, B=BP, T=TP, Tc=RWKV_TC, C=RWKV_C, ppb=RWKV_PPB, shared=True,
                                    unroll=RWKV_UNROLL)
            xm = _mm([o_m], [w_o], res=xm, tm=MAIN_TM, tn=MAIN_TN)
            out["rwkv_p"].append(_from_blockdiag(rw_m))
            out["shift_p"].append(pre_m[6][:, 0])
            out["shift_s"].append(pre_s[6])
        w_up, w_down = bf(ff_w_up[li]), bf(ff_w_down[li])
        cw, cb = ff_conv_w[li], row(ff_conv_b[li])
        us = _mm([xs], [w_up], g=norm_ffn_g[li], tm=NS, tn=MAIN_TN)
        m_t, ffn_t = _ffn_seq(us[BS:], cw, cb, jnp.zeros((1, CONV_F - 1, D_FF), F32), B=1, T=N_META, Tc=N_META,
                              shared=False)
        m_s = _ffn_dec(us[:BS], cw, cb, state_ffn_conv[li])
        ffn_s = jnp.concatenate([state_ffn_conv[li][:, 1:], us[:BS, None, :D_FF]], axis=1)
        xs = _mm([jnp.concatenate([m_s, m_t], axis=0)], [w_down], res=xs, tm=NS, tn=MAIN_TN)
        xm, ffn_m = _ffn_fused(xm, norm_ffn_g[li], w_up, cw, cb, w_down, ffn_t, B=BP, T=TP, tm=FFN_TM, tc=FFN_TCOL,
                               shared=True, g_out=norm_final_g if li == 3 else None)
        out["ffn_p"].append(ffn_m)
        out["ffn_s"].append(ffn_s)

    y_prompt = xm.reshape(BP, TP, D_MODEL)
    y_sample = _final_norm(xs, norm_final_g, NS)[:BS].reshape(BS, 1, D_MODEL)
    st = lambda k: jnp.stack(out[k])
    return (y_prompt, y_sample,
            st("ret_p"), st("lru_p"), st("lconv_p"), st("rwkv_p"), st("shift_p"), st("ffn_p"),
            ret_s_all, st("lru_s"), st("lconv_s"), rwkv_s_all, st("shift_s"), st("ffn_s"))
```

```python
import functools

import numpy as np
import jax
import jax.numpy as jnp
from jax import lax
from jax.experimental import pallas as pl
from jax.experimental.pallas import tpu as pltpu

F32 = jnp.float32
BF16 = jnp.bfloat16
HI = lax.Precision.HIGHEST

D_MODEL = 1024
N_META = 16
PAST_LEN = 16384
H_A, DK_A, DV_A = 4, 128, 256
W_B, NB_B, BS_B, CONV_B = 1024, 8, 128, 4
LRU_C = 8.0
HS_C = 64
N_PAIR = D_MODEL // (2 * HS_C)
LORA_PAD = 128
GN_EPS_C = 64e-5
D_FF, CONV_F = 2816, 3
EPS = 1e-6
ROPE_BASE = 10000.0
QK_W, VG_W = H_A * DK_A, H_A * DV_A

LANES = 128
TAIL = 8
VMEM_LIMIT = 52 * 1024 * 1024
PAD_T = 16
RWKV_SUB = 16


def _cparams(sem):
    return pltpu.CompilerParams(dimension_semantics=sem, vmem_limit_bytes=VMEM_LIMIT)


def _dot(a, b, prec=None):
    return jnp.dot(a, b, preferred_element_type=F32, precision=prec)


def _dot_nt(a, b, prec=None):
    return lax.dot_general(a, b, (((1,), (1,)), ((), ())), preferred_element_type=F32, precision=prec)


def _dot_tn(a, b, prec=None):
    return lax.dot_general(a, b, (((0,), (0,)), ((), ())), preferred_element_type=F32, precision=prec)


def _bf(x):
    return x.astype(BF16)


def _split(x):
    hi = x.astype(BF16)
    return hi, (x - hi.astype(F32)).astype(BF16)


def _dot3(a, b, dot):
    return dot(a[0], b[0]) + dot(a[0], b[1]) + dot(a[1], b[0])


def _dot3_wide(a, b):
    n = a[0].shape[1]
    lhs = jnp.concatenate([a[0], a[1]], axis=1)
    rhs = jnp.concatenate([jnp.concatenate([b[0], b[1]], axis=1),
                           jnp.concatenate([b[0], jnp.zeros_like(b[0])], axis=1)], axis=0)
    out = _dot(lhs, rhs)
    return out[:, :n] + out[:, n:]


def _dot3_wide_nt(a, b):
    n = b[0].shape[0]
    lhs = jnp.concatenate([a[0], a[1]], axis=1)
    rhs = jnp.concatenate([jnp.concatenate([b[0], b[0]], axis=1),
                           jnp.concatenate([b[1], jnp.zeros_like(b[1])], axis=1)], axis=0)
    out = _dot_nt(lhs, rhs)
    return out[:, :n] + out[:, n:]


def _rms(x, g):
    return x * lax.rsqrt(jnp.mean(x * x, axis=-1, keepdims=True) + EPS) * g


def _log_sigmoid(x):
    return jnp.minimum(x, 0.0) - jnp.log(1.0 + jnp.exp(-jnp.abs(x)))


def _mm_body(*refs, n_in, norm, res):
    it = iter(refs)
    x_refs = [next(it) for _ in range(n_in)]
    g_ref = next(it) if norm else None
    w_refs = [next(it) for _ in range(n_in)]
    r_ref = next(it) if res else None
    o_ref = next(it)
    if norm:
        xn_ref = next(it)

        @pl.when(pl.program_id(1) == 0)
        def _():
            xn_ref[...] = _rms(x_refs[0][...], g_ref[...]).astype(BF16)

        acc = _dot(xn_ref[...], w_refs[0][...])
    else:
        acc = _dot(x_refs[0][...].astype(BF16), w_refs[0][...])
        for x_ref, w_ref in zip(x_refs[1:], w_refs[1:]):
            acc = acc + _dot(x_ref[...].astype(BF16), w_ref[...])
    if res:
        acc = acc + r_ref[...]
    o_ref[...] = acc.astype(o_ref.dtype)


def _mm(xs, ws, *, g=None, res=None, tm, tn, out_dtype=F32):
    R = xs[0].shape[0]
    N = ws[0].shape[1]
    norm = g is not None
    ins, specs = [], []
    for x in xs:
        ins.append(x)
        specs.append(pl.BlockSpec((tm, x.shape[1]), lambda i, j: (i, 0)))
    if norm:
        ins.append(g.reshape(1, -1))
        specs.append(pl.BlockSpec((1, g.shape[-1]), lambda i, j: (0, 0)))
    for w in ws:
        ins.append(w)
        specs.append(pl.BlockSpec((w.shape[0], tn), lambda i, j: (0, j)))
    if res is not None:
        ins.append(res)
        specs.append(pl.BlockSpec((tm, tn), lambda i, j: (i, j)))
    scratch = [pltpu.VMEM((tm, xs[0].shape[1]), BF16)] if norm else []
    return pl.pallas_call(
        functools.partial(_mm_body, n_in=len(xs), norm=norm, res=res is not None),
        grid=(R // tm, N // tn),
        in_specs=specs,
        out_specs=pl.BlockSpec((tm, tn), lambda i, j: (i, j)),
        out_shape=jax.ShapeDtypeStruct((R, N), out_dtype),
        scratch_shapes=scratch,
        compiler_params=_cparams(("arbitrary", "arbitrary")),
        name="mm",
    )(*ins)


def _final_norm_body(x_ref, g_ref, o_ref):
    o_ref[...] = _rms(x_ref[...], g_ref[...])


def _final_norm(x, g, tm):
    R = x.shape[0]
    return pl.pallas_call(
        _final_norm_body,
        grid=(R // tm,),
        in_specs=[pl.BlockSpec((tm, D_MODEL), lambda i: (i, 0)), pl.BlockSpec((1, D_MODEL), lambda i: (0, 0))],
        out_specs=pl.BlockSpec((tm, D_MODEL), lambda i: (i, 0)),
        out_shape=jax.ShapeDtypeStruct((R, D_MODEL), F32),
        compiler_params=_cparams(("arbitrary",)),
        name="final_norm",
    )(x, g.reshape(1, -1))


def _ret_heads(q_ref, k_ref, v_ref, ga_ref, offs, chunks, cos_ref, sin_ref, s_ref, gn_ref, y_ref, *, C, c_true, single,
               after_chunk=None):
    qo, ko, vo, go = offs
    ti = lax.broadcasted_iota(jnp.int32, (C, C), 0)
    si = lax.broadcasted_iota(jnp.int32, (C, C), 1)
    dif = (ti - si).astype(F32)
    trow = lax.broadcasted_iota(jnp.int32, (C, 1), 0)
    tcol = trow.astype(F32)
    chunk = (lambda x: jnp.where(trow == 0, x, 0.0)) if single else (lambda x: x)
    heads = range(H_A)
    lg = [float(np.log1p(-(2.0 ** (-5.0 - h)))) for h in heads]
    mask = [jnp.where(dif >= 0, jnp.exp(lg[h] * jnp.maximum(dif, 0.0)), 0.0) for h in heads]
    dec_in = [jnp.exp((tcol + 1.0) * lg[h]) for h in heads]
    dec_k = [jnp.exp((c_true - 1.0 - tcol) * lg[h]) for h in heads]
    qc = lambda h: slice(qo + h * DK_A, qo + (h + 1) * DK_A)
    kc = lambda h: slice(ko + h * DK_A, ko + (h + 1) * DK_A)
    vc = lambda h: slice(vo + h * DV_A, vo + (h + 1) * DV_A)
    gc = lambda h: slice(go + h * DV_A, go + (h + 1) * DV_A)
    yc = lambda h: slice(h * DV_A, (h + 1) * DV_A)
    for ci, (rs, sidx, ts) in enumerate(chunks):
        cosf, sinf = cos_ref[ts, :], sin_ref[ts, :]
        rot = lambda x: x * cosf + pltpu.roll(x, DK_A // 2, 1) * sinf
        q = [rot(chunk(q_ref[rs, qc(h)])) for h in heads]
        k = [rot(chunk(k_ref[rs, kc(h)])) * (DK_A ** -0.5) for h in heads]
        vb = [chunk(v_ref[rs, vc(h)]).astype(BF16) for h in heads]
        qb = [x.astype(BF16) for x in q]
        sc = [(_dot_nt(qb[h], k[h].astype(BF16)) * mask[h]).astype(BF16) for h in heads]
        S = [s_ref[sidx, h] for h in heads]
        o = [_dot(sc[h], vb[h]) + _dot(qb[h], S[h].astype(BF16)) * dec_in[h] for h in heads]
        for h in heads:
            s_ref[sidx, h] = float(np.exp(c_true * lg[h])) * S[h] + _dot_tn((k[h] * dec_k[h]).astype(BF16), vb[h])
        d = [x - jnp.mean(x, axis=-1, keepdims=True) for x in o]
        var = [jnp.mean(x * x, axis=-1, keepdims=True) for x in d]
        for h in heads:
            on = d[h] * lax.rsqrt(var[h] + EPS) * gn_ref[:, yc(h)]
            ga = ga_ref[rs, gc(h)]
            if single:
                on = on[0:1]
            y_ref[rs, yc(h)] = (ga * jax.nn.sigmoid(ga) * on).astype(y_ref.dtype)
        if after_chunk is not None:
            after_chunk(ci, rs)


def _own_layer(s_ref, layer):
    if layer is None:
        return s_ref
    for other in range(s_ref.shape[0]):
        if other != layer:
            s_ref[other] = jnp.zeros(s_ref.shape[1:], s_ref.dtype)
    return s_ref.at[layer]


def _ret_body(q_ref, k_ref, v_ref, ga_ref, cos_ref, sin_ref, s0_ref, gn_ref, y_ref, s_ref, *, C, c_true, bb, single,
              layer=None):
    s_ref = _own_layer(s_ref, layer)

    @pl.when(pl.program_id(1) == 0)
    def _():
        s_ref[...] = s0_ref[...]

    rows = lambda s: slice(s, s + 1) if single else slice(s * C, (s + 1) * C)
    _ret_heads(q_ref, k_ref, v_ref, ga_ref, (0, 0, 0, 0), [(rows(s), s, slice(None)) for s in range(bb)],
               cos_ref, sin_ref, s_ref, gn_ref, y_ref, C=C, c_true=c_true, single=single)


def _drop_ref(body, i):
    def wrapped(*refs, **kw):
        return body(*refs[:i], *refs[i + 1:], **kw)
    return wrapped


def _retention(z, cosf, sinf, s0, gn, *, B, T, C, c_true, shared, bb=1, single=False, layer=None, prev=None):
    NC = 1 if single else T // C
    assert bb == 1 or (NC == 1 and not shared)
    assert not single or (T == 1 and c_true == 1)
    row = lambda b, c: b * NC + c
    R = bb if single else bb * C
    if layer is None:
        smap = (lambda b, c: (0, 0, 0, 0)) if shared else (lambda b, c: (b, 0, 0, 0))
        s_in = pl.BlockSpec((bb, H_A, DK_A, DV_A), smap)
        s_out = pl.BlockSpec((bb, H_A, DK_A, DV_A), lambda b, c: (b, 0, 0, 0))
        s_shape = (B, H_A, DK_A, DV_A)
    else:
        s_in = s_out = pl.BlockSpec((None, bb, H_A, DK_A, DV_A), lambda b, c: (layer, b, 0, 0, 0))
        if prev is None:
            s_out = pl.BlockSpec((s0.shape[0], bb, H_A, DK_A, DV_A), lambda b, c: (0, b, 0, 0, 0))
        s_shape = s0.shape
    ins = [z, z, z, z, cosf, sinf, s0, gn.reshape(1, -1)]
    specs = [
        pl.BlockSpec((R, QK_W), lambda b, c: (row(b, c), 0)),
        pl.BlockSpec((R, QK_W), lambda b, c: (row(b, c), 1)),
        pl.BlockSpec((R, VG_W), lambda b, c: (row(b, c), 1)),
        pl.BlockSpec((R, VG_W), lambda b, c: (row(b, c), 2)),
        pl.BlockSpec((C, DK_A), lambda b, c: (c, 0)),
        pl.BlockSpec((C, DK_A), lambda b, c: (c, 0)),
        s_in,
        pl.BlockSpec((1, VG_W), lambda b, c: (0, 0)),
    ]
    body = functools.partial(_ret_body, C=C, c_true=c_true, bb=bb, single=single,
                             layer=layer if prev is None else None)
    aliases = {}
    if prev is not None:
        ins.append(prev)
        specs.append(pl.BlockSpec(memory_space=pl.ANY))
        aliases = {len(ins) - 1: 1}
        body = _drop_ref(body, len(ins) - 1)
    return pl.pallas_call(
        body,
        grid=(B // bb, NC),
        in_specs=specs,
        out_specs=[pl.BlockSpec((R, VG_W), lambda b, c: (row(b, c), 0)), s_out],
        out_shape=[
            jax.ShapeDtypeStruct((B * T, VG_W), F32 if single else BF16),
            jax.ShapeDtypeStruct(s_shape, F32),
        ],
        input_output_aliases=aliases,
        compiler_params=_cparams(("arbitrary", "arbitrary")),
        name="retention",
    )(*ins)


def _rope_tables(pos):
    half = DK_A // 2
    inv = ROPE_BASE ** (-jnp.linspace(0.0, 1.0, half, dtype=F32))
    ang = pos.astype(F32)[:, None] * inv[None, :]
    cos, sin = jnp.cos(ang), jnp.sin(ang)
    return jnp.concatenate([cos, cos], axis=-1), jnp.concatenate([-sin, sin], axis=-1)


def _lru_gates(xc, wa, ba, wx, bx, lam):
    xg = xc.astype(BF16)
    r = jax.nn.sigmoid(_dot(xg, wa) + ba)
    i = jax.nn.sigmoid(_dot(xg, wx) + bx)
    log_a = (-LRU_C) * r * (-_log_sigmoid(lam))
    a = jnp.exp(log_a)
    u = jnp.sqrt(1.0 - a * a) * (i * xc)
    return a, u


def _lru_tile(x_of, gb_of, prm, h0_ref, tail0_ref, y_ref, hout_ref, cout_ref, xe_ref, hc_ref, scan_refs, Tc,
              between=None):
    cw_ref, cb_ref, wa_ref, ba_ref, wx_ref, bx_ref, lam_ref = prm
    a_scr, u_scr, h_scr = scan_refs
    seg_len = Tc // TAIL
    pitch = seg_len + 1

    @pl.when(pl.program_id(1) == 0)
    def _():
        xe_ref[0:TAIL, :] = tail0_ref[0]
        hc_ref[...] = h0_ref[0]

    for g in range(NB_B):
        cs = slice(g * BS_B, (g + 1) * BS_B)
        x = x_of(g)
        xe_ref[TAIL:TAIL + Tc, cs] = x
        xc = cb_ref[:, cs]
        for j in range(CONV_B - 1):
            off = TAIL - (CONV_B - 1) + j
            xc = xc + cw_ref[j:j + 1, cs] * xe_ref[off:off + Tc, cs]
        xc = xc + cw_ref[CONV_B - 1:CONV_B, cs] * x
        a, u = _lru_gates(xc, wa_ref[g], ba_ref[:, cs], wx_ref[g], bx_ref[:, cs], lam_ref[:, cs])
        for s in range(TAIL):
            a_scr[s * pitch:s * pitch + seg_len, :] = a[s * seg_len:(s + 1) * seg_len]
            u_scr[s * pitch:s * pitch + seg_len, :] = u[s * seg_len:(s + 1) * seg_len]
        step = lambda ref, p: ref[pl.ds(p, TAIL, stride=pitch), :]
        a_tot, u_tot = step(a_scr, 0), step(u_scr, 0)
        for p in range(1, seg_len):
            a_p = step(a_scr, p)
            u_tot = a_p * u_tot + step(u_scr, p)
            a_tot = a_p * a_tot
        h_in = hc_ref[:, cs]
        starts = [h_in]
        for s in range(TAIL - 1):
            h_in = a_tot[s:s + 1] * h_in + u_tot[s:s + 1]
            starts.append(h_in)
        h_last = a_tot[TAIL - 1:TAIL] * h_in + u_tot[TAIL - 1:TAIL]
        h = jnp.concatenate(starts, axis=0)
        for p in range(seg_len):
            h = step(a_scr, p) * h + step(u_scr, p)
            h_scr[pl.ds(p, TAIL, stride=pitch), :] = h
        hs = jnp.concatenate([h_scr[s * pitch:s * pitch + seg_len, :] for s in range(TAIL)], axis=0)
        y_ref[:, cs] = (hs * jax.nn.gelu(gb_of(g))).astype(BF16)
        hc_ref[:, cs] = h_last
        hout_ref[0, :, cs] = h_last
        cout_ref[0, :, cs] = xe_ref[Tc + TAIL - (CONV_B - 1):Tc + TAIL, cs]
        xe_ref[0:TAIL, cs] = xe_ref[Tc:Tc + TAIL, cs]
        if between is not None:
            between(g)


def _lru_seq_body(xb_ref, gb_ref, cw_ref, cb_ref, wa_ref, ba_ref, wx_ref, bx_ref, lam_ref, h0_ref, tail0_ref,
                  y_ref, hout_ref, cout_ref, xe_ref, hc_ref, a_scr, u_scr, h_scr, *, Tc):
    blk = lambda ref: (lambda g: ref[:, g * BS_B:(g + 1) * BS_B])
    _lru_tile(blk(xb_ref), blk(gb_ref), (cw_ref, cb_ref, wa_ref, ba_ref, wx_ref, bx_ref, lam_ref),
              h0_ref, tail0_ref, y_ref, hout_ref, cout_ref, xe_ref, hc_ref, (a_scr, u_scr, h_scr), Tc)


QKVG_W = 2 * QK_W + 2 * VG_W


def _even_body(x_ref, g_ref, w_ref, cw_ref, cb_ref, wa_ref, ba_ref, wx_ref, bx_ref, lam_ref, h0_ref, tail0_ref,
               cos_ref, sin_ref, s0_ref, gn_ref, woa_ref, wob_ref, o_ref, hout_ref, cout_ref, s_ref,
               xn_ref, zq_ref, ya_ref, yb_ref, xe_ref, hc_ref, a_scr, u_scr, h_scr, *, Tc, tn, C):
    @pl.when(pl.program_id(1) == 0)
    def _():
        s_ref[...] = s0_ref[...]

    xn_ref[...] = _rms(x_ref[...], g_ref[...]).astype(BF16)
    pairs = {}

    def col(start):
        def block(g):
            key = (start, g // 2)
            if key not in pairs:
                lo = start + (g // 2) * 2 * BS_B
                pairs[key] = _dot(xn_ref[...], w_ref[:, lo:lo + 2 * BS_B])
            return pairs[key][:, (g % 2) * BS_B:(g % 2 + 1) * BS_B]
        return block

    def qkvg_chunk(c):
        if c < QKVG_W // tn:
            zq_ref[:, c * tn:(c + 1) * tn] = _dot(xn_ref[...], w_ref[:, c * tn:(c + 1) * tn])

    _lru_tile(col(QKVG_W), col(QKVG_W + W_B), (cw_ref, cb_ref, wa_ref, ba_ref, wx_ref, bx_ref, lam_ref),
              h0_ref, tail0_ref, yb_ref, hout_ref, cout_ref, xe_ref, hc_ref, (a_scr, u_scr, h_scr), Tc,
              between=qkvg_chunk)
    o_ref[...] = x_ref[...] + _dot(yb_ref[...], wob_ref[...])
    chunks = [(slice(c * C, (c + 1) * C), 0, slice(c * C, (c + 1) * C)) for c in range(Tc // C)]
    def project_rows(ci, rs):
        o_ref[rs, :] += _dot(ya_ref[rs, :], woa_ref[...])

    _ret_heads(zq_ref, zq_ref, zq_ref, zq_ref, (0, QK_W, 2 * QK_W, 2 * QK_W + VG_W), chunks, cos_ref, sin_ref,
               s_ref, gn_ref, ya_ref, C=C, c_true=C, single=False, after_chunk=project_rows)


def _even_fused(x, g, w_in, p, h0, conv0, cosf, sinf, s0, gn, w_out_a, w_out_b, *, B, T, Tc, tn, C):
    NC = T // Tc
    shared3 = lambda b, c: (0, 0, 0)
    row = lambda b, c: (b * NC + c, 0)
    vec = pl.BlockSpec((1, W_B), lambda b, c: (0, 0))
    wspec = pl.BlockSpec((NB_B, BS_B, BS_B), lambda b, c: (0, 0, 0))
    const = lambda a: pl.BlockSpec(a.shape, lambda b, c: (0, 0), pipeline_mode=pl.Buffered(1))
    tile = pl.BlockSpec((Tc, D_MODEL), row)
    return pl.pallas_call(
        functools.partial(_even_body, Tc=Tc, tn=tn, C=C),
        grid=(B, NC),
        in_specs=[
            tile, pl.BlockSpec((1, D_MODEL), lambda b, c: (0, 0)), const(w_in),
            pl.BlockSpec((CONV_B, W_B), lambda b, c: (0, 0)), vec, wspec, vec, wspec, vec, vec,
            pl.BlockSpec((1, 1, W_B), shared3), pl.BlockSpec((1, TAIL, W_B), shared3),
            pl.BlockSpec((Tc, DK_A), lambda b, c: (c, 0)), pl.BlockSpec((Tc, DK_A), lambda b, c: (c, 0)),
            pl.BlockSpec((1, H_A, DK_A, DV_A), lambda b, c: (0, 0, 0, 0)),
            pl.BlockSpec((1, VG_W), lambda b, c: (0, 0)), const(w_out_a), const(w_out_b),
        ],
        out_specs=[
            tile,
            pl.BlockSpec((1, 1, W_B), lambda b, c: (b, 0, 0)),
            pl.BlockSpec((1, CONV_B - 1, W_B), lambda b, c: (b, 0, 0)),
            pl.BlockSpec((1, H_A, DK_A, DV_A), lambda b, c: (b, 0, 0, 0)),
        ],
        out_shape=[
            jax.ShapeDtypeStruct((B * T, D_MODEL), F32),
            jax.ShapeDtypeStruct((B, 1, W_B), F32),
            jax.ShapeDtypeStruct((B, CONV_B - 1, W_B), F32),
            jax.ShapeDtypeStruct((B, H_A, DK_A, DV_A), F32),
        ],
        scratch_shapes=[pltpu.VMEM((Tc, D_MODEL), BF16), pltpu.VMEM((Tc, QKVG_W), F32),
                        pltpu.VMEM((Tc, VG_W), BF16), pltpu.VMEM((Tc, W_B), BF16),
                        pltpu.VMEM((Tc + TAIL, W_B), F32), pltpu.VMEM((1, W_B), F32)]
        + [pltpu.VMEM((Tc + TAIL, BS_B), F32)] * 3,
        compiler_params=_cparams(("arbitrary", "arbitrary")),
        name="even_fused",
    )(x, g.reshape(1, -1), w_in, p["conv_w"], p["conv_b"], p["wa"], p["ba"], p["wx"], p["bx"], p["lam"], h0,
      _tail_rows(conv0), cosf, sinf, s0, gn.reshape(1, -1), w_out_a, w_out_b)


def _tail_rows(state):
    return jnp.pad(state, ((0, 0), (TAIL - state.shape[1], 0), (0, 0)))


def _lru_seq(z, p, h0, conv0, *, B, T, Tc, shared):
    NC = T // Tc
    smap = (lambda b, c: (0, 0, 0)) if shared else (lambda b, c: (b, 0, 0))
    row = lambda b, c: b * NC + c
    vec = lambda a: pl.BlockSpec((1, W_B), lambda b, c: (0, 0))
    wspec = pl.BlockSpec((NB_B, BS_B, BS_B), lambda b, c: (0, 0, 0))
    return pl.pallas_call(
        functools.partial(_lru_seq_body, Tc=Tc),
        grid=(B, NC),
        in_specs=[
            pl.BlockSpec((Tc, W_B), lambda b, c: (row(b, c), 3)),
            pl.BlockSpec((Tc, W_B), lambda b, c: (row(b, c), 4)),
            pl.BlockSpec((CONV_B, W_B), lambda b, c: (0, 0)), vec(0),
            wspec, vec(0), wspec, vec(0), vec(0),
            pl.BlockSpec((1, 1, W_B), smap),
            pl.BlockSpec((1, TAIL, W_B), smap),
        ],
        out_specs=[
            pl.BlockSpec((Tc, W_B), lambda b, c: (row(b, c), 0)),
            pl.BlockSpec((1, 1, W_B), lambda b, c: (b, 0, 0)),
            pl.BlockSpec((1, CONV_B - 1, W_B), lambda b, c: (b, 0, 0)),
        ],
        out_shape=[
            jax.ShapeDtypeStruct((B * T, W_B), BF16),
            jax.ShapeDtypeStruct((B, 1, W_B), F32),
            jax.ShapeDtypeStruct((B, CONV_B - 1, W_B), F32),
        ],
        scratch_shapes=[pltpu.VMEM((Tc + TAIL, W_B), F32), pltpu.VMEM((1, W_B), F32)]
        + [pltpu.VMEM((Tc + TAIL, BS_B), F32)] * 3,
        compiler_params=_cparams(("arbitrary", "arbitrary")),
        name="lru_seq",
    )(z, z, p["conv_w"], p["conv_b"], p["wa"], p["ba"], p["wx"], p["bx"], p["lam"], h0, _tail_rows(conv0))


def _lru_dec_body(xb_ref, gb_ref, s0_ref, s1_ref, s2_ref, cw_ref, cb_ref, wa_ref, ba_ref, wx_ref, bx_ref, lam_ref,
                  h0_ref, y_ref, hout_ref):
    for g in range(NB_B):
        cs = slice(g * BS_B, (g + 1) * BS_B)
        xc = (cb_ref[:, cs] + cw_ref[0:1, cs] * s0_ref[:, cs] + cw_ref[1:2, cs] * s1_ref[:, cs]
              + cw_ref[2:3, cs] * s2_ref[:, cs] + cw_ref[3:4, cs] * xb_ref[:, cs])
        a, u = _lru_gates(xc, wa_ref[g], ba_ref[:, cs], wx_ref[g], bx_ref[:, cs], lam_ref[:, cs])
        hs = a * h0_ref[:, cs] + u
        y_ref[:, cs] = (hs * jax.nn.gelu(gb_ref[:, cs])).astype(BF16)
        hout_ref[:, cs] = hs


def _lru_dec(z, p, h0, conv0):
    B = z.shape[0]
    full = lambda shape: pl.BlockSpec(shape, lambda i: (0,) * len(shape))
    return pl.pallas_call(
        _lru_dec_body,
        grid=(1,),
        in_specs=[
            pl.BlockSpec((B, W_B), lambda i: (0, 3)), pl.BlockSpec((B, W_B), lambda i: (0, 4)),
            full((B, W_B)), full((B, W_B)), full((B, W_B)),
            full((CONV_B, W_B)), full((1, W_B)),
            full((NB_B, BS_B, BS_B)), full((1, W_B)), full((NB_B, BS_B, BS_B)), full((1, W_B)), full((1, W_B)),
            full((B, W_B)),
        ],
        out_specs=[full((B, W_B)), full((B, W_B))],
        out_shape=[jax.ShapeDtypeStruct((B, W_B), BF16), jax.ShapeDtypeStruct((B, W_B), F32)],
        compiler_params=_cparams(("arbitrary",)),
        name="lru_dec",
    )(z, z, conv0[:, 0], conv0[:, 1], conv0[:, 2], p["conv_w"], p["conv_b"], p["wa"], p["ba"], p["wx"], p["bx"],
      p["lam"], h0)


def _ffn_seq_body(ug_ref, uv_ref, cw_ref, cb_ref, tail0_ref, m_ref, cout_ref, xe_ref, *, Tc):
    @pl.when(pl.program_id(1) == 0)
    def _():
        xe_ref[0:TAIL, :] = tail0_ref[0]

    x = ug_ref[...]
    xe_ref[TAIL:TAIL + Tc, :] = x
    cw = cw_ref[...]
    c = cb_ref[...]
    for j in range(CONV_F - 1):
        off = TAIL - (CONV_F - 1) + j
        c = c + cw[j:j + 1] * xe_ref[off:off + Tc, :]
    c = c + cw[CONV_F - 1:CONV_F] * x
    m_ref[...] = (jax.nn.gelu(c) * uv_ref[...]).astype(BF16)
    cout_ref[0] = xe_ref[Tc + TAIL - (CONV_F - 1):Tc + TAIL, :]
    xe_ref[0:TAIL, :] = xe_ref[Tc:Tc + TAIL, :]


def _ffn_seq(u, cw, cb, conv0, *, B, T, Tc, shared):
    NC = T // Tc
    smap = (lambda b, c: (0, 0, 0)) if shared else (lambda b, c: (b, 0, 0))
    row = lambda b, c: b * NC + c
    return pl.pallas_call(
        functools.partial(_ffn_seq_body, Tc=Tc),
        grid=(B, NC),
        in_specs=[
            pl.BlockSpec((Tc, D_FF), lambda b, c: (row(b, c), 0)),
            pl.BlockSpec((Tc, D_FF), lambda b, c: (row(b, c), 1)),
            pl.BlockSpec((CONV_F, D_FF), lambda b, c: (0, 0)),
            pl.BlockSpec((1, D_FF), lambda b, c: (0, 0)),
            pl.BlockSpec((1, TAIL, D_FF), smap),
        ],
        out_specs=[
            pl.BlockSpec((Tc, D_FF), lambda b, c: (row(b, c), 0)),
            pl.BlockSpec((1, CONV_F - 1, D_FF), lambda b, c: (b, 0, 0)),
        ],
        out_shape=[
            jax.ShapeDtypeStruct((B * T, D_FF), BF16),
            jax.ShapeDtypeStruct((B, CONV_F - 1, D_FF), F32),
        ],
        scratch_shapes=[pltpu.VMEM((Tc + TAIL, D_FF), F32)],
        compiler_params=_cparams(("arbitrary", "arbitrary")),
        name="ffn_seq",
    )(u, u, cw, cb, _tail_rows(conv0))


def _ffn_dec_body(ug_ref, uv_ref, s0_ref, s1_ref, cw_ref, cb_ref, m_ref):
    cw = cw_ref[...]
    c = cb_ref[...] + cw[0:1] * s0_ref[...] + cw[1:2] * s1_ref[...] + cw[2:3] * ug_ref[...]
    m_ref[...] = (jax.nn.gelu(c) * uv_ref[...]).astype(BF16)


def _ffn_dec(u, cw, cb, conv0):
    B = u.shape[0]
    full = lambda shape: pl.BlockSpec(shape, lambda i: (0,) * len(shape))
    return pl.pallas_call(
        _ffn_dec_body,
        grid=(1,),
        in_specs=[pl.BlockSpec((B, D_FF), lambda i: (0, 0)), pl.BlockSpec((B, D_FF), lambda i: (0, 1)),
                  full((B, D_FF)), full((B, D_FF)), full((CONV_F, D_FF)), full((1, D_FF))],
        out_specs=full((B, D_FF)),
        out_shape=jax.ShapeDtypeStruct((B, D_FF), BF16),
        compiler_params=_cparams(("arbitrary",)),
        name="ffn_dec",
    )(u, u, conv0[:, 0], conv0[:, 1], cw, cb)


def _ffn_fused_body(x_ref, g_ref, wup_ref, cw_ref, cb_ref, wdn_ref, tail0_ref, gout_ref, o_ref, cout_ref,
                    xn_ref, m_ref, tail_ref, *, tm, tc, out_norm):
    @pl.when(pl.program_id(1) == 0)
    def _():
        tail_ref[...] = tail0_ref[0]

    x = x_ref[...]
    xn_ref[...] = _rms(x, g_ref[...]).astype(BF16)
    row = lax.broadcasted_iota(jnp.int32, (tm, 1), 0)
    for j in range(D_FF // tc):
        cols = slice(j * tc, (j + 1) * tc)
        ug = _dot(xn_ref[...], wup_ref[:, cols])
        uv = _dot(xn_ref[...], wup_ref[:, D_FF + j * tc:D_FF + (j + 1) * tc])
        t1 = tail_ref[TAIL - 1:TAIL, cols]
        t2 = tail_ref[TAIL - 2:TAIL - 1, cols]
        s1 = jnp.where(row == 0, t1, pltpu.roll(ug, 1, 0))
        s2 = jnp.where(row == 0, t2, jnp.where(row == 1, t1, pltpu.roll(ug, 2, 0)))
        c = cb_ref[:, cols] + cw_ref[0:1, cols] * s2 + cw_ref[1:2, cols] * s1 + cw_ref[2:3, cols] * ug
        m_ref[:, cols] = (jax.nn.gelu(c) * uv).astype(BF16)
        tail_ref[:, cols] = ug[tm - TAIL:tm]
        cout_ref[0, :, cols] = ug[tm - (CONV_F - 1):tm]
    y = x + _dot(m_ref[...], wdn_ref[...])
    o_ref[...] = _rms(y, gout_ref[...]) if out_norm else y


def _ffn_fused(x, g, w_up, cw, cb, w_down, conv0, *, B, T, tm, tc, shared, g_out=None):
    NT = T // tm
    out_norm = g_out is not None
    g_out = g if g_out is None else g_out
    smap = (lambda b, t: (0, 0, 0)) if shared else (lambda b, t: (b, 0, 0))
    tile = pl.BlockSpec((tm, D_MODEL), lambda b, t: (b * NT + t, 0))
    const = lambda a: pl.BlockSpec(a.shape, lambda b, t: (0, 0), pipeline_mode=pl.Buffered(1))
    return pl.pallas_call(
        functools.partial(_ffn_fused_body, tm=tm, tc=tc, out_norm=out_norm),
        grid=(B, NT),
        in_specs=[tile, pl.BlockSpec((1, D_MODEL), lambda b, t: (0, 0)), const(w_up),
                  pl.BlockSpec((CONV_F, D_FF), lambda b, t: (0, 0)), pl.BlockSpec((1, D_FF), lambda b, t: (0, 0)),
                  const(w_down), pl.BlockSpec((1, TAIL, D_FF), smap),
                  pl.BlockSpec((1, D_MODEL), lambda b, t: (0, 0))],
        out_specs=[tile, pl.BlockSpec((1, CONV_F - 1, D_FF), lambda b, t: (b, 0, 0))],
        out_shape=[jax.ShapeDtypeStruct((B * T, D_MODEL), F32),
                   jax.ShapeDtypeStruct((B, CONV_F - 1, D_FF), F32)],
        scratch_shapes=[pltpu.VMEM((tm, D_MODEL), BF16), pltpu.VMEM((tm, D_FF), BF16),
                        pltpu.VMEM((TAIL, D_FF), F32)],
        compiler_params=_cparams(("arbitrary", "arbitrary")),
        name="ffn_fused",
    )(x, g.reshape(1, -1), w_up, cw, cb, w_down, _tail_rows(conv0), g_out.reshape(1, -1))


_PREP_W = ("mix", "w_r", "w_k", "w_v", "w0", "w1", "w2", "a0", "a1", "a2", "g1", "g2")
_PREP_VP = ("v0", "v1", "v2")


def _rwkv_prep_body(*refs, has_vp, carry):
    it = iter(refs)
    x_ref, gn_ref = next(it), next(it)
    p = {n: next(it) for n in _PREP_W}
    if has_vp:
        p.update({n: next(it) for n in _PREP_VP})
        vf_ref = next(it)
    prev_ref = next(it)
    r_ref, lw_ref, k_ref, v_ref, as_ref, g_ref, hn_ref = (next(it) for _ in range(7))
    h = _rms(x_ref[...], gn_ref[...])
    if carry:
        carry_ref = next(it)

        @pl.when(pl.program_id(1) == 0)
        def _():
            carry_ref[...] = prev_ref[0]

        row = lax.broadcasted_iota(jnp.int32, (h.shape[0], 1), 0)
        hprev = jnp.where(row == 0, carry_ref[...], pltpu.roll(h, 1, 0))
        carry_ref[...] = h[h.shape[0] - 1:]
        hn_ref[0] = h[h.shape[0] - 1:]
    else:
        hprev = prev_ref[...]
        hn_ref[...] = h
    xx = hprev - h
    mix = p["mix"][...]
    xs = [(h + xx * mix[n:n + 1]).astype(BF16) for n in range(6)]
    xr, xw, xk, xv, xa, xg = xs
    r_ref[...] = _dot(xr, p["w_r"][...])
    k_ref[...] = _dot(xk, p["w_k"][...])
    v = _dot(xv, p["w_v"][...])
    wl = p["w0"][...] + _dot(jnp.tanh(_dot(xw, p["w1"][...])).astype(BF16), p["w2"][...])
    lw_ref[...] = -jnp.exp(_log_sigmoid(wl) - 0.5)
    if has_vp:
        gate = jax.nn.sigmoid(p["v0"][...] + _dot(_dot(xv, p["v1"][...]).astype(BF16), p["v2"][...]))
        v = v + (vf_ref[...] - v) * gate
    v_ref[...] = v
    as_ref[...] = jax.nn.sigmoid(p["a0"][...] + _dot(_dot(xa, p["a1"][...]).astype(BF16), p["a2"][...]))
    g_ref[...] = _dot(jax.nn.sigmoid(_dot(xg, p["g1"][...])).astype(BF16), p["g2"][...])


def _rwkv_prep(x, gn, p, vp, v_first, prev, *, B, T, tm, carry, shared=False):
    R = x.shape[0]
    has_vp = vp is not None
    if carry:
        NT = T // tm
        grid = (B, NT)
        rowmap = lambda b, t: (b * NT + t, 0)
        cmap2 = lambda b, t: (0, 0)
        pmap = (lambda b, t: (0, 0, 0)) if shared else (lambda b, t: (b, 0, 0))
        prev_spec = pl.BlockSpec((1, 1, D_MODEL), pmap)
        hn_spec = pl.BlockSpec((1, 1, D_MODEL), lambda b, t: (b, 0, 0))
        hn_shape = jax.ShapeDtypeStruct((B, 1, D_MODEL), F32)
        sem = ("arbitrary", "arbitrary")
    else:
        grid = (R // tm,)
        rowmap = lambda i: (i, 0)
        cmap2 = lambda i: (0, 0)
        prev_spec = pl.BlockSpec((tm, D_MODEL), rowmap)
        hn_spec = pl.BlockSpec((tm, D_MODEL), rowmap)
        hn_shape = jax.ShapeDtypeStruct((R, D_MODEL), F32)
        sem = ("arbitrary",)
    tile = pl.BlockSpec((tm, D_MODEL), rowmap)
    ins = [x, gn.reshape(1, -1)]
    specs = [tile, pl.BlockSpec((1, D_MODEL), cmap2)]
    names = _PREP_W + (_PREP_VP if has_vp else ())
    src = dict(p)
    if has_vp:
        src.update(vp)
    for n in names:
        ins.append(src[n])
        specs.append(pl.BlockSpec(src[n].shape, cmap2))
    if has_vp:
        ins.append(v_first)
        specs.append(tile)
    ins.append(prev)
    specs.append(prev_spec)
    outs = pl.pallas_call(
        functools.partial(_rwkv_prep_body, has_vp=has_vp, carry=carry),
        grid=grid,
        in_specs=specs,
        out_specs=[tile] * 6 + [hn_spec],
        out_shape=[jax.ShapeDtypeStruct((R, D_MODEL), F32)] * 6 + [hn_shape],
        scratch_shapes=[pltpu.VMEM((1, D_MODEL), F32)] if carry else [],
        compiler_params=_cparams(sem),
        name="rwkv_prep",
    )(*ins)
    return outs


def _rwkv_chunk_body(r_ref, lw_ref, k_ref, v_ref, as_ref, g_ref, kk_ref, ka_ref, rk_ref, gg_ref, gb_ref, s0_ref,
                     o_ref, s_ref, *, Tc, C, n_pair, unroll):
    @pl.when(pl.program_id(2) == 0)
    def _():
        s_ref[...] = s0_ref[...]

    C2 = 2 * C
    lane = lax.broadcasted_iota(jnp.int32, (1, LANES), 1)
    m0 = (lane < HS_C).astype(F32)
    m1 = 1.0 - m0
    li = lax.broadcasted_iota(jnp.int32, (LANES, LANES), 0)
    lj = lax.broadcasted_iota(jnp.int32, (LANES, LANES), 1)
    same_head = (li >= HS_C) == (lj >= HS_C)
    bd_ones = same_head.astype(F32).astype(BF16)
    bd_avg = (same_head.astype(F32) * (1.0 / HS_C)).astype(BF16)
    ti = lax.broadcasted_iota(jnp.int32, (C, C), 0)
    si = lax.broadcasted_iota(jnp.int32, (C, C), 1)
    tri = (ti >= si).astype(F32).astype(BF16)
    tri2 = jnp.concatenate([tri, tri], axis=1)
    r2 = lax.broadcasted_iota(jnp.int32, (C2, C2), 0)
    c2 = lax.broadcasted_iota(jnp.int32, (C2, C2), 1)
    same = (r2 >= C) == (c2 >= C)
    tt, ss = r2 & (C - 1), c2 & (C - 1)
    m_strict = same & (tt > ss)
    m_incl = same & (tt >= ss)
    same_sub = (tt & -RWKV_SUB) == (ss & -RWKV_SUB)
    m_sub = m_strict & same_sub
    m_off = m_strict & jnp.logical_not(same_sub)
    eye = (r2 == c2).astype(F32)
    nb = C // RWKV_SUB
    stack = lambda x: jnp.concatenate([x * m0, x * m1], axis=0)
    dup = lambda x: jnp.concatenate([x, x], axis=0)

    def sub_chunk(ci, carry):
        row_l = [pl.ds(pl.multiple_of((ci * unroll + u) * C, C), C) for u in range(unroll)]
        rows = [rw for rw in row_l for _ in range(n_pair)]
        lanes = [slice(p * LANES, (p + 1) * LANES) for p in range(n_pair)] * unroll
        each = lambda f, *cols: [f(*xs) for xs in zip(*cols)]

        def load(rw, ls):
            r, lw, k, v = r_ref[rw, ls], lw_ref[rw, ls], k_ref[rw, ls], v_ref[rw, ls]
            asig = as_ref[rw, ls]
            kk = k * kk_ref[:, ls]
            k2 = k * (1.0 + (asig - 1.0) * ka_ref[:, ls])
            return r, lw, k2, v, asig, kk

        r, lw, k2, v, asig, kk = zip(*each(load, rows, lanes))
        ssq = each(lambda x: _dot(_bf(x * x), bd_ones), kk)
        cs = each(lambda x: _dot(tri2, jnp.concatenate(_split(x), axis=0)), lw)
        kk = each(lambda x, q: x * lax.rsqrt(jnp.maximum(q, 1e-24)), kk, ssq)
        b = each(lambda x, q: x * q, kk, asig)
        cend = [x[C - 1:C] for x in cs]
        e_neg = each(lambda x: jnp.exp(-x), cs)
        e_end = each(lambda x, y: jnp.exp(y - x), cs, cend)
        As = each(lambda x, c, l: _split(stack(-x * jnp.exp(c - l))), kk, cs, lw)
        Bd = each(lambda x, e: _split(dup(x * e)), b, e_neg)
        Rs = each(lambda x, c: _bf(stack(x * jnp.exp(c))), r, cs)
        Vs = each(lambda x: _bf(stack(x)), v)
        Kd = each(lambda x, e: _bf(dup(x * e)), k2, e_neg)
        G = each(_dot3_wide_nt, As, Bd)
        Aak = each(lambda x, y: _bf(jnp.where(m_strict, _dot_nt(x[0], y), 0.0)), As, Kd)
        Arb = each(lambda x, y: _bf(jnp.where(m_incl, _dot_nt(x, y[0]), 0.0)), Rs, Bd)
        Ark = each(lambda x, y: _bf(jnp.where(m_incl, _dot_nt(x, y), 0.0)), Rs, Kd)
        Nd = each(lambda x: jnp.where(m_sub, x, 0.0), G)
        P = each(lambda x: eye + x, Nd)
        Qs = each(_split, Nd)
        for _ in range(3):
            Qs = each(lambda q: _split(_dot3_wide(q, q)), Qs)
            P = each(lambda x, q: x + _dot3_wide(_split(x), q), P, Qs)
        if nb > 1:
            Pb = each(_bf, P)
            M = each(lambda x, y: _dot(x, _bf(jnp.where(m_off, y, 0.0))), Pb, G)
            Tm = each(lambda x: eye + x, M)
            for _ in range(int(np.ceil(np.log2(nb))) - 1):
                M = each(lambda x: _dot(_bf(x), _bf(x)), M)
                Tm = each(lambda x, y: x + _dot(_bf(x), _bf(y)), Tm, M)
            Tinv = each(lambda x, y: _bf(_dot(_bf(x), y)), Tm, Pb)
        else:
            Tinv = each(_bf, P)
        bE = each(lambda x, e: _bf(stack(x * e)), b, e_end)
        kE = each(lambda x, e: _bf(stack(x * e)), k2, e_end)
        gC = each(jnp.exp, cend)
        Ys = []
        for u in range(unroll):
            sl = slice(u * n_pair, (u + 1) * n_pair)
            S = [s_ref[0, p] for p in range(n_pair)]
            Sb = each(_bf, S)
            X = each(lambda a_, s_, k_, v_: _bf(_dot_nt(a_[0], s_) + _dot(k_, v_)), As[sl], Sb, Aak[sl], Vs[sl])
            Us = each(lambda t_, x_: _bf(_dot(t_, x_)), Tinv[sl], X)
            Ys += each(lambda r_, s_, b_, u_, k_, v_: _dot_nt(r_, s_) + _dot(b_, u_) + _dot(k_, v_),
                       Rs[sl], Sb, Arb[sl], Us, Ark[sl], Vs[sl])
            Sn = each(lambda s_, g_, u_, b_, v_, k_: s_ * g_ + _dot_tn(u_, b_) + _dot_tn(v_, k_),
                      S, gC[sl], Us, bE[sl], Vs[sl], kE[sl])
            for p in range(n_pair):
                s_ref[0, p] = Sn[p]
        y = [x[:C] + x[C:] for x in Ys]
        mu = each(lambda x: _dot(_bf(x), bd_avg), y)
        d = each(lambda x, m: x - m, y, mu)
        var = each(lambda x: _dot(_bf(x * x), bd_avg), d)
        bonus = each(lambda r_, k_, ls: _dot(_bf(r_ * k_ * rk_ref[:, ls]), bd_ones), r, k2, lanes)
        for i, (rw, ls) in enumerate(zip(rows, lanes)):
            on = d[i] * lax.rsqrt(var[i] + GN_EPS_C) * gg_ref[:, ls] + gb_ref[:, ls]
            o_ref[rw, ls] = ((on + bonus[i] * v[i]) * g_ref[rw, ls]).astype(BF16)
        return carry

    lax.fori_loop(0, Tc // (C * unroll), sub_chunk, 0)


def _rwkv_chunk(arrs, p, s0, *, B, T, Tc, C, ppb, shared, unroll=1):
    NC = T // Tc
    NP = N_PAIR // ppb
    Wb = ppb * LANES
    tile = pl.BlockSpec((Tc, Wb), lambda b, q, c: (b * NC + c, q))
    vec = pl.BlockSpec((1, Wb), lambda b, q, c: (0, q))
    smap = (lambda b, q, c: (0, q, 0, 0)) if shared else (lambda b, q, c: (b, q, 0, 0))
    return pl.pallas_call(
        functools.partial(_rwkv_chunk_body, Tc=Tc, C=C, n_pair=ppb, unroll=unroll),
        grid=(B, NP, NC),
        in_specs=[tile] * 6 + [vec] * 5 + [pl.BlockSpec((1, ppb, LANES, LANES), smap)],
        out_specs=[tile, pl.BlockSpec((1, ppb, LANES, LANES), lambda b, q, c: (b, q, 0, 0))],
        out_shape=[jax.ShapeDtypeStruct((B * T, D_MODEL), BF16),
                   jax.ShapeDtypeStruct((B, N_PAIR, LANES, LANES), F32)],
        compiler_params=_cparams(("arbitrary", "arbitrary", "arbitrary")),
        name="rwkv_chunk",
    )(*arrs, p["k_k"], p["k_a"], p["r_k"], p["gn_g"], p["gn_b"], s0)


def _rwkv_dec_body(r_ref, lw_ref, k_ref, v_ref, as_ref, g_ref, kk_ref, ka_ref, rk_ref, gg_ref, gb_ref, s0_ref,
                   o_ref, s_ref, r_scr, w_scr, k2_scr, v_scr, a_scr, b_scr, y_scr, *, nb, layer=None):
    s_ref = _own_layer(s_ref, layer)
    n_head = 2 * N_PAIR
    ri = lax.broadcasted_iota(jnp.int32, (HS_C, HS_C), 0)
    ci = lax.broadcasted_iota(jnp.int32, (HS_C, HS_C), 1)
    eye = (ri == ci).astype(F32)
    for h in range(n_head):
        ls = slice(h * HS_C, (h + 1) * HS_C)
        k, asig = k_ref[:, ls], as_ref[:, ls]
        kk = k * kk_ref[:, ls]
        kk = kk * lax.rsqrt(jnp.maximum(jnp.sum(kk * kk, axis=1, keepdims=True), 1e-24))
        a_scr[h] = -kk
        b_scr[h] = kk * asig
        k2_scr[h] = k * (1.0 + (asig - 1.0) * ka_ref[:, ls])
        r_scr[h] = r_ref[:, ls]
        v_scr[h] = v_ref[:, ls]
        w_scr[h] = jnp.exp(lw_ref[:, ls])

    def per_sample(s, carry):
        row = pl.ds(s, 1)
        heads = range(n_head)
        S = [s0_ref[s, h] for h in heads]
        sa = [jnp.sum(S[h] * a_scr[h, row, :], axis=1, keepdims=True) for h in heads]
        v_col = [jnp.sum(eye * v_scr[h, row, :], axis=1, keepdims=True) for h in heads]
        Sn = [S[h] * w_scr[h, row, :] + sa[h] * b_scr[h, row, :] + v_col[h] * k2_scr[h, row, :] for h in heads]
        for h in heads:
            s_ref[s, h] = Sn[h]
        y_col = [jnp.sum(Sn[h] * r_scr[h, row, :], axis=1, keepdims=True) for h in heads]
        for h in heads:
            y_scr[h, row, :] = jnp.sum(eye * y_col[h], axis=0, keepdims=True)
        return carry

    lax.fori_loop(0, nb, per_sample, 0)
    for h in range(n_head):
        ls = slice(h * HS_C, (h + 1) * HS_C)
        y = y_scr[h]
        d = y - jnp.mean(y, axis=1, keepdims=True)
        var = jnp.mean(d * d, axis=1, keepdims=True)
        on = d * lax.rsqrt(var + GN_EPS_C) * gg_ref[:, ls] + gb_ref[:, ls]
        bonus = jnp.sum(r_scr[h] * k2_scr[h] * rk_ref[:, ls], axis=1, keepdims=True) * v_scr[h]
        o_ref[:, ls] = (on + bonus) * g_ref[:, ls]


def _rwkv_dec(arrs, p, s0, *, nb, layer, prev=None):
    B = s0.shape[1]
    n_head = 2 * N_PAIR
    tile = pl.BlockSpec((nb, D_MODEL), lambda i: (i, 0))
    vec = pl.BlockSpec((1, D_MODEL), lambda i: (0, 0))
    sspec = pl.BlockSpec((None, nb, n_head, HS_C, HS_C), lambda i: (layer, i, 0, 0, 0))
    ins = [*arrs, p["k_k"], p["k_a"], p["r_k"], p["gn_g"], p["gn_b"], s0]
    specs = [tile] * 6 + [vec] * 5 + [sspec]
    s_out = sspec
    if prev is None:
        s_out = pl.BlockSpec((s0.shape[0], nb, n_head, HS_C, HS_C), lambda i: (0, i, 0, 0, 0))
    body = functools.partial(_rwkv_dec_body, nb=nb, layer=layer if prev is None else None)
    aliases = {}
    if prev is not None:
        ins.append(prev)
        specs.append(pl.BlockSpec(memory_space=pl.ANY))
        aliases = {len(ins) - 1: 1}
        body = _drop_ref(body, len(ins) - 1)
    return pl.pallas_call(
        body,
        grid=(B // nb,),
        in_specs=specs,
        out_specs=[tile, s_out],
        out_shape=[jax.ShapeDtypeStruct((B, D_MODEL), F32), jax.ShapeDtypeStruct(s0.shape, F32)],
        scratch_shapes=[pltpu.VMEM((n_head, nb, HS_C), F32)] * 7,
        input_output_aliases=aliases,
        compiler_params=_cparams(("arbitrary",)),
        name="rwkv_dec",
    )(*ins)


def _from_blockdiag(s):
    B = s.shape[0]
    return jnp.stack([s[:, :, :HS_C, :HS_C], s[:, :, HS_C:, HS_C:]], axis=2).reshape(B, 2 * N_PAIR, HS_C, HS_C)


MAIN_TM = 2048
MAIN_TN = 512
RET_C = 128
LRU_TC = 256
FFN_TC = 256
FFN_TM = 512
FFN_TCOL = 256
PREP_TM = 256
RWKV_TC = 256
RWKV_C = 64
RWKV_PPB = 8
RWKV_UNROLL = 2
RWKV_DEC_NB = 8
RET_DEC_NB = 8


def kernel(x_prompt, x_sample, state_ret, state_lru, state_lru_conv, state_rwkv, state_shift, state_ffn_conv,
           meta_tokens, norm_mix_g, norm_ffn_g, norm_final_g,
           ev_w_in, ev_ret_gn_g, ev_lru_conv_w, ev_lru_conv_b, ev_lru_wa, ev_lru_ba, ev_lru_wx, ev_lru_bx,
           ev_lru_lambda, ev_w_out,
           od_mix, od_w_r, od_w_k, od_w_v, od_w0, od_w1, od_w2, od_a0, od_a1, od_a2, od_v0, od_v1, od_v2,
           od_g1, od_g2, od_k_k, od_k_a, od_r_k, od_gn_g, od_gn_b, od_w_o,
           ff_w_up, ff_conv_w, ff_conv_b, ff_w_down):
    BP, TP, _ = x_prompt.shape
    BS = x_sample.shape[0]
    NS = BS + N_META
    bf = lambda a: a.astype(BF16)
    row = lambda a: a.reshape(1, -1)

    def lora_in(w):
        return bf(jnp.pad(w, ((0, 0), (0, LORA_PAD - w.shape[1]))))

    def lora_out(w):
        return bf(jnp.pad(w, ((0, LORA_PAD - w.shape[0]), (0, 0))))

    xm = x_prompt.reshape(BP * TP, D_MODEL)
    xs = jnp.concatenate([x_sample.reshape(BS, D_MODEL), meta_tokens.astype(x_prompt.dtype)], axis=0)

    cos_m, sin_m = _rope_tables(N_META + jnp.arange(TP, dtype=jnp.int32))
    cos_t, sin_t = _rope_tables(jnp.arange(N_META, dtype=jnp.int32))
    cos_s, sin_s = _rope_tables(PAST_LEN + jnp.arange(PAD_T, dtype=jnp.int32))

    out = {k: [] for k in ("ret_p", "lru_p", "lconv_p", "rwkv_p", "shift_p", "ffn_p",
                           "ret_s", "lru_s", "lconv_s", "rwkv_s", "shift_s", "ffn_s")}
    v_first_m = v_first_s = None
    ret_s_all = rwkv_s_all = None
    for li in range(4):
        j = li // 2
        if li % 2 == 0:
            w_in = bf(ev_w_in[j])
            w_out_a, w_out_b = bf(ev_w_out[j][:VG_W]), bf(ev_w_out[j][VG_W:])
            gn = ev_ret_gn_g[j]
            lp = dict(conv_w=ev_lru_conv_w[j], conv_b=row(ev_lru_conv_b[j]), wa=bf(ev_lru_wa[j]),
                      ba=row(ev_lru_ba[j]), wx=bf(ev_lru_wx[j]), bx=row(ev_lru_bx[j]), lam=row(ev_lru_lambda[j]))
            zs = _mm([xs], [w_in], g=norm_mix_g[li], tm=NS, tn=MAIN_TN)
            z_smp, z_meta = zs[:BS], zs[BS:]
            ya_t, ret_t = _retention(z_meta, cos_t, sin_t, jnp.zeros((1, H_A, DK_A, DV_A), F32), gn,
                                     B=1, T=N_META, C=N_META, c_true=N_META, shared=False)
            yb_t, lru_t, lconv_t = _lru_seq(z_meta, lp, jnp.zeros((1, 1, W_B), F32),
                                            jnp.zeros((1, CONV_B - 1, W_B), F32), B=1, T=N_META, Tc=N_META,
                                            shared=False)
            ya_s, ret_s_all = _retention(z_smp, cos_s, sin_s, state_ret, gn, B=BS, T=1, C=PAD_T, c_true=1,
                                         shared=False, bb=RET_DEC_NB, single=True, layer=j, prev=ret_s_all)
            yb_s, lru_s = _lru_dec(z_smp, lp, state_lru[j], state_lru_conv[j])
            lconv_s = jnp.concatenate([state_lru_conv[j][:, 1:], z_smp[:, None, 3 * W_B:4 * W_B]], axis=1)
            ya = jnp.concatenate([ya_s, ya_t.astype(F32)], axis=0)
            yb = jnp.concatenate([yb_s, yb_t], axis=0)
            xs = _mm([ya, yb], [w_out_a, w_out_b], res=xs, tm=NS, tn=MAIN_TN)
            xm, lru_m, lconv_m, ret_m = _even_fused(xm, norm_mix_g[li], w_in, lp, lru_t, lconv_t, cos_m, sin_m, ret_t,
                                                    gn, w_out_a, w_out_b, B=BP, T=TP, Tc=LRU_TC, tn=MAIN_TN,
                                                    C=RET_C)
            out["ret_p"].append(ret_m)
            out["lru_p"].append(lru_m[:, 0])
            out["lconv_p"].append(lconv_m)
            out["lru_s"].append(lru_s)
            out["lconv_s"].append(lconv_s)
        else:
            p = dict(mix=od_mix[j], w_r=bf(od_w_r[j]), w_k=bf(od_w_k[j]), w_v=bf(od_w_v[j]), w0=row(od_w0[j]),
                     w1=lora_in(od_w1[j]), w2=lora_out(od_w2[j]), a0=row(od_a0[j]), a1=lora_in(od_a1[j]),
                     a2=lora_out(od_a2[j]), g1=lora_in(od_g1[j]), g2=lora_out(od_g2[j]),
                     k_k=row(od_k_k[j]), k_a=row(od_k_a[j]), r_k=row(od_r_k[j]), gn_g=row(od_gn_g[j]),
                     gn_b=row(od_gn_b[j]))
            vp = None
            if j > 0:
                vp = dict(v0=row(od_v0[j - 1]), v1=lora_in(od_v1[j - 1]), v2=lora_out(od_v2[j - 1]))
            w_o = bf(od_w_o[j])
            gmix = norm_mix_g[li]
            pre_t = _rwkv_prep(xs[BS:], gmix, p, vp, None if vp is None else v_first_s[BS:],
                               jnp.zeros((1, 1, D_MODEL), F32), B=1, T=N_META, tm=N_META, carry=True)
            o_t, rw_t = _rwkv_chunk(pre_t[:6], p, jnp.zeros((1, N_PAIR, LANES, LANES), F32),
                                    B=1, T=N_META, Tc=N_META, C=N_META, ppb=RWKV_PPB, shared=False)
            shift_t = pre_t[6]
            pre_s = _rwkv_prep(xs[:BS], gmix, p, vp, None if vp is None else v_first_s[:BS],
                               state_shift[j], B=BS, T=1, tm=BS, carry=False)
            o_s, rwkv_s_all = _rwkv_dec(pre_s[:6], p, state_rwkv, nb=RWKV_DEC_NB, layer=j, prev=rwkv_s_all)
            if vp is None:
                v_first_s = jnp.concatenate([pre_s[3], pre_t[3]], axis=0)
            o_small = jnp.concatenate([o_s, o_t.astype(F32)], axis=0)
            xs = _mm([o_small], [w_o], res=xs, tm=NS, tn=MAIN_TN)
            pre_m = _rwkv_prep(xm, gmix, p, vp, v_first_m, shift_t, B=BP, T=TP, tm=PREP_TM, carry=True, shared=True)
            if vp is None:
                v_first_m = pre_m[3]
            o_m, rw_m = _rwkv_chunk(pre_m[:6], p, rw_t, B=BP, T=TP, Tc=RWKV_TC, C=RWKV_C, ppb=RWKV_PPB, shared=True,
                                    unroll=RWKV_UNROLL)
            xm = _mm([o_m], [w_o], res=xm, tm=MAIN_TM, tn=MAIN_TN)
            out["rwkv_p"].append(_from_blockdiag(rw_m))
            out["shift_p"].append(pre_m[6][:, 0])
            out["shift_s"].append(pre_s[6])
        w_up, w_down = bf(ff_w_up[li]), bf(ff_w_down[li])
        cw, cb = ff_conv_w[li], row(ff_conv_b[li])
        us = _mm([xs], [w_up], g=norm_ffn_g[li], tm=NS, tn=MAIN_TN)
        m_t, ffn_t = _ffn_seq(us[BS:], cw, cb, jnp.zeros((1, CONV_F - 1, D_FF), F32), B=1, T=N_META, Tc=N_META,
                              shared=False)
        m_s = _ffn_dec(us[:BS], cw, cb, state_ffn_conv[li])
        ffn_s = jnp.concatenate([state_ffn_conv[li][:, 1:], us[:BS, None, :D_FF]], axis=1)
        xs = _mm([jnp.concatenate([m_s, m_t], axis=0)], [w_down], res=xs, tm=NS, tn=MAIN_TN)
        xm, ffn_m = _ffn_fused(xm, norm_ffn_g[li], w_up, cw, cb, w_down, ffn_t, B=BP, T=TP, tm=FFN_TM, tc=FFN_TCOL,
                               shared=True, g_out=norm_final_g if li == 3 else None)
        out["ffn_p"].append(ffn_m)
        out["ffn_s"].append(ffn_s)

    y_prompt = xm.reshape(BP, TP, D_MODEL)
    y_sample = _final_norm(xs, norm_final_g, NS)[:BS].reshape(BS, 1, D_MODEL)
    st = lambda k: jnp.stack(out[k])
    return (y_prompt, y_sample,
            st("ret_p"), st("lru_p"), st("lconv_p"), st("rwkv_p"), st("shift_p"), st("ffn_p"),
            ret_s_all, st("lru_s"), st("lconv_s"), rwkv_s_all, st("shift_s"), st("ffn_s"))
```

```python
import functools

import numpy as np
import jax
import jax.numpy as jnp
from jax import lax
from jax.experimental import pallas as pl
from jax.experimental.pallas import tpu as pltpu

F32 = jnp.float32
BF16 = jnp.bfloat16
HI = lax.Precision.HIGHEST

D_MODEL = 1024
N_META = 16
PAST_LEN = 16384
H_A, DK_A, DV_A = 4, 128, 256
W_B, NB_B, BS_B, CONV_B = 1024, 8, 128, 4
LRU_C = 8.0
HS_C = 64
N_PAIR = D_MODEL // (2 * HS_C)
LORA_PAD = 128
GN_EPS_C = 64e-5
D_FF, CONV_F = 2816, 3
EPS = 1e-6
ROPE_BASE = 10000.0
QK_W, VG_W = H_A * DK_A, H_A * DV_A

LANES = 128
TAIL = 8
VMEM_LIMIT = 52 * 1024 * 1024
PAD_T = 16
RWKV_SUB = 16


def _cparams(sem):
    return pltpu.CompilerParams(dimension_semantics=sem, vmem_limit_bytes=VMEM_LIMIT)


def _dot(a, b, prec=None):
    return jnp.dot(a, b, preferred_element_type=F32, precision=prec)


def _dot_nt(a, b, prec=None):
    return lax.dot_general(a, b, (((1,), (1,)), ((), ())), preferred_element_type=F32, precision=prec)


def _dot_tn(a, b, prec=None):
    return lax.dot_general(a, b, (((0,), (0,)), ((), ())), preferred_element_type=F32, precision=prec)


def _bf(x):
    return x.astype(BF16)


def _split(x):
    hi = x.astype(BF16)
    return hi, (x - hi.astype(F32)).astype(BF16)


def _dot3(a, b, dot):
    return dot(a[0], b[0]) + dot(a[0], b[1]) + dot(a[1], b[0])


def _dot3_wide(a, b):
    n = a[0].shape[1]
    lhs = jnp.concatenate([a[0], a[1]], axis=1)
    rhs = jnp.concatenate([jnp.concatenate([b[0], b[1]], axis=1),
                           jnp.concatenate([b[0], jnp.zeros_like(b[0])], axis=1)], axis=0)
    out = _dot(lhs, rhs)
    return out[:, :n] + out[:, n:]


def _dot3_wide_nt(a, b):
    n = b[0].shape[0]
    lhs = jnp.concatenate([a[0], a[1]], axis=1)
    rhs = jnp.concatenate([jnp.concatenate([b[0], b[0]], axis=1),
                           jnp.concatenate([b[1], jnp.zeros_like(b[1])], axis=1)], axis=0)
    out = _dot_nt(lhs, rhs)
    return out[:, :n] + out[:, n:]


def _rms(x, g):
    return x * lax.rsqrt(jnp.mean(x * x, axis=-1, keepdims=True) + EPS) * g


def _log_sigmoid(x):
    return jnp.minimum(x, 0.0) - jnp.log(1.0 + jnp.exp(-jnp.abs(x)))


def _mm_body(*refs, n_in, norm, res):
    it = iter(refs)
    x_refs = [next(it) for _ in range(n_in)]
    g_ref = next(it) if norm else None
    w_refs = [next(it) for _ in range(n_in)]
    r_ref = next(it) if res else None
    o_ref = next(it)
    if norm:
        xn_ref = next(it)

        @pl.when(pl.program_id(1) == 0)
        def _():
            xn_ref[...] = _rms(x_refs[0][...], g_ref[...]).astype(BF16)

        acc = _dot(xn_ref[...], w_refs[0][...])
    else:
        acc = _dot(x_refs[0][...].astype(BF16), w_refs[0][...])
        for x_ref, w_ref in zip(x_refs[1:], w_refs[1:]):
            acc = acc + _dot(x_ref[...].astype(BF16), w_ref[...])
    if res:
        acc = acc + r_ref[...]
    o_ref[...] = acc.astype(o_ref.dtype)


def _mm(xs, ws, *, g=None, res=None, tm, tn, out_dtype=F32):
    R = xs[0].shape[0]
    N = ws[0].shape[1]
    norm = g is not None
    ins, specs = [], []
    for x in xs:
        ins.append(x)
        specs.append(pl.BlockSpec((tm, x.shape[1]), lambda i, j: (i, 0)))
    if norm:
        ins.append(g.reshape(1, -1))
        specs.append(pl.BlockSpec((1, g.shape[-1]), lambda i, j: (0, 0)))
    for w in ws:
        ins.append(w)
        specs.append(pl.BlockSpec((w.shape[0], tn), lambda i, j: (0, j)))
    if res is not None:
        ins.append(res)
        specs.append(pl.BlockSpec((tm, tn), lambda i, j: (i, j)))
    scratch = [pltpu.VMEM((tm, xs[0].shape[1]), BF16)] if norm else []
    return pl.pallas_call(
        functools.partial(_mm_body, n_in=len(xs), norm=norm, res=res is not None),
        grid=(R // tm, N // tn),
        in_specs=specs,
        out_specs=pl.BlockSpec((tm, tn), lambda i, j: (i, j)),
        out_shape=jax.ShapeDtypeStruct((R, N), out_dtype),
        scratch_shapes=scratch,
        compiler_params=_cparams(("arbitrary", "arbitrary")),
        name="mm",
    )(*ins)


def _final_norm_body(x_ref, g_ref, o_ref):
    o_ref[...] = _rms(x_ref[...], g_ref[...])


def _final_norm(x, g, tm):
    R = x.shape[0]
    return pl.pallas_call(
        _final_norm_body,
        grid=(R // tm,),
        in_specs=[pl.BlockSpec((tm, D_MODEL), lambda i: (i, 0)), pl.BlockSpec((1, D_MODEL), lambda i: (0, 0))],
        out_specs=pl.BlockSpec((tm, D_MODEL), lambda i: (i, 0)),
        out_shape=jax.ShapeDtypeStruct((R, D_MODEL), F32),
        compiler_params=_cparams(("arbitrary",)),
        name="final_norm",
    )(x, g.reshape(1, -1))


def _ret_heads(q_ref, k_ref, v_ref, ga_ref, offs, chunks, cos_ref, sin_ref, s_ref, gn_ref, y_ref, *, C, c_true, single,
               after_chunk=None):
    qo, ko, vo, go = offs
    ti = lax.broadcasted_iota(jnp.int32, (C, C), 0)
    si = lax.broadcasted_iota(jnp.int32, (C, C), 1)
    dif = (ti - si).astype(F32)
    trow = lax.broadcasted_iota(jnp.int32, (C, 1), 0)
    tcol = trow.astype(F32)
    chunk = (lambda x: jnp.where(trow == 0, x, 0.0)) if single else (lambda x: x)
    heads = range(H_A)
    lg = [float(np.log1p(-(2.0 ** (-5.0 - h)))) for h in heads]
    mask = [jnp.where(dif >= 0, jnp.exp(lg[h] * jnp.maximum(dif, 0.0)), 0.0) for h in heads]
    dec_in = [jnp.exp((tcol + 1.0) * lg[h]) for h in heads]
    dec_k = [jnp.exp((c_true - 1.0 - tcol) * lg[h]) for h in heads]
    qc = lambda h: slice(qo + h * DK_A, qo + (h + 1) * DK_A)
    kc = lambda h: slice(ko + h * DK_A, ko + (h + 1) * DK_A)
    vc = lambda h: slice(vo + h * DV_A, vo + (h + 1) * DV_A)
    gc = lambda h: slice(go + h * DV_A, go + (h + 1) * DV_A)
    yc = lambda h: slice(h * DV_A, (h + 1) * DV_A)
    for ci, (rs, sidx, ts) in enumerate(chunks):
        cosf, sinf = cos_ref[ts, :], sin_ref[ts, :]
        rot = lambda x: x * cosf + pltpu.roll(x, DK_A // 2, 1) * sinf
        q = [rot(chunk(q_ref[rs, qc(h)])) for h in heads]
        k = [rot(chunk(k_ref[rs, kc(h)])) * (DK_A ** -0.5) for h in heads]
        vb = [chunk(v_ref[rs, vc(h)]).astype(BF16) for h in heads]
        qb = [x.astype(BF16) for x in q]
        sc = [(_dot_nt(qb[h], k[h].astype(BF16)) * mask[h]).astype(BF16) for h in heads]
        S = [s_ref[sidx, h] for h in heads]
        o = [_dot(sc[h], vb[h]) + _dot(qb[h], S[h].astype(BF16)) * dec_in[h] for h in heads]
        for h in heads:
            s_ref[sidx, h] = float(np.exp(c_true * lg[h])) * S[h] + _dot_tn((k[h] * dec_k[h]).astype(BF16), vb[h])
        d = [x - jnp.mean(x, axis=-1, keepdims=True) for x in o]
        var = [jnp.mean(x * x, axis=-1, keepdims=True) for x in d]
        for h in heads:
            on = d[h] * lax.rsqrt(var[h] + EPS) * gn_ref[:, yc(h)]
            ga = ga_ref[rs, gc(h)]
            if single:
                on = on[0:1]
            y_ref[rs, yc(h)] = (ga * jax.nn.sigmoid(ga) * on).astype(y_ref.dtype)
        if after_chunk is not None:
            after_chunk(ci, rs)


def _own_layer(s_ref, layer):
    if layer is None:
        return s_ref
    for other in range(s_ref.shape[0]):
        if other != layer:
            s_ref[other] = jnp.zeros(s_ref.shape[1:], s_ref.dtype)
    return s_ref.at[layer]


def _ret_body(q_ref, k_ref, v_ref, ga_ref, cos_ref, sin_ref, s0_ref, gn_ref, y_ref, s_ref, *, C, c_true, bb, single,
              layer=None):
    s_ref = _own_layer(s_ref, layer)

    @pl.when(pl.program_id(1) == 0)
    def _():
        s_ref[...] = s0_ref[...]

    rows = lambda s: slice(s, s + 1) if single else slice(s * C, (s + 1) * C)
    _ret_heads(q_ref, k_ref, v_ref, ga_ref, (0, 0, 0, 0), [(rows(s), s, slice(None)) for s in range(bb)],
               cos_ref, sin_ref, s_ref, gn_ref, y_ref, C=C, c_true=c_true, single=single)


def _drop_ref(body, i):
    def wrapped(*refs, **kw):
        return body(*refs[:i], *refs[i + 1:], **kw)
    return wrapped


def _retention(z, cosf, sinf, s0, gn, *, B, T, C, c_true, shared, bb=1, single=False, layer=None, prev=None):
    NC = 1 if single else T // C
    assert bb == 1 or (NC == 1 and not shared)
    assert not single or (T == 1 and c_true == 1)
    row = lambda b, c: b * NC + c
    R = bb if single else bb * C
    if layer is None:
        smap = (lambda b, c: (0, 0, 0, 0)) if shared else (lambda b, c: (b, 0, 0, 0))
        s_in = pl.BlockSpec((bb, H_A, DK_A, DV_A), smap)
        s_out = pl.BlockSpec((bb, H_A, DK_A, DV_A), lambda b, c: (b, 0, 0, 0))
        s_shape = (B, H_A, DK_A, DV_A)
    else:
        s_in = s_out = pl.BlockSpec((None, bb, H_A, DK_A, DV_A), lambda b, c: (layer, b, 0, 0, 0))
        if prev is None:
            s_out = pl.BlockSpec((s0.shape[0], bb, H_A, DK_A, DV_A), lambda b, c: (0, b, 0, 0, 0))
        s_shape = s0.shape
    ins = [z, z, z, z, cosf, sinf, s0, gn.reshape(1, -1)]
    specs = [
        pl.BlockSpec((R, QK_W), lambda b, c: (row(b, c), 0)),
        pl.BlockSpec((R, QK_W), lambda b, c: (row(b, c), 1)),
        pl.BlockSpec((R, VG_W), lambda b, c: (row(b, c), 1)),
        pl.BlockSpec((R, VG_W), lambda b, c: (row(b, c), 2)),
        pl.BlockSpec((C, DK_A), lambda b, c: (c, 0)),
        pl.BlockSpec((C, DK_A), lambda b, c: (c, 0)),
        s_in,
        pl.BlockSpec((1, VG_W), lambda b, c: (0, 0)),
    ]
    body = functools.partial(_ret_body, C=C, c_true=c_true, bb=bb, single=single,
                             layer=layer if prev is None else None)
    aliases = {}
    if prev is not None:
        ins.append(prev)
        specs.append(pl.BlockSpec(memory_space=pl.ANY))
        aliases = {len(ins) - 1: 1}
        body = _drop_ref(body, len(ins) - 1)
    return pl.pallas_call(
        body,
        grid=(B // bb, NC),
        in_specs=specs,
        out_specs=[pl.BlockSpec((R, VG_W), lambda b, c: (row(b, c), 0)), s_out],
        out_shape=[
            jax.ShapeDtypeStruct((B * T, VG_W), F32 if single else BF16),
            jax.ShapeDtypeStruct(s_shape, F32),
        ],
        input_output_aliases=aliases,
        compiler_params=_cparams(("arbitrary", "arbitrary")),
        name="retention",
    )(*ins)


def _rope_tables(pos):
    half = DK_A // 2
    inv = ROPE_BASE ** (-jnp.linspace(0.0, 1.0, half, dtype=F32))
    ang = pos.astype(F32)[:, None] * inv[None, :]
    cos, sin = jnp.cos(ang), jnp.sin(ang)
    return jnp.concatenate([cos, cos], axis=-1), jnp.concatenate([-sin, sin], axis=-1)


def _lru_gates(xc, wa, ba, wx, bx, lam):
    xg = xc.astype(BF16)
    r = jax.nn.sigmoid(_dot(xg, wa) + ba)
    i = jax.nn.sigmoid(_dot(xg, wx) + bx)
    log_a = (-LRU_C) * r * (-_log_sigmoid(lam))
    a = jnp.exp(log_a)
    u = jnp.sqrt(1.0 - a * a) * (i * xc)
    return a, u


def _lru_tile(x_of, gb_of, prm, h0_ref, tail0_ref, y_ref, hout_ref, cout_ref, xe_ref, hc_ref, scan_refs, Tc,
              between=None):
    cw_ref, cb_ref, wa_ref, ba_ref, wx_ref, bx_ref, lam_ref = prm
    a_scr, u_scr, h_scr = scan_refs
    seg_len = Tc // TAIL
    pitch = seg_len + 1

    @pl.when(pl.program_id(1) == 0)
    def _():
        xe_ref[0:TAIL, :] = tail0_ref[0]
        hc_ref[...] = h0_ref[0]

    for g in range(NB_B):
        cs = slice(g * BS_B, (g + 1) * BS_B)
        x = x_of(g)
        xe_ref[TAIL:TAIL + Tc, cs] = x
        xc = cb_ref[:, cs]
        for j in range(CONV_B - 1):
            off = TAIL - (CONV_B - 1) + j
            xc = xc + cw_ref[j:j + 1, cs] * xe_ref[off:off + Tc, cs]
        xc = xc + cw_ref[CONV_B - 1:CONV_B, cs] * x
        a, u = _lru_gates(xc, wa_ref[g], ba_ref[:, cs], wx_ref[g], bx_ref[:, cs], lam_ref[:, cs])
        for s in range(TAIL):
            a_scr[s * pitch:s * pitch + seg_len, :] = a[s * seg_len:(s + 1) * seg_len]
            u_scr[s * pitch:s * pitch + seg_len, :] = u[s * seg_len:(s + 1) * seg_len]
        step = lambda ref, p: ref[pl.ds(p, TAIL, stride=pitch), :]
        a_tot, u_tot = step(a_scr, 0), step(u_scr, 0)
        for p in range(1, seg_len):
            a_p = step(a_scr, p)
            u_tot = a_p * u_tot + step(u_scr, p)
            a_tot = a_p * a_tot
        h_in = hc_ref[:, cs]
        starts = [h_in]
        for s in range(TAIL - 1):
            h_in = a_tot[s:s + 1] * h_in + u_tot[s:s + 1]
            starts.append(h_in)
        h_last = a_tot[TAIL - 1:TAIL] * h_in + u_tot[TAIL - 1:TAIL]
        h = jnp.concatenate(starts, axis=0)
        for p in range(seg_len):
            h = step(a_scr, p) * h + step(u_scr, p)
            h_scr[pl.ds(p, TAIL, stride=pitch), :] = h
        hs = jnp.concatenate([h_scr[s * pitch:s * pitch + seg_len, :] for s in range(TAIL)], axis=0)
        y_ref[:, cs] = (hs * jax.nn.gelu(gb_of(g))).astype(BF16)
        hc_ref[:, cs] = h_last
        hout_ref[0, :, cs] = h_last
        cout_ref[0, :, cs] = xe_ref[Tc + TAIL - (CONV_B - 1):Tc + TAIL, cs]
        xe_ref[0:TAIL, cs] = xe_ref[Tc:Tc + TAIL, cs]
        if between is not None:
            between(g)


def _lru_seq_body(xb_ref, gb_ref, cw_ref, cb_ref, wa_ref, ba_ref, wx_ref, bx_ref, lam_ref, h0_ref, tail0_ref,
                  y_ref, hout_ref, cout_ref, xe_ref, hc_ref, a_scr, u_scr, h_scr, *, Tc):
    blk = lambda ref: (lambda g: ref[:, g * BS_B:(g + 1) * BS_B])
    _lru_tile(blk(xb_ref), blk(gb_ref), (cw_ref, cb_ref, wa_ref, ba_ref, wx_ref, bx_ref, lam_ref),
              h0_ref, tail0_ref, y_ref, hout_ref, cout_ref, xe_ref, hc_ref, (a_scr, u_scr, h_scr), Tc)


QKVG_W = 2 * QK_W + 2 * VG_W


def _even_body(x_ref, g_ref, w_ref, cw_ref, cb_ref, wa_ref, ba_ref, wx_ref, bx_ref, lam_ref, h0_ref, tail0_ref,
               cos_ref, sin_ref, s0_ref, gn_ref, woa_ref, wob_ref, o_ref, hout_ref, cout_ref, s_ref,
               xn_ref, zq_ref, ya_ref, yb_ref, xe_ref, hc_ref, a_scr, u_scr, h_scr, *, Tc, tn, C):
    @pl.when(pl.program_id(1) == 0)
    def _():
        s_ref[...] = s0_ref[...]

    xn_ref[...] = _rms(x_ref[...], g_ref[...]).astype(BF16)
    pairs = {}

    def col(start):
        def block(g):
            key = (start, g // 2)
            if key not in pairs:
                lo = start + (g // 2) * 2 * BS_B
                pairs[key] = _dot(xn_ref[...], w_ref[:, lo:lo + 2 * BS_B])
            return pairs[key][:, (g % 2) * BS_B:(g % 2 + 1) * BS_B]
        return block

    def qkvg_chunk(c):
        if c < QKVG_W // tn:
            zq_ref[:, c * tn:(c + 1) * tn] = _dot(xn_ref[...], w_ref[:, c * tn:(c + 1) * tn])

    _lru_tile(col(QKVG_W), col(QKVG_W + W_B), (cw_ref, cb_ref, wa_ref, ba_ref, wx_ref, bx_ref, lam_ref),
              h0_ref, tail0_ref, yb_ref, hout_ref, cout_ref, xe_ref, hc_ref, (a_scr, u_scr, h_scr), Tc,
              between=qkvg_chunk)
    o_ref[...] = x_ref[...] + _dot(yb_ref[...], wob_ref[...])
    chunks = [(slice(c * C, (c + 1) * C), 0, slice(c * C, (c + 1) * C)) for c in range(Tc // C)]
    def project_rows(ci, rs):
        o_ref[rs, :] += _dot(ya_ref[rs, :], woa_ref[...])

    _ret_heads(zq_ref, zq_ref, zq_ref, zq_ref, (0, QK_W, 2 * QK_W, 2 * QK_W + VG_W), chunks, cos_ref, sin_ref,
               s_ref, gn_ref, ya_ref, C=C, c_true=C, single=False, after_chunk=project_rows)


def _even_fused(x, g, w_in, p, h0, conv0, cosf, sinf, s0, gn, w_out_a, w_out_b, *, B, T, Tc, tn, C):
    NC = T // Tc
    shared3 = lambda b, c: (0, 0, 0)
    row = lambda b, c: (b * NC + c, 0)
    vec = pl.BlockSpec((1, W_B), lambda b, c: (0, 0))
    wspec = pl.BlockSpec((NB_B, BS_B, BS_B), lambda b, c: (0, 0, 0))
    const = lambda a: pl.BlockSpec(a.shape, lambda b, c: (0, 0), pipeline_mode=pl.Buffered(1))
    tile = pl.BlockSpec((Tc, D_MODEL), row)
    return pl.pallas_call(
        functools.partial(_even_body, Tc=Tc, tn=tn, C=C),
        grid=(B, NC),
        in_specs=[
            tile, pl.BlockSpec((1, D_MODEL), lambda b, c: (0, 0)), const(w_in),
            pl.BlockSpec((CONV_B, W_B), lambda b, c: (0, 0)), vec, wspec, vec, wspec, vec, vec,
            pl.BlockSpec((1, 1, W_B), shared3), pl.BlockSpec((1, TAIL, W_B), shared3),
            pl.BlockSpec((Tc, DK_A), lambda b, c: (c, 0)), pl.BlockSpec((Tc, DK_A), lambda b, c: (c, 0)),
            pl.BlockSpec((1, H_A, DK_A, DV_A), lambda b, c: (0, 0, 0, 0)),
            pl.BlockSpec((1, VG_W), lambda b, c: (0, 0)), const(w_out_a), const(w_out_b),
        ],
        out_specs=[
            tile,
            pl.BlockSpec((1, 1, W_B), lambda b, c: (b, 0, 0)),
            pl.BlockSpec((1, CONV_B - 1, W_B), lambda b, c: (b, 0, 0)),
            pl.BlockSpec((1, H_A, DK_A, DV_A), lambda b, c: (b, 0, 0, 0)),
        ],
        out_shape=[
            jax.ShapeDtypeStruct((B * T, D_MODEL), F32),
            jax.ShapeDtypeStruct((B, 1, W_B), F32),
            jax.ShapeDtypeStruct((B, CONV_B - 1, W_B), F32),
            jax.ShapeDtypeStruct((B, H_A, DK_A, DV_A), F32),
        ],
        scratch_shapes=[pltpu.VMEM((Tc, D_MODEL), BF16), pltpu.VMEM((Tc, QKVG_W), F32),
                        pltpu.VMEM((Tc, VG_W), BF16), pltpu.VMEM((Tc, W_B), BF16),
                        pltpu.VMEM((Tc + TAIL, W_B), F32), pltpu.VMEM((1, W_B), F32)]
        + [pltpu.VMEM((Tc + TAIL, BS_B), F32)] * 3,
        compiler_params=_cparams(("arbitrary", "arbitrary")),
        name="even_fused",
    )(x, g.reshape(1, -1), w_in, p["conv_w"], p["conv_b"], p["wa"], p["ba"], p["wx"], p["bx"], p["lam"], h0,
      _tail_rows(conv0), cosf, sinf, s0, gn.reshape(1, -1), w_out_a, w_out_b)


def _tail_rows(state):
    return jnp.pad(state, ((0, 0), (TAIL - state.shape[1], 0), (0, 0)))


def _lru_seq(z, p, h0, conv0, *, B, T, Tc, shared):
    NC = T // Tc
    smap = (lambda b, c: (0, 0, 0)) if shared else (lambda b, c: (b, 0, 0))
    row = lambda b, c: b * NC + c
    vec = lambda a: pl.BlockSpec((1, W_B), lambda b, c: (0, 0))
    wspec = pl.BlockSpec((NB_B, BS_B, BS_B), lambda b, c: (0, 0, 0))
    return pl.pallas_call(
        functools.partial(_lru_seq_body, Tc=Tc),
        grid=(B, NC),
        in_specs=[
            pl.BlockSpec((Tc, W_B), lambda b, c: (row(b, c), 3)),
            pl.BlockSpec((Tc, W_B), lambda b, c: (row(b, c), 4)),
            pl.BlockSpec((CONV_B, W_B), lambda b, c: (0, 0)), vec(0),
            wspec, vec(0), wspec, vec(0), vec(0),
            pl.BlockSpec((1, 1, W_B), smap),
            pl.BlockSpec((1, TAIL, W_B), smap),
        ],
        out_specs=[
            pl.BlockSpec((Tc, W_B), lambda b, c: (row(b, c), 0)),
            pl.BlockSpec((1, 1, W_B), lambda b, c: (b, 0, 0)),
            pl.BlockSpec((1, CONV_B - 1, W_B), lambda b, c: (b, 0, 0)),
        ],
        out_shape=[
            jax.ShapeDtypeStruct((B * T, W_B), BF16),
            jax.ShapeDtypeStruct((B, 1, W_B), F32),
            jax.ShapeDtypeStruct((B, CONV_B - 1, W_B), F32),
        ],
        scratch_shapes=[pltpu.VMEM((Tc + TAIL, W_B), F32), pltpu.VMEM((1, W_B), F32)]
        + [pltpu.VMEM((Tc + TAIL, BS_B), F32)] * 3,
        compiler_params=_cparams(("arbitrary", "arbitrary")),
        name="lru_seq",
    )(z, z, p["conv_w"], p["conv_b"], p["wa"], p["ba"], p["wx"], p["bx"], p["lam"], h0, _tail_rows(conv0))


def _lru_dec_body(xb_ref, gb_ref, s0_ref, s1_ref, s2_ref, cw_ref, cb_ref, wa_ref, ba_ref, wx_ref, bx_ref, lam_ref,
                  h0_ref, y_ref, hout_ref):
    for g in range(NB_B):
        cs = slice(g * BS_B, (g + 1) * BS_B)
        xc = (cb_ref[:, cs] + cw_ref[0:1, cs] * s0_ref[:, cs] + cw_ref[1:2, cs] * s1_ref[:, cs]
              + cw_ref[2:3, cs] * s2_ref[:, cs] + cw_ref[3:4, cs] * xb_ref[:, cs])
        a, u = _lru_gates(xc, wa_ref[g], ba_ref[:, cs], wx_ref[g], bx_ref[:, cs], lam_ref[:, cs])
        hs = a * h0_ref[:, cs] + u
        y_ref[:, cs] = (hs * jax.nn.gelu(gb_ref[:, cs])).astype(BF16)
        hout_ref[:, cs] = hs


def _lru_dec(z, p, h0, conv0):
    B = z.shape[0]
    full = lambda shape: pl.BlockSpec(shape, lambda i: (0,) * len(shape))
    return pl.pallas_call(
        _lru_dec_body,
        grid=(1,),
        in_specs=[
            pl.BlockSpec((B, W_B), lambda i: (0, 3)), pl.BlockSpec((B, W_B), lambda i: (0, 4)),
            full((B, W_B)), full((B, W_B)), full((B, W_B)),
            full((CONV_B, W_B)), full((1, W_B)),
            full((NB_B, BS_B, BS_B)), full((1, W_B)), full((NB_B, BS_B, BS_B)), full((1, W_B)), full((1, W_B)),
            full((B, W_B)),
        ],
        out_specs=[full((B, W_B)), full((B, W_B))],
        out_shape=[jax.ShapeDtypeStruct((B, W_B), BF16), jax.ShapeDtypeStruct((B, W_B), F32)],
        compiler_params=_cparams(("arbitrary",)),
        name="lru_dec",
    )(z, z, conv0[:, 0], conv0[:, 1], conv0[:, 2], p["conv_w"], p["conv_b"], p["wa"], p["ba"], p["wx"], p["bx"],
      p["lam"], h0)


def _ffn_seq_body(ug_ref, uv_ref, cw_ref, cb_ref, tail0_ref, m_ref, cout_ref, xe_ref, *, Tc):
    @pl.when(pl.program_id(1) == 0)
    def _():
        xe_ref[0:TAIL, :] = tail0_ref[0]

    x = ug_ref[...]
    xe_ref[TAIL:TAIL + Tc, :] = x
    cw = cw_ref[...]
    c = cb_ref[...]
    for j in range(CONV_F - 1):
        off = TAIL - (CONV_F - 1) + j
        c = c + cw[j:j + 1] * xe_ref[off:off + Tc, :]
    c = c + cw[CONV_F - 1:CONV_F] * x
    m_ref[...] = (jax.nn.gelu(c) * uv_ref[...]).astype(BF16)
    cout_ref[0] = xe_ref[Tc + TAIL - (CONV_F - 1):Tc + TAIL, :]
    xe_ref[0:TAIL, :] = xe_ref[Tc:Tc + TAIL, :]


def _ffn_seq(u, cw, cb, conv0, *, B, T, Tc, shared):
    NC = T // Tc
    smap = (lambda b, c: (0, 0, 0)) if shared else (lambda b, c: (b, 0, 0))
    row = lambda b, c: b * NC + c
    return pl.pallas_call(
        functools.partial(_ffn_seq_body, Tc=Tc),
        grid=(B, NC),
        in_specs=[
            pl.BlockSpec((Tc, D_FF), lambda b, c: (row(b, c), 0)),
            pl.BlockSpec((Tc, D_FF), lambda b, c: (row(b, c), 1)),
            pl.BlockSpec((CONV_F, D_FF), lambda b, c: (0, 0)),
            pl.BlockSpec((1, D_FF), lambda b, c: (0, 0)),
            pl.BlockSpec((1, TAIL, D_FF), smap),
        ],
        out_specs=[
            pl.BlockSpec((Tc, D_FF), lambda b, c: (row(b, c), 0)),
            pl.BlockSpec((1, CONV_F - 1, D_FF), lambda b, c: (b, 0, 0)),
        ],
        out_shape=[
            jax.ShapeDtypeStruct((B * T, D_FF), BF16),
            jax.ShapeDtypeStruct((B, CONV_F - 1, D_FF), F32),
        ],
        scratch_shapes=[pltpu.VMEM((Tc + TAIL, D_FF), F32)],
        compiler_params=_cparams(("arbitrary", "arbitrary")),
        name="ffn_seq",
    )(u, u, cw, cb, _tail_rows(conv0))


def _ffn_dec_body(ug_ref, uv_ref, s0_ref, s1_ref, cw_ref, cb_ref, m_ref):
    cw = cw_ref[...]
    c = cb_ref[...] + cw[0:1] * s0_ref[...] + cw[1:2] * s1_ref[...] + cw[2:3] * ug_ref[...]
    m_ref[...] = (jax.nn.gelu(c) * uv_ref[...]).astype(BF16)


def _ffn_dec(u, cw, cb, conv0):
    B = u.shape[0]
    full = lambda shape: pl.BlockSpec(shape, lambda i: (0,) * len(shape))
    return pl.pallas_call(
        _ffn_dec_body,
        grid=(1,),
        in_specs=[pl.BlockSpec((B, D_FF), lambda i: (0, 0)), pl.BlockSpec((B, D_FF), lambda i: (0, 1)),
                  full((B, D_FF)), full((B, D_FF)), full((CONV_F, D_FF)), full((1, D_FF))],
        out_specs=full((B, D_FF)),
        out_shape=jax.ShapeDtypeStruct((B, D_FF), BF16),
        compiler_params=_cparams(("arbitrary",)),
        name="ffn_dec",
    )(u, u, conv0[:, 0], conv0[:, 1], cw, cb)


def _ffn_fused_body(x_ref, g_ref, wup_ref, cw_ref, cb_ref, wdn_ref, tail0_ref, gout_ref, o_ref, cout_ref,
                    xn_ref, m_ref, tail_ref, *, tm, tc, out_norm):
    @pl.when(pl.program_id(1) == 0)
    def _():
        tail_ref[...] = tail0_ref[0]

    x = x_ref[...]
    xn_ref[...] = _rms(x, g_ref[...]).astype(BF16)
    row = lax.broadcasted_iota(jnp.int32, (tm, 1), 0)
    for j in range(D_FF // tc):
        cols = slice(j * tc, (j + 1) * tc)
        ug = _dot(xn_ref[...], wup_ref[:, cols])
        uv = _dot(xn_ref[...], wup_ref[:, D_FF + j * tc:D_FF + (j + 1) * tc])
        t1 = tail_ref[TAIL - 1:TAIL, cols]
        t2 = tail_ref[TAIL - 2:TAIL - 1, cols]
        s1 = jnp.where(row == 0, t1, pltpu.roll(ug, 1, 0))
        s2 = jnp.where(row == 0, t2, jnp.where(row == 1, t1, pltpu.roll(ug, 2, 0)))
        c = cb_ref[:, cols] + cw_ref[0:1, cols] * s2 + cw_ref[1:2, cols] * s1 + cw_ref[2:3, cols] * ug
        m_ref[:, cols] = (jax.nn.gelu(c) * uv).astype(BF16)
        tail_ref[:, cols] = ug[tm - TAIL:tm]
        cout_ref[0, :, cols] = ug[tm - (CONV_F - 1):tm]
    y = x + _dot(m_ref[...], wdn_ref[...])
    o_ref[...] = _rms(y, gout_ref[...]) if out_norm else y


def _ffn_fused(x, g, w_up, cw, cb, w_down, conv0, *, B, T, tm, tc, shared, g_out=None):
    NT = T // tm
    out_norm = g_out is not None
    g_out = g if g_out is None else g_out
    smap = (lambda b, t: (0, 0, 0)) if shared else (lambda b, t: (b, 0, 0))
    tile = pl.BlockSpec((tm, D_MODEL), lambda b, t: (b * NT + t, 0))
    const = lambda a: pl.BlockSpec(a.shape, lambda b, t: (0, 0), pipeline_mode=pl.Buffered(1))
    return pl.pallas_call(
        functools.partial(_ffn_fused_body, tm=tm, tc=tc, out_norm=out_norm),
        grid=(B, NT),
        in_specs=[tile, pl.BlockSpec((1, D_MODEL), lambda b, t: (0, 0)), const(w_up),
                  pl.BlockSpec((CONV_F, D_FF), lambda b, t: (0, 0)), pl.BlockSpec((1, D_FF), lambda b, t: (0, 0)),
                  const(w_down), pl.BlockSpec((1, TAIL, D_FF), smap),
                  pl.BlockSpec((1, D_MODEL), lambda b, t: (0, 0))],
        out_specs=[tile, pl.BlockSpec((1, CONV_F - 1, D_FF), lambda b, t: (b, 0, 0))],
        out_shape=[jax.ShapeDtypeStruct((B * T, D_MODEL), F32),
                   jax.ShapeDtypeStruct((B, CONV_F - 1, D_FF), F32)],
        scratch_shapes=[pltpu.VMEM((tm, D_MODEL), BF16), pltpu.VMEM((tm, D_FF), BF16),
                        pltpu.VMEM((TAIL, D_FF), F32)],
        compiler_params=_cparams(("arbitrary", "arbitrary")),
        name="ffn_fused",
    )(x, g.reshape(1, -1), w_up, cw, cb, w_down, _tail_rows(conv0), g_out.reshape(1, -1))


_PREP_W = ("mix", "w_r", "w_k", "w_v", "w0", "w1", "w2", "a0", "a1", "a2", "g1", "g2")
_PREP_VP = ("v0", "v1", "v2")


def _rwkv_prep_body(*refs, has_vp, carry):
    it = iter(refs)
    x_ref, gn_ref = next(it), next(it)
    p = {n: next(it) for n in _PREP_W}
    if has_vp:
        p.update({n: next(it) for n in _PREP_VP})
        vf_ref = next(it)
    prev_ref = next(it)
    r_ref, lw_ref, k_ref, v_ref, as_ref, g_ref, hn_ref = (next(it) for _ in range(7))
    h = _rms(x_ref[...], gn_ref[...])
    if carry:
        carry_ref = next(it)

        @pl.when(pl.program_id(1) == 0)
        def _():
            carry_ref[...] = prev_ref[0]

        row = lax.broadcasted_iota(jnp.int32, (h.shape[0], 1), 0)
        hprev = jnp.where(row == 0, carry_ref[...], pltpu.roll(h, 1, 0))
        carry_ref[...] = h[h.shape[0] - 1:]
        hn_ref[0] = h[h.shape[0] - 1:]
    else:
        hprev = prev_ref[...]
        hn_ref[...] = h
    xx = hprev - h
    mix = p["mix"][...]
    xs = [(h + xx * mix[n:n + 1]).astype(BF16) for n in range(6)]
    xr, xw, xk, xv, xa, xg = xs
    r_ref[...] = _dot(xr, p["w_r"][...])
    k_ref[...] = _dot(xk, p["w_k"][...])
    v = _dot(xv, p["w_v"][...])
    wl = p["w0"][...] + _dot(jnp.tanh(_dot(xw, p["w1"][...])).astype(BF16), p["w2"][...])
    lw_ref[...] = -jnp.exp(_log_sigmoid(wl) - 0.5)
    if has_vp:
        gate = jax.nn.sigmoid(p["v0"][...] + _dot(_dot(xv, p["v1"][...]).astype(BF16), p["v2"][...]))
        v = v + (vf_ref[...] - v) * gate
    v_ref[...] = v
    as_ref[...] = jax.nn.sigmoid(p["a0"][...] + _dot(_dot(xa, p["a1"][...]).astype(BF16), p["a2"][...]))
    g_ref[...] = _dot(jax.nn.sigmoid(_dot(xg, p["g1"][...])).astype(BF16), p["g2"][...])


def _rwkv_prep(x, gn, p, vp, v_first, prev, *, B, T, tm, carry, shared=False):
    R = x.shape[0]
    has_vp = vp is not None
    if carry:
        NT = T // tm
        grid = (B, NT)
        rowmap = lambda b, t: (b * NT + t, 0)
        cmap2 = lambda b, t: (0, 0)
        pmap = (lambda b, t: (0, 0, 0)) if shared else (lambda b, t: (b, 0, 0))
        prev_spec = pl.BlockSpec((1, 1, D_MODEL), pmap)
        hn_spec = pl.BlockSpec((1, 1, D_MODEL), lambda b, t: (b, 0, 0))
        hn_shape = jax.ShapeDtypeStruct((B, 1, D_MODEL), F32)
        sem = ("arbitrary", "arbitrary")
    else:
        grid = (R // tm,)
        rowmap = lambda i: (i, 0)
        cmap2 = lambda i: (0, 0)
        prev_spec = pl.BlockSpec((tm, D_MODEL), rowmap)
        hn_spec = pl.BlockSpec((tm, D_MODEL), rowmap)
        hn_shape = jax.ShapeDtypeStruct((R, D_MODEL), F32)
        sem = ("arbitrary",)
    tile = pl.BlockSpec((tm, D_MODEL), rowmap)
    ins = [x, gn.reshape(1, -1)]
    specs = [tile, pl.BlockSpec((1, D_MODEL), cmap2)]
    names = _PREP_W + (_PREP_VP if has_vp else ())
    src = dict(p)
    if has_vp:
        src.update(vp)
    for n in names:
        ins.append(src[n])
        specs.append(pl.BlockSpec(src[n].shape, cmap2))
    if has_vp:
        ins.append(v_first)
        specs.append(tile)
    ins.append(prev)
    specs.append(prev_spec)
    outs = pl.pallas_call(
        functools.partial(_rwkv_prep_body, has_vp=has_vp, carry=carry),
        grid=grid,
        in_specs=specs,
        out_specs=[tile] * 6 + [hn_spec],
        out_shape=[jax.ShapeDtypeStruct((R, D_MODEL), F32)] * 6 + [hn_shape],
        scratch_shapes=[pltpu.VMEM((1, D_MODEL), F32)] if carry else [],
        compiler_params=_cparams(sem),
        name="rwkv_prep",
    )(*ins)
    return outs


def _rwkv_chunk_body(r_ref, lw_ref, k_ref, v_ref, as_ref, g_ref, kk_ref, ka_ref, rk_ref, gg_ref, gb_ref, s0_ref,
                     *rest, Tc, C, n_pair, unroll, project):
    if project:
        x_ref, wo_ref, xo_ref, s_ref, o_ref = rest
    else:
        o_ref, s_ref = rest

    @pl.when(pl.program_id(2) == 0)
    def _():
        s_ref[...] = s0_ref[...]

    C2 = 2 * C
    lane = lax.broadcasted_iota(jnp.int32, (1, LANES), 1)
    m0 = (lane < HS_C).astype(F32)
    m1 = 1.0 - m0
    li = lax.broadcasted_iota(jnp.int32, (LANES, LANES), 0)
    lj = lax.broadcasted_iota(jnp.int32, (LANES, LANES), 1)
    same_head = (li >= HS_C) == (lj >= HS_C)
    bd_ones = same_head.astype(F32).astype(BF16)
    bd_avg = (same_head.astype(F32) * (1.0 / HS_C)).astype(BF16)
    ti = lax.broadcasted_iota(jnp.int32, (C, C), 0)
    si = lax.broadcasted_iota(jnp.int32, (C, C), 1)
    tri = (ti >= si).astype(F32).astype(BF16)
    tri2 = jnp.concatenate([tri, tri], axis=1)
    r2 = lax.broadcasted_iota(jnp.int32, (C2, C2), 0)
    c2 = lax.broadcasted_iota(jnp.int32, (C2, C2), 1)
    same = (r2 >= C) == (c2 >= C)
    tt, ss = r2 & (C - 1), c2 & (C - 1)
    m_strict = same & (tt > ss)
    m_incl = same & (tt >= ss)
    same_sub = (tt & -RWKV_SUB) == (ss & -RWKV_SUB)
    m_sub = m_strict & same_sub
    m_off = m_strict & jnp.logical_not(same_sub)
    eye = (r2 == c2).astype(F32)
    nb = C // RWKV_SUB
    stack = lambda x: jnp.concatenate([x * m0, x * m1], axis=0)
    dup = lambda x: jnp.concatenate([x, x], axis=0)

    def sub_chunk(ci, carry):
        row_l = [pl.ds(pl.multiple_of((ci * unroll + u) * C, C), C) for u in range(unroll)]
        rows = [rw for rw in row_l for _ in range(n_pair)]
        lanes = [slice(p * LANES, (p + 1) * LANES) for p in range(n_pair)] * unroll
        each = lambda f, *cols: [f(*xs) for xs in zip(*cols)]

        def load(rw, ls):
            r, lw, k, v = r_ref[rw, ls], lw_ref[rw, ls], k_ref[rw, ls], v_ref[rw, ls]
            asig = as_ref[rw, ls]
            kk = k * kk_ref[:, ls]
            k2 = k * (1.0 + (asig - 1.0) * ka_ref[:, ls])
            return r, lw, k2, v, asig, kk

        r, lw, k2, v, asig, kk = zip(*each(load, rows, lanes))
        ssq = each(lambda x: _dot(_bf(x * x), bd_ones), kk)
        cs = each(lambda x: _dot(tri2, jnp.concatenate(_split(x), axis=0)), lw)
        kk = each(lambda x, q: x * lax.rsqrt(jnp.maximum(q, 1e-24)), kk, ssq)
        b = each(lambda x, q: x * q, kk, asig)
        cend = [x[C - 1:C] for x in cs]
        e_neg = each(lambda x: jnp.exp(-x), cs)
        e_end = each(lambda x, y: jnp.exp(y - x), cs, cend)
        As = each(lambda x, c, l: _split(stack(-x * jnp.exp(c - l))), kk, cs, lw)
        Bd = each(lambda x, e: _split(dup(x * e)), b, e_neg)
        Rs = each(lambda x, c: _bf(stack(x * jnp.exp(c))), r, cs)
        Vs = each(lambda x: _bf(stack(x)), v)
        Kd = each(lambda x, e: _bf(dup(x * e)), k2, e_neg)
        G = each(_dot3_wide_nt, As, Bd)
        Aak = each(lambda x, y: _bf(jnp.where(m_strict, _dot_nt(x[0], y), 0.0)), As, Kd)
        Arb = each(lambda x, y: _bf(jnp.where(m_incl, _dot_nt(x, y[0]), 0.0)), Rs, Bd)
        Ark = each(lambda x, y: _bf(jnp.where(m_incl, _dot_nt(x, y), 0.0)), Rs, Kd)
        Nd = each(lambda x: jnp.where(m_sub, x, 0.0), G)
        P = each(lambda x: eye + x, Nd)
        Qs = each(_split, Nd)
        for _ in range(3):
            Qs = each(lambda q: _split(_dot3_wide(q, q)), Qs)
            P = each(lambda x, q: x + _dot3_wide(_split(x), q), P, Qs)
        if nb > 1:
            Pb = each(_bf, P)
            M = each(lambda x, y: _dot(x, _bf(jnp.where(m_off, y, 0.0))), Pb, G)
            Tm = each(lambda x: eye + x, M)
            for _ in range(int(np.ceil(np.log2(nb))) - 1):
                M = each(lambda x: _dot(_bf(x), _bf(x)), M)
                Tm = each(lambda x, y: x + _dot(_bf(x), _bf(y)), Tm, M)
            Tinv = each(lambda x, y: _bf(_dot(_bf(x), y)), Tm, Pb)
        else:
            Tinv = each(_bf, P)
        bE = each(lambda x, e: _bf(stack(x * e)), b, e_end)
        kE = each(lambda x, e: _bf(stack(x * e)), k2, e_end)
        gC = each(jnp.exp, cend)
        Ys = []
        for u in range(unroll):
            sl = slice(u * n_pair, (u + 1) * n_pair)
            S = [s_ref[0, p] for p in range(n_pair)]
            Sb = each(_bf, S)
            X = each(lambda a_, s_, k_, v_: _bf(_dot_nt(a_[0], s_) + _dot(k_, v_)), As[sl], Sb, Aak[sl], Vs[sl])
            Us = each(lambda t_, x_: _bf(_dot(t_, x_)), Tinv[sl], X)
            Ys += each(lambda r_, s_, b_, u_, k_, v_: _dot_nt(r_, s_) + _dot(b_, u_) + _dot(k_, v_),
                       Rs[sl], Sb, Arb[sl], Us, Ark[sl], Vs[sl])
            Sn = each(lambda s_, g_, u_, b_, v_, k_: s_ * g_ + _dot_tn(u_, b_) + _dot_tn(v_, k_),
                      S, gC[sl], Us, bE[sl], Vs[sl], kE[sl])
            for p in range(n_pair):
                s_ref[0, p] = Sn[p]
        y = [x[:C] + x[C:] for x in Ys]
        mu = each(lambda x: _dot(_bf(x), bd_avg), y)
        d = each(lambda x, m: x - m, y, mu)
        var = each(lambda x: _dot(_bf(x * x), bd_avg), d)
        bonus = each(lambda r_, k_, ls: _dot(_bf(r_ * k_ * rk_ref[:, ls]), bd_ones), r, k2, lanes)
        for i, (rw, ls) in enumerate(zip(rows, lanes)):
            on = d[i] * lax.rsqrt(var[i] + GN_EPS_C) * gg_ref[:, ls] + gb_ref[:, ls]
            o_ref[rw, ls] = ((on + bonus[i] * v[i]) * g_ref[rw, ls]).astype(BF16)
        return carry

    lax.fori_loop(0, Tc // (C * unroll), sub_chunk, 0)
    if project:
        xo_ref[...] = x_ref[...] + _dot(o_ref[...], wo_ref[...])


def _rwkv_chunk(arrs, p, s0, *, B, T, Tc, C, ppb, shared, unroll=1, x=None, w_o=None):
    NC = T // Tc
    NP = N_PAIR // ppb
    Wb = ppb * LANES
    project = x is not None
    assert not project or NP == 1
    tile = pl.BlockSpec((Tc, Wb), lambda b, q, c: (b * NC + c, q))
    vec = pl.BlockSpec((1, Wb), lambda b, q, c: (0, q))
    smap = (lambda b, q, c: (0, q, 0, 0)) if shared else (lambda b, q, c: (b, q, 0, 0))
    ins = [*arrs, p["k_k"], p["k_a"], p["r_k"], p["gn_g"], p["gn_b"], s0]
    specs = [tile] * 6 + [vec] * 5 + [pl.BlockSpec((1, ppb, LANES, LANES), smap)]
    if project:
        ins += [x, w_o]
        specs += [tile, pl.BlockSpec(w_o.shape, lambda b, q, c: (0, 0), pipeline_mode=pl.Buffered(1))]
    return pl.pallas_call(
        functools.partial(_rwkv_chunk_body, Tc=Tc, C=C, n_pair=ppb, unroll=unroll, project=project),
        grid=(B, NP, NC),
        in_specs=specs,
        out_specs=[tile, pl.BlockSpec((1, ppb, LANES, LANES), lambda b, q, c: (b, q, 0, 0))],
        out_shape=[jax.ShapeDtypeStruct((B * T, D_MODEL), F32 if project else BF16),
                   jax.ShapeDtypeStruct((B, N_PAIR, LANES, LANES), F32)],
        scratch_shapes=[pltpu.VMEM((Tc, D_MODEL), BF16)] if project else [],
        compiler_params=_cparams(("arbitrary", "arbitrary", "arbitrary")),
        name="rwkv_chunk",
    )(*ins)


def _rwkv_dec_body(r_ref, lw_ref, k_ref, v_ref, as_ref, g_ref, kk_ref, ka_ref, rk_ref, gg_ref, gb_ref, s0_ref,
                   o_ref, s_ref, r_scr, w_scr, k2_scr, v_scr, a_scr, b_scr, y_scr, *, nb, layer=None):
    s_ref = _own_layer(s_ref, layer)
    n_head = 2 * N_PAIR
    ri = lax.broadcasted_iota(jnp.int32, (HS_C, HS_C), 0)
    ci = lax.broadcasted_iota(jnp.int32, (HS_C, HS_C), 1)
    eye = (ri == ci).astype(F32)
    for h in range(n_head):
        ls = slice(h * HS_C, (h + 1) * HS_C)
        k, asig = k_ref[:, ls], as_ref[:, ls]
        kk = k * kk_ref[:, ls]
        kk = kk * lax.rsqrt(jnp.maximum(jnp.sum(kk * kk, axis=1, keepdims=True), 1e-24))
        a_scr[h] = -kk
        b_scr[h] = kk * asig
        k2_scr[h] = k * (1.0 + (asig - 1.0) * ka_ref[:, ls])
        r_scr[h] = r_ref[:, ls]
        v_scr[h] = v_ref[:, ls]
        w_scr[h] = jnp.exp(lw_ref[:, ls])

    def per_sample(s, carry):
        row = pl.ds(s, 1)
        heads = range(n_head)
        S = [s0_ref[s, h] for h in heads]
        sa = [jnp.sum(S[h] * a_scr[h, row, :], axis=1, keepdims=True) for h in heads]
        v_col = [jnp.sum(eye * v_scr[h, row, :], axis=1, keepdims=True) for h in heads]
        Sn = [S[h] * w_scr[h, row, :] + sa[h] * b_scr[h, row, :] + v_col[h] * k2_scr[h, row, :] for h in heads]
        for h in heads:
            s_ref[s, h] = Sn[h]
        y_col = [jnp.sum(Sn[h] * r_scr[h, row, :], axis=1, keepdims=True) for h in heads]
        for h in heads:
            y_scr[h, row, :] = jnp.sum(eye * y_col[h], axis=0, keepdims=True)
        return carry

    lax.fori_loop(0, nb, per_sample, 0)
    for h in range(n_head):
        ls = slice(h * HS_C, (h + 1) * HS_C)
        y = y_scr[h]
        d = y - jnp.mean(y, axis=1, keepdims=True)
        var = jnp.mean(d * d, axis=1, keepdims=True)
        on = d * lax.rsqrt(var + GN_EPS_C) * gg_ref[:, ls] + gb_ref[:, ls]
        bonus = jnp.sum(r_scr[h] * k2_scr[h] * rk_ref[:, ls], axis=1, keepdims=True) * v_scr[h]
        o_ref[:, ls] = (on + bonus) * g_ref[:, ls]


def _rwkv_dec(arrs, p, s0, *, nb, layer, prev=None):
    B = s0.shape[1]
    n_head = 2 * N_PAIR
    tile = pl.BlockSpec((nb, D_MODEL), lambda i: (i, 0))
    vec = pl.BlockSpec((1, D_MODEL), lambda i: (0, 0))
    sspec = pl.BlockSpec((None, nb, n_head, HS_C, HS_C), lambda i: (layer, i, 0, 0, 0))
    ins = [*arrs, p["k_k"], p["k_a"], p["r_k"], p["gn_g"], p["gn_b"], s0]
    specs = [tile] * 6 + [vec] * 5 + [sspec]
    s_out = sspec
    if prev is None:
        s_out = pl.BlockSpec((s0.shape[0], nb, n_head, HS_C, HS_C), lambda i: (0, i, 0, 0, 0))
    body = functools.partial(_rwkv_dec_body, nb=nb, layer=layer if prev is None else None)
    aliases = {}
    if prev is not None:
        ins.append(prev)
        specs.append(pl.BlockSpec(memory_space=pl.ANY))
        aliases = {len(ins) - 1: 1}
        body = _drop_ref(body, len(ins) - 1)
    return pl.pallas_call(
        body,
        grid=(B // nb,),
        in_specs=specs,
        out_specs=[tile, s_out],
        out_shape=[jax.ShapeDtypeStruct((B, D_MODEL), F32), jax.ShapeDtypeStruct(s0.shape, F32)],
        scratch_shapes=[pltpu.VMEM((n_head, nb, HS_C), F32)] * 7,
        input_output_aliases=aliases,
        compiler_params=_cparams(("arbitrary",)),
        name="rwkv_dec",
    )(*ins)


def _from_blockdiag(s):
    B = s.shape[0]
    return jnp.stack([s[:, :, :HS_C, :HS_C], s[:, :, HS_C:, HS_C:]], axis=2).reshape(B, 2 * N_PAIR, HS_C, HS_C)


MAIN_TN = 512
RET_C = 128
LRU_TC = 256
FFN_TC = 256
FFN_TM = 512
FFN_TCOL = 256
PREP_TM = 256
RWKV_TC = 256
RWKV_C = 64
RWKV_PPB = 8
RWKV_UNROLL = 2
RWKV_DEC_NB = 8
RET_DEC_NB = 8


def kernel(x_prompt, x_sample, state_ret, state_lru, state_lru_conv, state_rwkv, state_shift, state_ffn_conv,
           meta_tokens, norm_mix_g, norm_ffn_g, norm_final_g,
           ev_w_in, ev_ret_gn_g, ev_lru_conv_w, ev_lru_conv_b, ev_lru_wa, ev_lru_ba, ev_lru_wx, ev_lru_bx,
           ev_lru_lambda, ev_w_out,
           od_mix, od_w_r, od_w_k, od_w_v, od_w0, od_w1, od_w2, od_a0, od_a1, od_a2, od_v0, od_v1, od_v2,
           od_g1, od_g2, od_k_k, od_k_a, od_r_k, od_gn_g, od_gn_b, od_w_o,
           ff_w_up, ff_conv_w, ff_conv_b, ff_w_down):
    BP, TP, _ = x_prompt.shape
    BS = x_sample.shape[0]
    NS = BS + N_META
    bf = lambda a: a.astype(BF16)
    row = lambda a: a.reshape(1, -1)

    def lora_in(w):
        return bf(jnp.pad(w, ((0, 0), (0, LORA_PAD - w.shape[1]))))

    def lora_out(w):
        return bf(jnp.pad(w, ((0, LORA_PAD - w.shape[0]), (0, 0))))

    xm = x_prompt.reshape(BP * TP, D_MODEL)
    xs = jnp.concatenate([x_sample.reshape(BS, D_MODEL), meta_tokens.astype(x_prompt.dtype)], axis=0)

    cos_m, sin_m = _rope_tables(N_META + jnp.arange(TP, dtype=jnp.int32))
    cos_t, sin_t = _rope_tables(jnp.arange(N_META, dtype=jnp.int32))
    cos_s, sin_s = _rope_tables(PAST_LEN + jnp.arange(PAD_T, dtype=jnp.int32))

    out = {k: [] for k in ("ret_p", "lru_p", "lconv_p", "rwkv_p", "shift_p", "ffn_p",
                           "ret_s", "lru_s", "lconv_s", "rwkv_s", "shift_s", "ffn_s")}
    v_first_m = v_first_s = None
    ret_s_all = rwkv_s_all = None
    for li in range(4):
        j = li // 2
        if li % 2 == 0:
            w_in = bf(ev_w_in[j])
            w_out_a, w_out_b = bf(ev_w_out[j][:VG_W]), bf(ev_w_out[j][VG_W:])
            gn = ev_ret_gn_g[j]
            lp = dict(conv_w=ev_lru_conv_w[j], conv_b=row(ev_lru_conv_b[j]), wa=bf(ev_lru_wa[j]),
                      ba=row(ev_lru_ba[j]), wx=bf(ev_lru_wx[j]), bx=row(ev_lru_bx[j]), lam=row(ev_lru_lambda[j]))
            zs = _mm([xs], [w_in], g=norm_mix_g[li], tm=NS, tn=MAIN_TN)
            z_smp, z_meta = zs[:BS], zs[BS:]
            ya_t, ret_t = _retention(z_meta, cos_t, sin_t, jnp.zeros((1, H_A, DK_A, DV_A), F32), gn,
                                     B=1, T=N_META, C=N_META, c_true=N_META, shared=False)
            yb_t, lru_t, lconv_t = _lru_seq(z_meta, lp, jnp.zeros((1, 1, W_B), F32),
                                            jnp.zeros((1, CONV_B - 1, W_B), F32), B=1, T=N_META, Tc=N_META,
                                            shared=False)
            ya_s, ret_s_all = _retention(z_smp, cos_s, sin_s, state_ret, gn, B=BS, T=1, C=PAD_T, c_true=1,
                                         shared=False, bb=RET_DEC_NB, single=True, layer=j, prev=ret_s_all)
            yb_s, lru_s = _lru_dec(z_smp, lp, state_lru[j], state_lru_conv[j])
            lconv_s = jnp.concatenate([state_lru_conv[j][:, 1:], z_smp[:, None, 3 * W_B:4 * W_B]], axis=1)
            ya = jnp.concatenate([ya_s, ya_t.astype(F32)], axis=0)
            yb = jnp.concatenate([yb_s, yb_t], axis=0)
            xs = _mm([ya, yb], [w_out_a, w_out_b], res=xs, tm=NS, tn=MAIN_TN)
            xm, lru_m, lconv_m, ret_m = _even_fused(xm, norm_mix_g[li], w_in, lp, lru_t, lconv_t, cos_m, sin_m, ret_t,
                                                    gn, w_out_a, w_out_b, B=BP, T=TP, Tc=LRU_TC, tn=MAIN_TN,
                                                    C=RET_C)
            out["ret_p"].append(ret_m)
            out["lru_p"].append(lru_m[:, 0])
            out["lconv_p"].append(lconv_m)
            out["lru_s"].append(lru_s)
            out["lconv_s"].append(lconv_s)
        else:
            p = dict(mix=od_mix[j], w_r=bf(od_w_r[j]), w_k=bf(od_w_k[j]), w_v=bf(od_w_v[j]), w0=row(od_w0[j]),
                     w1=lora_in(od_w1[j]), w2=lora_out(od_w2[j]), a0=row(od_a0[j]), a1=lora_in(od_a1[j]),
                     a2=lora_out(od_a2[j]), g1=lora_in(od_g1[j]), g2=lora_out(od_g2[j]),
                     k_k=row(od_k_k[j]), k_a=row(od_k_a[j]), r_k=row(od_r_k[j]), gn_g=row(od_gn_g[j]),
                     gn_b=row(od_gn_b[j]))
            vp = None
            if j > 0:
                vp = dict(v0=row(od_v0[j - 1]), v1=lora_in(od_v1[j - 1]), v2=lora_out(od_v2[j - 1]))
            w_o = bf(od_w_o[j])
            gmix = norm_mix_g[li]
            pre_t = _rwkv_prep(xs[BS:], gmix, p, vp, None if vp is None else v_first_s[BS:],
                               jnp.zeros((1, 1, D_MODEL), F32), B=1, T=N_META, tm=N_META, carry=True)
            o_t, rw_t = _rwkv_chunk(pre_t[:6], p, jnp.zeros((1, N_PAIR, LANES, LANES), F32),
                                    B=1, T=N_META, Tc=N_META, C=N_META, ppb=RWKV_PPB, shared=False)
            shift_t = pre_t[6]
            pre_s = _rwkv_prep(xs[:BS], gmix, p, vp, None if vp is None else v_first_s[:BS],
                               state_shift[j], B=BS, T=1, tm=BS, carry=False)
            o_s, rwkv_s_all = _rwkv_dec(pre_s[:6], p, state_rwkv, nb=RWKV_DEC_NB, layer=j, prev=rwkv_s_all)
            if vp is None:
                v_first_s = jnp.concatenate([pre_s[3], pre_t[3]], axis=0)
            o_small = jnp.concatenate([o_s, o_t.astype(F32)], axis=0)
            xs = _mm([o_small], [w_o], res=xs, tm=NS, tn=MAIN_TN)
            pre_m = _rwkv_prep(xm, gmix, p, vp, v_first_m, shift_t, B=BP, T=TP, tm=PREP_TM, carry=True, shared=True)
            if vp is None:
                v_first_m = pre_m[3]
            xm, rw_m = _rwkv_chunk(pre_m[:6], p, rw_t, B=BP, T=TP, Tc=RWKV_TC, C=RWKV_C, ppb=RWKV_PPB, shared=True,
                                   unroll=RWKV_UNROLL, x=xm, w_o=w_o)
            out["rwkv_p"].append(_from_blockdiag(rw_m))
            out["shift_p"].append(pre_m[6][:, 0])
            out["shift_s"].append(pre_s[6])
        w_up, w_down = bf(ff_w_up[li]), bf(ff_w_down[li])
        cw, cb = ff_conv_w[li], row(ff_conv_b[li])
        us = _mm([xs], [w_up], g=norm_ffn_g[li], tm=NS, tn=MAIN_TN)
        m_t, ffn_t = _ffn_seq(us[BS:], cw, cb, jnp.zeros((1, CONV_F - 1, D_FF), F32), B=1, T=N_META, Tc=N_META,
                              shared=False)
        m_s = _ffn_dec(us[:BS], cw, cb, state_ffn_conv[li])
        ffn_s = jnp.concatenate([state_ffn_conv[li][:, 1:], us[:BS, None, :D_FF]], axis=1)
        xs = _mm([jnp.concatenate([m_s, m_t], axis=0)], [w_down], res=xs, tm=NS, tn=MAIN_TN)
        xm, ffn_m = _ffn_fused(xm, norm_ffn_g[li], w_up, cw, cb, w_down, ffn_t, B=BP, T=TP, tm=FFN_TM, tc=FFN_TCOL,
                               shared=True, g_out=norm_final_g if li == 3 else None)
        out["ffn_p"].append(ffn_m)
        out["ffn_s"].append(ffn_s)

    y_prompt = xm.reshape(BP, TP, D_MODEL)
    y_sample = _final_norm(xs, norm_final_g, NS)[:BS].reshape(BS, 1, D_MODEL)
    st = lambda k: jnp.stack(out[k])
    return (y_prompt, y_sample,
            st("ret_p"), st("lru_p"), st("lconv_p"), st("rwkv_p"), st("shift_p"), st("ffn_p"),
            ret_s_all, st("lru_s"), st("lconv_s"), rwkv_s_all, st("shift_s"), st("ffn_s"))
```
